```python
import math
import jax, jax.numpy as jnp
from jax import lax
import numpy as np

D_MODEL = 2048
BATCH = 32
SEQ = 256
DEPTH = 4
DEC_BATCH = 8
DEC_SEQ = 2048
PAST_LEN = 512

GRID_W = 64
N_GROUPS = 4
GROUP_WIDTH = 512
D_FF = 5632
N_MOD = 9
ROPE_THETA = 10000.0
Q_BLOCK = 128
EPS = 1e-6

MLA_HEADS = 4
MLA_NOPE = 128
MLA_ROPE = 64
MLA_V = 128
MLA_Q_LORA = 384
MLA_KV_LORA = 256

GQA_HEADS = 4
GQA_KV_HEADS = 2
GQA_HEAD_DIM = 128

MLSTM_HEADS = 4
MLSTM_HEAD_DIM = 128
MLSTM_CHUNK = 64

S5_CHANNELS = 512
S5_GROUP = 16
S5_GROUPS = S5_CHANNELS // S5_GROUP
S5_STATE = 64

IN_SIZES = (MLA_Q_LORA, MLA_KV_LORA, MLA_ROPE,
            GQA_HEADS * GQA_HEAD_DIM, GQA_KV_HEADS * GQA_HEAD_DIM, GQA_KV_HEADS * GQA_HEAD_DIM,
            MLSTM_HEADS * MLSTM_HEAD_DIM, MLSTM_HEADS * MLSTM_HEAD_DIM, MLSTM_HEADS * MLSTM_HEAD_DIM,
            MLSTM_HEADS * MLSTM_HEAD_DIM, 2 * 2 * MLSTM_HEADS,
            S5_CHANNELS)
IN_COLS = sum(IN_SIZES)

kernel_name = 'hybrid_diffusion_prefix_trunk_step'


def rmsnorm(x, g):
    xf = x.astype(jnp.float32)
    y = xf * lax.rsqrt(jnp.mean(xf * xf, axis=-1, keepdims=True) + EPS)
    return (y * g.astype(jnp.float32)).astype(x.dtype)


def swiglu(h, w13, w2):
    gate, up = jnp.split(h @ w13, 2, axis=-1)
    return (jax.nn.silu(gate) * up) @ w2


def grid_positions(n_tokens):
    rows = n_tokens // GRID_W
    row = jnp.repeat(jnp.arange(rows, dtype=jnp.float32), GRID_W)
    col = jnp.tile(jnp.arange(GRID_W, dtype=jnp.float32), rows)
    return row, col


def rope_1d(x, pos):
    d = x.shape[-1]
    inv = ROPE_THETA ** (-jnp.arange(0, d, 2, dtype=jnp.float32) / d)
    ang = pos[:, None] * inv[None, :]
    cos = jnp.cos(ang)[:, None, :]
    sin = jnp.sin(ang)[:, None, :]
    x1, x2 = jnp.split(x.astype(jnp.float32), 2, axis=-1)
    return jnp.concatenate([x1 * cos - x2 * sin, x2 * cos + x1 * sin], axis=-1).astype(x.dtype)


def axial_rope(x, pos):
    row, col = pos
    half = x.shape[-1] // 2
    return jnp.concatenate([rope_1d(x[..., :half], row), rope_1d(x[..., half:], col)], axis=-1)


def attention(q, k, v, scale):
    b, sq, kh, g, dk = q.shape
    nb = sq // Q_BLOCK
    q_blocks = jnp.moveaxis(q.reshape(b, nb, Q_BLOCK, kh, g, dk), 1, 0)

    def one_block(qb):
        s = jnp.einsum('bqhgd,bkhd->bhgqk', qb, k, preferred_element_type=jnp.float32) * scale
        p = jax.nn.softmax(s, axis=-1).astype(v.dtype)
        return jnp.einsum('bhgqk,bkhe->bqhge', p, v)

    out = lax.map(one_block, q_blocks)
    return jnp.moveaxis(out, 0, 1).reshape(b, sq, kh, g, v.shape[-1])


def mla_keys_values(ckv, kr, w_ukv):
    b, s, _ = ckv.shape
    kv = (ckv @ w_ukv).reshape(b, s, MLA_HEADS, MLA_NOPE + MLA_V)
    k_nope, v = jnp.split(kv, [MLA_NOPE], axis=-1)
    k_rope = jnp.broadcast_to(kr[:, :, None, :], (b, s, MLA_HEADS, MLA_ROPE)).astype(k_nope.dtype)
    return jnp.concatenate([k_nope, k_rope], axis=-1), v


def mlstm_chunkwise(q, k, v, log_i, log_f, c0, n0, m0):
    b, s, nh, d = q.shape
    nc = s // MLSTM_CHUNK

    def chunks(a):
        return jnp.moveaxis(a.astype(jnp.float32).reshape(b, nc, MLSTM_CHUNK, *a.shape[2:]), 1, 0)

    causal = jnp.tril(jnp.ones((MLSTM_CHUNK, MLSTM_CHUNK), dtype=bool))[None, :, :, None]

    def step(carry, xs):
        c, n, m = carry
        qc, kc, vc, li, lf = xs
        cum = jnp.cumsum(lf, axis=1)
        dmat = jnp.where(causal, cum[:, :, None, :] - cum[:, None, :, :] + li[:, None, :, :], -jnp.inf)
        inter = cum + m[:, None, :]
        m_t = jnp.maximum(inter, jnp.max(dmat, axis=2))
        w = jnp.exp(dmat - m_t[:, :, None, :])
        sc = jnp.einsum('bthd,bshd->btsh', qc, kc) * w
        inter_w = jnp.exp(inter - m_t)
        num = jnp.einsum('btsh,bshd->bthd', sc, vc) + inter_w[..., None] * jnp.einsum('bthk,bhkv->bthv', qc, c)
        den = jnp.sum(sc, axis=2) + inter_w * jnp.einsum('bthk,bhk->bth', qc, n)
        h = num / jnp.maximum(jnp.abs(den), jnp.exp(-m_t))[..., None]
        tot = cum[:, -1, :]
        g = tot[:, None, :] - cum + li
        m_new = jnp.maximum(tot + m, jnp.max(g, axis=1))
        decay = jnp.exp(tot + m - m_new)
        ws = jnp.exp(g - m_new[:, None, :])
        c_new = decay[:, :, None, None] * c + jnp.einsum('bsh,bshk,bshv->bhkv', ws, kc, vc)
        n_new = decay[:, :, None] * n + jnp.einsum('bsh,bshk->bhk', ws, kc)
        return (c_new, n_new, m_new), h

    carry0 = (c0.astype(jnp.float32), n0.astype(jnp.float32), m0.astype(jnp.float32))
    xs = (chunks(q), chunks(k), chunks(v), chunks(log_i), chunks(log_f))
    (c, n, m), hs = lax.scan(step, carry0, xs)
    return jnp.moveaxis(hs, 0, 1).reshape(b, s, nh, d), c, n, m


def mlstm_bidirectional(q, k, v, gates, init):
    c0, n0, m0 = init
    log_i = gates[:, :, :, 0].astype(jnp.float32)
    log_f = jax.nn.log_sigmoid(gates[:, :, :, 1].astype(jnp.float32))
    h_f, cf, nf, mf = mlstm_chunkwise(q, k, v, log_i[:, :, 0], log_f[:, :, 0], c0[:, 0], n0[:, 0], m0[:, 0])
    h_b, cb, nb, mb = mlstm_chunkwise(jnp.flip(q, 1), jnp.flip(k, 1), jnp.flip(v, 1),
                                      jnp.flip(log_i[:, :, 1], 1), jnp.flip(log_f[:, :, 1], 1),
                                      c0[:, 1], n0[:, 1], m0[:, 1])
    h = h_f + jnp.flip(h_b, 1)
    return h, (jnp.stack([cf, cb], 1), jnp.stack([nf, nb], 1), jnp.stack([mf, mb], 1))


def _linear_combine(left, right):
    a_l, b_l = left
    a_r, b_r = right
    return a_l * a_r, a_r * b_l + b_r


def s5_scan(u, a_re, a_im, log_dt, b_mat, x0):
    lam = lax.complex(a_re.astype(jnp.float32), a_im.astype(jnp.float32))
    dt = jnp.exp(log_dt.astype(jnp.float32))[:, None]
    a_bar = jnp.exp(lam * dt)
    b_bar = ((a_bar - 1.0) / lam)[:, :, None] * b_mat
    bu = jnp.einsum('bsgc,gpc->bsgp', u, b_bar)
    bu = bu.at[:, 0].add(a_bar[None] * x0)
    a_seq = jnp.broadcast_to(a_bar, bu.shape)
    _, xs = lax.associative_scan(_linear_combine, (a_seq, bu), axis=1)
    return xs


def s5_bidirectional(u, lp, init):
    b, s, _ = u.shape
    uf = u.astype(jnp.float32).reshape(b, s, S5_GROUPS, S5_GROUP)
    uc = uf.astype(jnp.complex64)
    bm = lax.complex(lp['s5_b_re'].astype(jnp.float32), lp['s5_b_im'].astype(jnp.float32))
    cm = lax.complex(lp['s5_c_re'].astype(jnp.float32), lp['s5_c_im'].astype(jnp.float32))
    x0 = lax.complex(init[0].astype(jnp.float32), init[1].astype(jnp.float32))
    a_re, a_im, log_dt = lp['s5_a_re'], lp['s5_a_im'], lp['s5_log_dt']
    xs_f = s5_scan(uc, a_re[0], a_im[0], log_dt[0], bm, x0[:, 0])
    xs_b = jnp.flip(s5_scan(jnp.flip(uc, 1), a_re[1], a_im[1], log_dt[1], bm, x0[:, 1]), 1)
    y = jnp.real(jnp.einsum('bsgp,gcp->bsgc', xs_f + xs_b, cm)) \
        + lp['s5_d'].astype(jnp.float32).reshape(S5_GROUPS, S5_GROUP) * uf
    y = jax.nn.gelu(y.reshape(b, s, S5_CHANNELS)).astype(u.dtype)
    out = y * jax.nn.sigmoid(y @ lp['s5_w_glu'])
    fin = jnp.stack([xs_f[:, -1], xs_b[:, 0]], axis=1)
    return out, (jnp.real(fin), jnp.imag(fin))


def mixer(h, lp, pos, ctx):
    b, s, _ = h.shape
    latent = ctx is not None
    split_at = np.cumsum(IN_SIZES)[:-1].tolist()
    (cq, ckv, kr, gq, gk, gv, mq, mk, mv, mo, mg, su) = jnp.split(h @ lp['w_in'], split_at, axis=-1)

    q_a = (rmsnorm(cq, lp['mla_q_norm']) @ lp['mla_w_uq']).reshape(b, s, MLA_HEADS, MLA_NOPE + MLA_ROPE)
    ckv = rmsnorm(ckv, lp['mla_kv_norm'])
    if latent:
        q_a = jnp.concatenate([q_a[..., :MLA_NOPE], axial_rope(q_a[..., MLA_NOPE:], pos)], axis=-1)
        k_a, v_a = mla_keys_values(ckv, axial_rope(kr[:, :, None, :], pos)[:, :, 0], lp['mla_w_ukv'])
        k_actx, v_actx = mla_keys_values(ctx[0], ctx[1], lp['mla_w_ukv'])
        k_a = jnp.concatenate([k_actx, k_a], axis=1)
        v_a = jnp.concatenate([v_actx, v_a], axis=1)
    else:
        k_a, v_a = mla_keys_values(ckv, kr, lp['mla_w_ukv'])
    o_a = attention(q_a[:, :, :, None, :], k_a, v_a, (MLA_NOPE + MLA_ROPE) ** -0.5)

    q_b = rmsnorm(gq.reshape(b, s, GQA_HEADS, GQA_HEAD_DIM), lp['gqa_q_norm'])
    k_b = rmsnorm(gk.reshape(b, s, GQA_KV_HEADS, GQA_HEAD_DIM), lp['gqa_k_norm'])
    v_b = gv.reshape(b, s, GQA_KV_HEADS, GQA_HEAD_DIM)
    if latent:
        q_b = axial_rope(q_b, pos)
        k_bs = jnp.concatenate([ctx[2].astype(k_b.dtype), axial_rope(k_b, pos)], axis=1)
        v_bs = jnp.concatenate([ctx[3].astype(v_b.dtype), v_b], axis=1)
    else:
        k_bs, v_bs = k_b, v_b
    q_bg = q_b.reshape(b, s, GQA_KV_HEADS, GQA_HEADS // GQA_KV_HEADS, GQA_HEAD_DIM)
    o_b = attention(q_bg, k_bs, v_bs, GQA_HEAD_DIM ** -0.5)

    heads = (b, s, MLSTM_HEADS, MLSTM_HEAD_DIM)
    gates = mg.reshape(b, s, 2, 2, MLSTM_HEADS) + lp['mlstm_gate_b']
    if latent:
        m_init = (ctx[4], ctx[5], ctx[6])
    else:
        m_init = (jnp.zeros((b, 2, MLSTM_HEADS, MLSTM_HEAD_DIM, MLSTM_HEAD_DIM), jnp.float32),
                  jnp.zeros((b, 2, MLSTM_HEADS, MLSTM_HEAD_DIM), jnp.float32),
                  jnp.zeros((b, 2, MLSTM_HEADS), jnp.float32))
    h_c, st_c = mlstm_bidirectional(mq.reshape(heads), mk.reshape(heads) * MLSTM_HEAD_DIM ** -0.5,
                                    mv.reshape(heads), gates, m_init)
    o_c = (jax.nn.sigmoid(mo.astype(jnp.float32)) * h_c.reshape(b, s, -1)).astype(h.dtype)

    if latent:
        s_init = (ctx[7], ctx[8])
    else:
        s_init = (jnp.zeros((b, 2, S5_GROUPS, S5_STATE), jnp.float32),
                  jnp.zeros((b, 2, S5_GROUPS, S5_STATE), jnp.float32))
    o_d, st_d = s5_bidirectional(su, lp, s_init)

    merged = jnp.concatenate([o_a.reshape(b, s, -1).astype(h.dtype), o_b.reshape(b, s, -1).astype(h.dtype),
                              o_c, o_d.astype(h.dtype)], axis=-1)
    merged = rmsnorm(merged.reshape(b, s, N_GROUPS, GROUP_WIDTH),
                     lp['out_norm'].reshape(N_GROUPS, GROUP_WIDTH)).reshape(b, s, N_GROUPS * GROUP_WIDTH)
    out = merged @ lp['w_out']
    if latent:
        return out, None
    return out, (ckv, kr, k_b, v_b, *st_c, *st_d)


def trunk_layer(x, mod, lp, pos, ctx):
    def m(j):
        return mod[:, None, j, :]
    h = rmsnorm(x, lp['norm_g'][0]) * (1.0 + m(1)) + m(0)
    x = x + 0.5 * m(2) * swiglu(h, lp['ffn_w13'][0], lp['ffn_w2'][0])
    h = rmsnorm(x, lp['norm_g'][1]) * (1.0 + m(4)) + m(3)
    mix, new_ctx = mixer(h, lp, pos, ctx)
    x = x + m(5) * mix
    h = rmsnorm(x, lp['norm_g'][2]) * (1.0 + m(7)) + m(6)
    x = x + 0.5 * m(8) * swiglu(h, lp['ffn_w13'][1], lp['ffn_w2'][1])
    return x, new_ctx


def setup_inputs(seed: int = 0) -> dict:
    key = jax.random.key(seed)
    keys = iter(jax.random.split(key, 64))
    f32 = jnp.float32

    def nrm(shape, scale=1.0):
        return scale * jax.random.normal(next(keys), shape, f32)

    def gain(shape):
        return 1.0 + 0.02 * jax.random.normal(next(keys), shape, f32)

    d = D_MODEL
    hm, dm = MLSTM_HEADS, MLSTM_HEAD_DIM
    inp = {}
    inp['x_prompt'] = nrm((BATCH, SEQ, d))
    inp['x_sample'] = nrm((DEC_BATCH, DEC_SEQ, d))
    inp['cache_mla_ckv'] = nrm((DEC_BATCH, DEPTH, PAST_LEN, MLA_KV_LORA))
    inp['cache_mla_krope'] = nrm((DEC_BATCH, DEPTH, PAST_LEN, MLA_ROPE))
    inp['cache_gqa_k'] = nrm((DEC_BATCH, DEPTH, PAST_LEN, GQA_KV_HEADS, GQA_HEAD_DIM))
    inp['cache_gqa_v'] = nrm((DEC_BATCH, DEPTH, PAST_LEN, GQA_KV_HEADS, GQA_HEAD_DIM))
    inp['state_mlstm_c'] = nrm((DEC_BATCH, DEPTH, 2, hm, dm, dm), 0.1)
    inp['state_mlstm_n'] = nrm((DEC_BATCH, DEPTH, 2, hm, dm), 0.1)
    inp['state_mlstm_m'] = nrm((DEC_BATCH, DEPTH, 2, hm), 0.5)
    inp['state_s5_re'] = nrm((DEC_BATCH, DEPTH, 2, S5_GROUPS, S5_STATE), 0.1)
    inp['state_s5_im'] = nrm((DEC_BATCH, DEPTH, 2, S5_GROUPS, S5_STATE), 0.1)
    inp['c'] = nrm((DEC_BATCH, d))
    inp['c_ctx'] = nrm((d,))
    inp['ada_w'] = nrm((DEPTH, d, N_MOD * d), 0.5 * d ** -0.5)
    inp['ada_b'] = nrm((DEPTH, N_MOD * d), 0.01)
    inp['norm_g'] = gain((DEPTH, 3, d))
    inp['ffn_w13'] = nrm((DEPTH, 2, d, 2 * D_FF), d ** -0.5)
    inp['ffn_w2'] = nrm((DEPTH, 2, D_FF, d), D_FF ** -0.5)
    inp['w_in'] = nrm((DEPTH, d, IN_COLS), d ** -0.5)
    inp['mla_q_norm'] = gain((DEPTH, MLA_Q_LORA))
    inp['mla_kv_norm'] = gain((DEPTH, MLA_KV_LORA))
    inp['mla_w_uq'] = nrm((DEPTH, MLA_Q_LORA, MLA_HEADS * (MLA_NOPE + MLA_ROPE)), MLA_Q_LORA ** -0.5)
    inp['mla_w_ukv'] = nrm((DEPTH, MLA_KV_LORA, MLA_HEADS * (MLA_NOPE + MLA_V)), MLA_KV_LORA ** -0.5)
    inp['gqa_q_norm'] = gain((DEPTH, GQA_HEAD_DIM))
    inp['gqa_k_norm'] = gain((DEPTH, GQA_HEAD_DIM))
    inp['mlstm_gate_b'] = jnp.concatenate(
        [nrm((DEPTH, 2, 1, hm), 0.1),
         jnp.linspace(3.0, 6.0, hm, dtype=f32) + nrm((DEPTH, 2, 1, hm), 0.1)], axis=2)
    inp['s5_a_re'] = -0.5 * gain((DEPTH, 2, S5_GROUPS, S5_STATE))
    inp['s5_a_im'] = jnp.pi * jnp.arange(S5_STATE, dtype=f32) + nrm((DEPTH, 2, S5_GROUPS, S5_STATE), 0.01)
    inp['s5_log_dt'] = jax.random.uniform(next(keys), (DEPTH, 2, S5_GROUPS), f32,
                                          math.log(1e-3), math.log(1e-1))
    inp['s5_b_re'] = nrm((DEPTH, S5_GROUPS, S5_STATE, S5_GROUP), (2 * S5_GROUP) ** -0.5)
    inp['s5_b_im'] = nrm((DEPTH, S5_GROUPS, S5_STATE, S5_GROUP), (2 * S5_GROUP) ** -0.5)
    inp['s5_c_re'] = nrm((DEPTH, S5_GROUPS, S5_GROUP, S5_STATE), S5_STATE ** -0.5)
    inp['s5_c_im'] = nrm((DEPTH, S5_GROUPS, S5_GROUP, S5_STATE), S5_STATE ** -0.5)
    inp['s5_d'] = nrm((DEPTH, S5_CHANNELS))
    inp['s5_w_glu'] = nrm((DEPTH, S5_CHANNELS, S5_CHANNELS), S5_CHANNELS ** -0.5)
    inp['out_norm'] = gain((DEPTH, N_GROUPS * GROUP_WIDTH))
    inp['w_out'] = nrm((DEPTH, N_GROUPS * GROUP_WIDTH, d), (N_GROUPS * GROUP_WIDTH) ** -0.5)
    inp['final_norm'] = gain((d,))
    return inp


def reference(x_prompt, x_sample, cache_mla_ckv, cache_mla_krope, cache_gqa_k, cache_gqa_v,
              state_mlstm_c, state_mlstm_n, state_mlstm_m, state_s5_re, state_s5_im,
              c, c_ctx, ada_w, ada_b, norm_g, ffn_w13, ffn_w2, w_in,
              mla_q_norm, mla_kv_norm, mla_w_uq, mla_w_ukv, gqa_q_norm, gqa_k_norm,
              mlstm_gate_b, s5_a_re, s5_a_im, s5_log_dt, s5_b_re, s5_b_im, s5_c_re, s5_c_im,
              s5_d, s5_w_glu, out_norm, w_out, final_norm):
    pos = grid_positions(x_sample.shape[1])
    x_ctx, x_lat = x_prompt, x_sample
    per_layer = []
    for l in range(DEPTH):
        lp = {'w_in': w_in[l], 'norm_g': norm_g[l], 'ffn_w13': ffn_w13[l], 'ffn_w2': ffn_w2[l],
              'mla_q_norm': mla_q_norm[l], 'mla_kv_norm': mla_kv_norm[l],
              'mla_w_uq': mla_w_uq[l], 'mla_w_ukv': mla_w_ukv[l],
              'gqa_q_norm': gqa_q_norm[l], 'gqa_k_norm': gqa_k_norm[l],
              'mlstm_gate_b': mlstm_gate_b[l],
              's5_a_re': s5_a_re[l], 's5_a_im': s5_a_im[l], 's5_log_dt': s5_log_dt[l],
              's5_b_re': s5_b_re[l], 's5_b_im': s5_b_im[l], 's5_c_re': s5_c_re[l], 's5_c_im': s5_c_im[l],
              's5_d': s5_d[l], 's5_w_glu': s5_w_glu[l], 'out_norm': out_norm[l], 'w_out': w_out[l]}
        mod_ctx = (jax.nn.silu(c_ctx[None, :]) @ ada_w[l] + ada_b[l]).reshape(1, N_MOD, D_MODEL)
        mod_lat = (jax.nn.silu(c) @ ada_w[l] + ada_b[l]).reshape(-1, N_MOD, D_MODEL)
        x_ctx, ctx_l = trunk_layer(x_ctx, mod_ctx, lp, None, None)
        per_layer.append(ctx_l)
        cached = (cache_mla_ckv[:, l], cache_mla_krope[:, l], cache_gqa_k[:, l], cache_gqa_v[:, l],
                  state_mlstm_c[:, l], state_mlstm_n[:, l], state_mlstm_m[:, l],
                  state_s5_re[:, l], state_s5_im[:, l])
        x_lat, _ = trunk_layer(x_lat, mod_lat, lp, pos, cached)
    (new_mla_ckv, new_mla_krope, new_gqa_k, new_gqa_v, new_mlstm_c, new_mlstm_n, new_mlstm_m,
     new_s5_re, new_s5_im) = [jnp.stack([t[i] for t in per_layer], axis=1).astype(x_prompt.dtype)
                              for i in range(9)]
    y_prompt = rmsnorm(x_ctx, final_norm)
    y_sample = rmsnorm(x_lat, final_norm)
    return (y_prompt, y_sample, new_mla_ckv, new_mla_krope, new_gqa_k, new_gqa_v,
            new_mlstm_c, new_mlstm_n, new_mlstm_m, new_s5_re, new_s5_im)
```

```python
import functools
import math

import jax
import jax.numpy as jnp
from jax import lax
from jax.experimental import pallas as pl
from jax.experimental.pallas import tpu as pltpu

F32 = jnp.float32
BF16 = jnp.bfloat16

EPS = 1e-6
ROPE_THETA = 10000.0
GRID_W = 64
N_MOD = 9

D_FF = 5632
MLA_HEADS, MLA_NOPE, MLA_ROPE, MLA_V = 4, 128, 64, 128
MLA_Q_LORA, MLA_KV_LORA = 384, 256
GQA_HEADS, GQA_KV_HEADS, GQA_HEAD_DIM = 4, 2, 128
MLSTM_HEADS, MLSTM_HEAD_DIM = 4, 128
S5_GROUPS, S5_GROUP, S5_STATE = 32, 16, 64
S5_CH = S5_GROUPS * S5_GROUP
S5_NS = S5_GROUPS * S5_STATE
GROUP_WIDTH = 512

LANE = 128
SUBLANE = 8
VMEM_LIMIT_BYTES = 56 * 1024 * 1024

_ORIG = dict(cq=(0, 384), ckv=(384, 256), kr=(640, 64), gq=(704, 512), gk=(1216, 256), gv=(1472, 256),
             mq=(1728, 512), mk=(2240, 512), mv=(2752, 512), mo=(3264, 512), mg=(3776, 16), su=(3792, 512))
_ORDER = ("mq", "mk", "mv", "mo", "gq", "su", "gk", "gv", "ckv", "cq", "kr", "mg")
_COL = {}
_off = 0
for _n in _ORDER:
    _COL[_n] = _off
    _off += _ORIG[_n][1]
PROJ_COLS = ((_off + LANE - 1) // LANE) * LANE
KRMG_BLOCK = _COL["kr"] // LANE
MG_LANE = _COL["mg"] - _COL["kr"]


def _cparams(sem):
    return pltpu.CompilerParams(dimension_semantics=sem, vmem_limit_bytes=VMEM_LIMIT_BYTES)


def _pick(n, pref):
    best = None
    for t in range(min(n, pref), 0, -1):
        if n % t == 0 and (t % SUBLANE == 0 or t == n):
            best = t
            break
    return best if best is not None else n


def _rms(x, g):
    return x * lax.rsqrt(jnp.mean(x * x, axis=-1, keepdims=True) + EPS) * g


def _modulate(x, g, scale, shift):
    return _rms(x, g) * (1.0 + scale) + shift


def _swap_halves(x, half):
    w = x.shape[-1]
    lane = lax.broadcasted_iota(jnp.int32, x.shape, x.ndim - 1)
    first = (lane & (2 * half - 1)) < half
    return jnp.where(first, pltpu.roll(x, w - half, axis=x.ndim - 1), pltpu.roll(x, half, axis=x.ndim - 1))


def _silu(x):
    return x * jax.nn.sigmoid(x)


def _mod_kernel(c_ref, w_ref, b_ref, o_ref):
    a = _silu(c_ref[...]).astype(BF16)
    o_ref[...] = jnp.dot(a, w_ref[...].astype(BF16), preferred_element_type=F32) + b_ref[...]


def _modulation(cvecs, ada_w, ada_b):
    depth, d, nm = ada_w.shape
    r = cvecs.shape[0]
    tn = _pick(nm, 1024)
    return pl.pallas_call(
        _mod_kernel,
        out_shape=jax.ShapeDtypeStruct((depth, r, nm), F32),
        grid=(depth, nm // tn),
        in_specs=[pl.BlockSpec((r, d), lambda l, j: (0, 0)),
                  pl.BlockSpec((None, d, tn), lambda l, j: (l, 0, j)),
                  pl.BlockSpec((None, 1, tn), lambda l, j: (l, 0, j))],
        out_specs=pl.BlockSpec((None, r, tn), lambda l, j: (l, 0, j)),
        compiler_params=_cparams(("parallel", "parallel")),
        name="adaln_mod",
    )(cvecs, ada_w, ada_b.reshape(depth, 1, nm))


def _rope_kernel(ca_ref, sa_ref, cb_ref, sb_ref, *, log2w):
    s = ca_ref.shape[0]
    t = lax.broadcasted_iota(jnp.int32, (s, 1), 0)
    row = lax.shift_right_logical(t, log2w).astype(F32)
    col = (t & ((1 << log2w) - 1)).astype(F32)

    def tables(width, off, rd):
        lane = lax.broadcasted_iota(jnp.int32, (1, width), 1)
        r = lane - off
        inr = (r >= 0) & (r < rd)
        half, quarter = rd // 2, rd // 4
        is_col = r >= half
        rr = jnp.where(is_col, r - half, r)
        second = rr >= quarter
        j = jnp.where(second, rr - quarter, rr).astype(F32)
        inv = jnp.exp(j * (-2.0 / half * math.log(ROPE_THETA)))
        ang = jnp.where(is_col, col, row) * inv
        sign = jnp.where(second, 1.0, -1.0)
        return jnp.where(inr, jnp.cos(ang), 1.0), jnp.where(inr, sign * jnp.sin(ang), 0.0)

    ca, sa = tables(2 * LANE, MLA_NOPE, MLA_ROPE)
    cb, sb = tables(GQA_HEAD_DIM, 0, GQA_HEAD_DIM)
    ca_ref[...] = ca
    sa_ref[...] = sa
    cb_ref[...] = cb
    sb_ref[...] = sb


def _rope_tables(s):
    log2w = GRID_W.bit_length() - 1
    assert 1 << log2w == GRID_W
    shp = lambda w: jax.ShapeDtypeStruct((s, w), F32)
    return pl.pallas_call(
        functools.partial(_rope_kernel, log2w=log2w),
        out_shape=(shp(2 * LANE), shp(2 * LANE), shp(GQA_HEAD_DIM), shp(GQA_HEAD_DIM)),
        name="rope_tables",
    )()


def _s5_param_kernel(are_ref, aim_ref, ldt_ref, bre_ref, bim_ref, oar_ref, oai_ref, obr_ref, obi_ref):
    lr, li = are_ref[...], aim_ref[...]
    dt = jnp.exp(ldt_ref[...])
    mag = jnp.exp(lr * dt)
    ar, ai = mag * jnp.cos(li * dt), mag * jnp.sin(li * dt)
    oar_ref[...] = ar
    oai_ref[...] = ai
    nr, ni = ar - 1.0, ai
    den = lr * lr + li * li
    fr, fi = (nr * lr + ni * li) / den, (ni * lr - nr * li) / den
    br, bi = bre_ref[...], bim_ref[...]
    for d in range(fr.shape[0]):
        obr_ref[d] = fr[d:d + 1] * br - fi[d:d + 1] * bi
        obi_ref[d] = fr[d:d + 1] * bi + fi[d:d + 1] * br


def _s5_params(a_re, a_im, log_dt, b_re, b_im, c_re, c_im):
    nd = a_re.shape[0]
    are = a_re.reshape(nd, S5_NS)
    aim = a_im.reshape(nd, S5_NS)
    ldt = jnp.broadcast_to(log_dt[:, :, None], (nd, S5_GROUPS, S5_STATE)).reshape(nd, S5_NS)
    bre = jnp.transpose(b_re, (2, 0, 1)).reshape(S5_GROUP, S5_NS)
    bim = jnp.transpose(b_im, (2, 0, 1)).reshape(S5_GROUP, S5_NS)
    v = jax.ShapeDtypeStruct((nd, S5_NS), F32)
    m = jax.ShapeDtypeStruct((nd, S5_GROUP, S5_NS), F32)
    ar, ai, bbr, bbi = pl.pallas_call(_s5_param_kernel, out_shape=(v, v, m, m), name="s5_discretise")(
        are, aim, ldt, bre, bim)
    eye = jnp.eye(S5_GROUPS, dtype=F32)

    def dense_b(bb):
        bb = bb.reshape(nd, S5_GROUP, S5_GROUPS, S5_STATE)
        return jnp.einsum("dcgp,gh->dgchp", bb, eye).reshape(nd, S5_CH, S5_NS)

    def dense_c(cc):
        return jnp.einsum("gcp,gh->hpgc", cc, eye).reshape(S5_NS, S5_CH)

    bd = jnp.concatenate([dense_b(bbr), dense_b(bbi)], axis=-1).astype(BF16)
    cd = jnp.concatenate([dense_c(c_re), -dense_c(c_im)], axis=0).astype(BF16)
    abar = jnp.concatenate([ar, ai], axis=-1).reshape(nd, 1, 2 * S5_NS)
    return abar, bd, cd


def _ffn_kernel(x_ref, mod_ref, g_ref, w1_ref, w3_ref, w2_ref, o_ref, h_ref, *, row0):
    j = pl.program_id(1)

    @pl.when(j == 0)
    def _():
        h = _modulate(x_ref[...], g_ref[...], mod_ref[row0 + 1:row0 + 2, :], mod_ref[row0:row0 + 1, :])
        h_ref[...] = h.astype(BF16)
        o_ref[...] = jnp.zeros_like(o_ref)

    h = h_ref[...]
    gate = jnp.dot(h, w1_ref[...], preferred_element_type=F32)
    up = jnp.dot(h, w3_ref[...], preferred_element_type=F32)
    act = (_silu(gate) * up).astype(BF16)
    o_ref[...] += jnp.dot(act, w2_ref[...], preferred_element_type=F32)

    @pl.when(j == pl.num_programs(1) - 1)
    def _():
        o_ref[...] = x_ref[...] + (0.5 * mod_ref[row0 + 2:row0 + 3, :]) * o_ref[...]


def _ffn(x, mod, g, w13, w2, *, seq, row0, tm_pref=512, tf_pref=512):
    n, d = x.shape
    groups = mod.shape[0]
    tm = _pick(seq if groups > 1 else n, tm_pref)
    tf = _pick(D_FF, tf_pref)
    nf = D_FF // tf
    per = seq // tm if groups > 1 else 1
    mod_map = (lambda i, j: (i // per, 0, 0)) if groups > 1 else (lambda i, j: (0, 0, 0))
    return pl.pallas_call(
        functools.partial(_ffn_kernel, row0=row0),
        out_shape=jax.ShapeDtypeStruct((n, d), F32),
        grid=(n // tm, nf),
        in_specs=[pl.BlockSpec((tm, d), lambda i, j: (i, 0)),
                  pl.BlockSpec((None, N_MOD, d), mod_map),
                  pl.BlockSpec((1, d), lambda i, j: (0, 0)),
                  pl.BlockSpec((d, tf), lambda i, j: (0, j)),
                  pl.BlockSpec((d, tf), lambda i, j: (0, nf + j)),
                  pl.BlockSpec((tf, d), lambda i, j: (j, 0))],
        out_specs=pl.BlockSpec((tm, d), lambda i, j: (i, 0)),
        scratch_shapes=[pltpu.VMEM((tm, d), BF16)],
        compiler_params=_cparams(("parallel", "arbitrary")),
        name="ffn",
    )(x, mod, g, w13, w13, w2)


def _inproj_kernel(x_ref, mod_ref, g_ref, w_ref, o_ref):
    h = _modulate(x_ref[...], g_ref[...], mod_ref[4:5, :], mod_ref[3:4, :]).astype(BF16)
    o_ref[...] = jnp.dot(h, w_ref[...], preferred_element_type=F32)


def _inproj(x, mod, g, w, *, seq, tm_pref=512):
    n, d = x.shape
    groups = mod.shape[0]
    tm = _pick(seq if groups > 1 else n, tm_pref)
    tn = PROJ_COLS // 2
    per = seq // tm if groups > 1 else 1
    mod_map = (lambda jc, i: (i // per, 0, 0)) if groups > 1 else (lambda jc, i: (0, 0, 0))
    return pl.pallas_call(
        _inproj_kernel,
        out_shape=jax.ShapeDtypeStruct((n, PROJ_COLS), F32),
        grid=(PROJ_COLS // tn, n // tm),
        in_specs=[pl.BlockSpec((tm, d), lambda jc, i: (i, 0)),
                  pl.BlockSpec((None, N_MOD, d), mod_map),
                  pl.BlockSpec((1, d), lambda jc, i: (0, 0)),
                  pl.BlockSpec((d, tn), lambda jc, i: (0, jc))],
        out_specs=pl.BlockSpec((tm, tn), lambda jc, i: (i, jc)),
        compiler_params=_cparams(("parallel", "parallel")),
        name="in_proj",
    )(x, mod, g, w)


def _mla_kv(ckv_n, kr128, wukv_ref, ka_ref, va_ref):
    kv = jnp.dot(ckv_n.astype(BF16), wukv_ref[...], preferred_element_type=F32)
    krb = kr128.astype(BF16)
    hw = MLA_NOPE + MLA_V
    for h in range(MLA_HEADS):
        ka_ref[:, h * 2 * LANE:h * 2 * LANE + MLA_NOPE] = kv[:, h * hw:h * hw + MLA_NOPE].astype(BF16)
        ka_ref[:, h * 2 * LANE + MLA_NOPE:(h + 1) * 2 * LANE] = krb
        va_ref[:, h * MLA_V:(h + 1) * MLA_V] = kv[:, h * hw + MLA_NOPE:(h + 1) * hw].astype(BF16)


def _attn_prep_kernel(*refs, rope):
    if rope:
        (cq_ref, ckv_ref, krmg_ref, gq_ref, gk_ref, gv_ref, ca_ref, sa_ref, cb_ref, sb_ref,
         qng_ref, kvng_ref, wuq_ref, wukv_ref, gqg_ref, gkg_ref,
         qa_ref, ka_ref, va_ref, ckvn_ref, qb_ref, kb_ref, vb_ref, kbn_ref) = refs
    else:
        (cq_ref, ckv_ref, krmg_ref, gq_ref, gk_ref, gv_ref,
         qng_ref, kvng_ref, wuq_ref, wukv_ref, gqg_ref, gkg_ref,
         qa_ref, ka_ref, va_ref, ckvn_ref, qb_ref, kb_ref, vb_ref, kbn_ref) = refs

    cqn = _rms(cq_ref[...], qng_ref[...]).astype(BF16)
    qa = jnp.dot(cqn, wuq_ref[...], preferred_element_type=F32)
    if rope:
        ca = jnp.concatenate([ca_ref[...]] * MLA_HEADS, axis=1)
        sa = jnp.concatenate([sa_ref[...]] * MLA_HEADS, axis=1)
        qa = qa * ca + _swap_halves(qa, MLA_ROPE // 4) * sa
    qa_ref[...] = (qa * ((MLA_NOPE + MLA_ROPE) ** -0.5)).astype(BF16)

    ckv_n = _rms(ckv_ref[...], kvng_ref[...])
    ckvn_ref[...] = ckv_n
    krmg = krmg_ref[...]
    lane = lax.broadcasted_iota(jnp.int32, krmg.shape, 1)
    if rope:
        krmg = krmg * ca_ref[:, LANE:] + _swap_halves(krmg, MLA_ROPE // 4) * sa_ref[:, LANE:]
    kr128 = jnp.where(lane < MLA_ROPE, krmg, 0.0)
    _mla_kv(ckv_n, kr128, wukv_ref, ka_ref, va_ref)

    hd = GQA_HEAD_DIM
    gq, gk = gq_ref[...], gk_ref[...]
    for h in range(GQA_HEADS):
        q = _rms(gq[:, h * hd:(h + 1) * hd], gqg_ref[...])
        if rope:
            q = q * cb_ref[...] + _swap_halves(q, hd // 4) * sb_ref[...]
        qb_ref[:, h * hd:(h + 1) * hd] = (q * (hd ** -0.5)).astype(BF16)
    for h in range(GQA_KV_HEADS):
        k = _rms(gk[:, h * hd:(h + 1) * hd], gkg_ref[...])
        kbn_ref[:, h * hd:(h + 1) * hd] = k
        if rope:
            k = k * cb_ref[...] + _swap_halves(k, hd // 4) * sb_ref[...]
        kb_ref[:, h * hd:(h + 1) * hd] = k.astype(BF16)
    vb_ref[...] = gv_ref[...].astype(BF16)


def _attn_prep(proj, tables, lp, *, seq, tm_pref=512):
    n = proj.shape[0]
    tm = _pick(seq, tm_pref)
    per = seq // tm
    rope = tables is not None

    def pblock(name, width):
        idx = _COL[name] // width
        assert idx * width == _COL[name]
        return pl.BlockSpec((tm, width), lambda i: (i, idx))

    full = lambda a: pl.BlockSpec(a.shape, lambda i: (0,) * a.ndim)
    in_specs = [pblock("cq", MLA_Q_LORA), pblock("ckv", MLA_KV_LORA),
                pl.BlockSpec((tm, LANE), lambda i: (i, KRMG_BLOCK)),
                pblock("gq", 512), pblock("gk", 256), pblock("gv", 256)]
    args = [proj] * 6
    if rope:
        in_specs += [pl.BlockSpec((tm, t.shape[1]), lambda i: (i % per, 0)) for t in tables]
        args += list(tables)
    ws = [lp["mla_q_norm"], lp["mla_kv_norm"], lp["mla_w_uq"], lp["mla_w_ukv"], lp["gqa_q_norm"], lp["gqa_k_norm"]]
    in_specs += [full(w) for w in ws]
    args += ws
    widths = [(4 * 2 * LANE, BF16), (4 * 2 * LANE, BF16), (4 * MLA_V, BF16), (MLA_KV_LORA, F32),
              (GQA_HEADS * GQA_HEAD_DIM, BF16), (GQA_KV_HEADS * GQA_HEAD_DIM, BF16),
              (GQA_KV_HEADS * GQA_HEAD_DIM, BF16), (GQA_KV_HEADS * GQA_HEAD_DIM, F32)]
    return pl.pallas_call(
        functools.partial(_attn_prep_kernel, rope=rope),
        out_shape=tuple(jax.ShapeDtypeStruct((n, w), dt) for w, dt in widths),
        grid=(n // tm,),
        in_specs=in_specs,
        out_specs=tuple(pl.BlockSpec((tm, w), lambda i: (i, 0)) for w, _ in widths),
        compiler_params=_cparams(("parallel",)),
        name="attn_prep",
    )(*args)


def _cache_kv_kernel(ckv_ref, kr_ref, wukv_ref, ka_ref, va_ref):
    _mla_kv(ckv_ref[...], kr_ref[...], wukv_ref, ka_ref, va_ref)


def _cache_kv(ckv, kr128, wukv, tm_pref=512):
    n = ckv.shape[0]
    tm = _pick(n, tm_pref)
    return pl.pallas_call(
        _cache_kv_kernel,
        out_shape=(jax.ShapeDtypeStruct((n, 4 * 2 * LANE), BF16), jax.ShapeDtypeStruct((n, 4 * MLA_V), BF16)),
        grid=(n // tm,),
        in_specs=[pl.BlockSpec((tm, MLA_KV_LORA), lambda i: (i, 0)),
                  pl.BlockSpec((tm, LANE), lambda i: (i, 0)),
                  pl.BlockSpec(wukv.shape, lambda i: (0, 0))],
        out_specs=(pl.BlockSpec((tm, 4 * 2 * LANE), lambda i: (i, 0)),
                   pl.BlockSpec((tm, 4 * MLA_V), lambda i: (i, 0))),
        compiler_params=_cparams(("parallel",)),
        name="mla_cache_kv",
    )(ckv, kr128, wukv)


def _attn_kernel(q_ref, *refs, nseg):
    o_ref = refs[-1]
    q = q_ref[...]
    ss = [lax.dot_general(q, refs[2 * i][...].astype(BF16), (((1,), (1,)), ((), ())),
                          preferred_element_type=F32) for i in range(nseg)]
    m = functools.reduce(jnp.maximum, [jnp.max(s, axis=-1, keepdims=True) for s in ss])
    ps = [jnp.exp(s - m) for s in ss]
    l = functools.reduce(jnp.add, [jnp.sum(p, axis=-1, keepdims=True) for p in ps])
    o = functools.reduce(jnp.add, [jnp.dot(p.astype(BF16), refs[2 * i + 1][...].astype(BF16),
                                           preferred_element_type=F32) for i, p in enumerate(ps)])
    o_ref[...] = o / l


def _attention(q, segs, *, heads, kv_heads, dk, dv, tq_pref=512):
    b, sq, _ = q.shape
    tq = _pick(sq, tq_pref)
    g = heads // kv_heads
    in_specs = [pl.BlockSpec((None, tq, dk), lambda bi, h, i: (bi, i, h))]
    args = [q]
    for k, v in segs:
        sk = k.shape[1]
        in_specs += [pl.BlockSpec((None, sk, dk), lambda bi, h, i: (bi, 0, h // g)),
                     pl.BlockSpec((None, sk, dv), lambda bi, h, i: (bi, 0, h // g))]
        args += [k, v]
    return pl.pallas_call(
        functools.partial(_attn_kernel, nseg=len(segs)),
        out_shape=jax.ShapeDtypeStruct((b, sq, heads * dv), F32),
        grid=(b, heads, sq // tq),
        in_specs=in_specs,
        out_specs=pl.BlockSpec((None, tq, dv), lambda bi, h, i: (bi, i, h)),
        compiler_params=_cparams(("parallel", "parallel", "parallel")),
        name="attention",
    )(*args)


def _log_sigmoid(x):
    return jnp.minimum(x, 0.0) - jnp.log1p(jnp.exp(-jnp.abs(x)))


def _mlstm_kernel(*refs, rev, direction, has_init, emit_state):
    q_ref, k_ref, v_ref, gate_ref, bias_ref = refs[:5]
    pos = 5
    if has_init:
        c0_ref, n0_ref, m0_ref = refs[pos:pos + 3]
        pos += 3
    h_ref = refs[pos]
    pos += 1
    if emit_state:
        co_ref, no_ref, mo_ref = refs[pos:pos + 3]
        pos += 3
    c_s, n_s, m_s = refs[pos:pos + 3]

    c = pl.program_id(1)
    nh, hd = MLSTM_HEADS, MLSTM_HEAD_DIM
    cl = q_ref.shape[0]

    @pl.when(c == 0)
    def _():
        if has_init:
            c_s[...] = c0_ref[...]
            n_s[...] = n0_ref[...]
            m_s[...] = m0_ref[...]
        else:
            c_s[...] = jnp.zeros_like(c_s)
            n_s[...] = jnp.zeros_like(n_s)
            m_s[...] = jnp.zeros_like(m_s)

    g = gate_ref[...] + bias_ref[...]
    gt = g.T
    lf = _log_sigmoid(g)
    ri = lax.broadcasted_iota(jnp.int32, (cl, cl), 0)
    ci = lax.broadcasted_iota(jnp.int32, (cl, cl), 1)
    mask = (ri <= ci) if rev else (ri >= ci)
    cum = jnp.dot(mask.astype(F32), lf, preferred_element_type=F32, precision=lax.Precision.HIGHEST)
    cumt = cum.T
    last = 0 if rev else cl - 1
    li0 = MG_LANE + direction * 2 * nh
    lf0 = li0 + nh

    for h in range(nh):
        cum_c = cum[:, lf0 + h:lf0 + h + 1]
        cum_r = cumt[lf0 + h:lf0 + h + 1, :]
        li_r = gt[li0 + h:li0 + h + 1, :]
        li_c = g[:, li0 + h:li0 + h + 1]
        m_prev = m_s[h:h + 1, 0:1]
        dmat = jnp.where(mask, cum_c - cum_r + li_r, -jnp.inf)
        inter = cum_c + m_prev
        m_t = jnp.maximum(inter, jnp.max(dmat, axis=1, keepdims=True))
        w = jnp.exp(dmat - m_t)
        qh = q_ref[:, h * hd:(h + 1) * hd].astype(BF16)
        kh = k_ref[:, h * hd:(h + 1) * hd] * (hd ** -0.5)
        vh = v_ref[:, h * hd:(h + 1) * hd].astype(BF16)
        kt = kh.T
        sc = jnp.dot(qh, kt.astype(BF16), preferred_element_type=F32) * w
        inter_w = jnp.exp(inter - m_t)
        cmat = c_s[h]
        nrow = n_s[h:h + 1, :]
        num = (jnp.dot(sc.astype(BF16), vh, preferred_element_type=F32)
               + inter_w * jnp.dot(qh, cmat.astype(BF16), preferred_element_type=F32))
        qn = jnp.sum(q_ref[:, h * hd:(h + 1) * hd] * nrow, axis=1, keepdims=True)
        den = jnp.sum(sc, axis=1, keepdims=True) + inter_w * qn
        h_ref[:, h * hd:(h + 1) * hd] = num / jnp.maximum(jnp.abs(den), jnp.exp(-m_t))

        tot = cum_c[last:last + 1, :]
        g_r = tot - cum_r + li_r
        g_c = tot - cum_c + li_c
        m_new = jnp.maximum(tot + m_prev, jnp.max(g_r, axis=1, keepdims=True))
        decay = jnp.exp(tot + m_prev - m_new)
        ws_r = jnp.exp(g_r - m_new)
        ws_c = jnp.exp(g_c - m_new)
        c_s[h] = decay * cmat + jnp.dot((kt * ws_r).astype(BF16), vh, preferred_element_type=F32)
        n_s[h:h + 1, :] = decay * nrow + jnp.sum(kh * ws_c, axis=0, keepdims=True)
        m_s[h:h + 1, :] = jnp.broadcast_to(m_new, (1, hd))

    if emit_state:
        @pl.when(c == pl.num_programs(1) - 1)
        def _():
            co_ref[...] = c_s[...]
            no_ref[...] = n_s[...]
            mo_ref[...] = m_s[...]


def _mlstm_dir(proj, bias128, init, *, batch, seq, direction, emit_state, chunk_pref=256):
    n = proj.shape[0]
    cl = _pick(seq, chunk_pref)
    nc = seq // cl
    rev = direction == 1
    nh, hd = MLSTM_HEADS, MLSTM_HEAD_DIM
    row = (lambda b, c: b * nc + (nc - 1 - c)) if rev else (lambda b, c: b * nc + c)
    qkv = lambda name: pl.BlockSpec((cl, nh * hd), lambda b, c: (row(b, c), _COL[name] // (nh * hd)))
    in_specs = [qkv("mq"), qkv("mk"), qkv("mv"),
                pl.BlockSpec((cl, LANE), lambda b, c: (row(b, c), KRMG_BLOCK)),
                pl.BlockSpec((1, LANE), lambda b, c: (0, 0))]
    args = [proj, proj, proj, proj, bias128]
    has_init = init is not None
    if has_init:
        c0, n0, m0 = init
        in_specs += [pl.BlockSpec((None, None, nh, hd, hd), lambda b, c: (b, direction, 0, 0, 0)),
                     pl.BlockSpec((None, None, nh, hd), lambda b, c: (b, direction, 0, 0)),
                     pl.BlockSpec((None, None, nh, hd), lambda b, c: (b, direction, 0, 0))]
        args += [c0, n0, m0]
    out_shape = [jax.ShapeDtypeStruct((n, nh * hd), F32)]
    out_specs = [pl.BlockSpec((cl, nh * hd), lambda b, c: (row(b, c), 0))]
    if emit_state:
        out_shape += [jax.ShapeDtypeStruct((batch, nh, hd, hd), F32), jax.ShapeDtypeStruct((batch, nh, hd), F32),
                      jax.ShapeDtypeStruct((batch, nh, hd), F32)]
        out_specs += [pl.BlockSpec((None, nh, hd, hd), lambda b, c: (b, 0, 0, 0)),
                      pl.BlockSpec((None, nh, hd), lambda b, c: (b, 0, 0)),
                      pl.BlockSpec((None, nh, hd), lambda b, c: (b, 0, 0))]
    return pl.pallas_call(
        functools.partial(_mlstm_kernel, rev=rev, direction=direction, has_init=has_init, emit_state=emit_state),
        out_shape=tuple(out_shape),
        grid=(batch, nc),
        in_specs=in_specs,
        out_specs=tuple(out_specs),
        scratch_shapes=[pltpu.VMEM((nh, hd, hd), F32), pltpu.VMEM((nh, hd), F32), pltpu.VMEM((nh, hd), F32)],
        compiler_params=_cparams(("parallel", "arbitrary")),
        name="mlstm",
    )(*args)


def _s5_kernel(*refs, batch, has_init):
    if has_init:
        u_ref, bd_ref, a_ref, cd_ref, x0_ref, y_ref, xf_ref, bu_s, x_s = refs
    else:
        u_ref, bd_ref, a_ref, cd_ref, y_ref, xf_ref, bu_s, x_s = refs
    d = pl.program_id(0)
    c = pl.program_id(1)
    ns = S5_NS
    tsteps = u_ref.shape[0] // batch
    sg = min(SUBLANE, batch)

    @pl.when(c == 0)
    def _():
        if has_init:
            x_s[...] = x0_ref[...]
        else:
            x_s[...] = jnp.zeros_like(x_s)

    bu_s[...] = jnp.dot(u_ref[...].astype(BF16), bd_ref[...], preferred_element_type=F32)
    a_re = jnp.broadcast_to(a_ref[:, :ns], (sg, ns))
    a_im = jnp.broadcast_to(a_ref[:, ns:], (sg, ns))

    for grp in range(batch // sg):
        def body(t, carry):
            xr, xi = carry
            tt = t + d * (tsteps - 1 - 2 * t)
            r0 = pl.multiple_of(tt * batch + grp * sg, sg)
            br = bu_s[pl.ds(r0, sg), :ns]
            bi = bu_s[pl.ds(r0, sg), ns:]
            nr = a_re * xr - a_im * xi + br
            ni = a_re * xi + a_im * xr + bi
            bu_s[pl.ds(r0, sg), :ns] = nr
            bu_s[pl.ds(r0, sg), ns:] = ni
            return nr, ni

        x0 = (x_s[grp * sg:(grp + 1) * sg, :ns], x_s[grp * sg:(grp + 1) * sg, ns:])
        xr, xi = lax.fori_loop(0, tsteps, body, x0)
        x_s[grp * sg:(grp + 1) * sg, :ns] = xr
        x_s[grp * sg:(grp + 1) * sg, ns:] = xi

    y_ref[...] = jnp.dot(bu_s[...].astype(BF16), cd_ref[...], preferred_element_type=F32)

    @pl.when(c == pl.num_programs(1) - 1)
    def _():
        xf_ref[...] = x_s[...]


def _s5(u_tm, abar, bd, cd, x0, *, batch, seq, rows_pref=512):
    tsteps = _pick(seq, max(1, rows_pref // batch))
    rows = tsteps * batch
    nc = seq // tsteps
    cpos = lambda d, c: c + d * (nc - 1 - 2 * c)
    has_init = x0 is not None
    in_specs = [pl.BlockSpec((rows, S5_CH), lambda d, c: (cpos(d, c), 0)),
                pl.BlockSpec((None, S5_CH, 2 * S5_NS), lambda d, c: (d, 0, 0)),
                pl.BlockSpec((None, 1, 2 * S5_NS), lambda d, c: (d, 0, 0)),
                pl.BlockSpec((2 * S5_NS, S5_CH), lambda d, c: (0, 0))]
    args = [u_tm, bd, abar, cd]
    if has_init:
        in_specs.append(pl.BlockSpec((None, batch, 2 * S5_NS), lambda d, c: (d, 0, 0)))
        args.append(x0)
    return pl.pallas_call(
        functools.partial(_s5_kernel, batch=batch, has_init=has_init),
        out_shape=(jax.ShapeDtypeStruct((2, seq * batch, S5_CH), F32),
                   jax.ShapeDtypeStruct((2, batch, 2 * S5_NS), F32)),
        grid=(2, nc),
        in_specs=in_specs,
        out_specs=(pl.BlockSpec((None, rows, S5_CH), lambda d, c: (d, cpos(d, c), 0)),
                   pl.BlockSpec((None, batch, 2 * S5_NS), lambda d, c: (d, 0, 0))),
        scratch_shapes=[pltpu.VMEM((rows, 2 * S5_NS), F32), pltpu.VMEM((batch, 2 * S5_NS), F32)],
        compiler_params=_cparams(("parallel", "arbitrary")),
        name="s5_scan",
    )(*args)


def _gelu_tanh(x):
    return 0.5 * x * (1.0 + jnp.tanh(math.sqrt(2.0 / math.pi) * (x + 0.044715 * (x * x * x))))


def _outproj_kernel(x_ref, mod_ref, oa_ref, ob_ref, hf_ref, hb_ref, mo_ref, yf_ref, yb_ref, su_ref,
                    d_ref, wglu_ref, on_ref, wout_ref, o_ref):
    oc = jax.nn.sigmoid(mo_ref[...]) * (hf_ref[...] + hb_ref[...])
    y = _gelu_tanh(yf_ref[...] + yb_ref[...] + d_ref[...] * su_ref[...])
    od = y * jax.nn.sigmoid(jnp.dot(y.astype(BF16), wglu_ref[...], preferred_element_type=F32))
    acc = None
    for gi, part in enumerate((oa_ref[...], ob_ref[...], oc, od)):
        nrm = _rms(part, on_ref[gi:gi + 1, :]).astype(BF16)
        t = jnp.dot(nrm, wout_ref[gi * GROUP_WIDTH:(gi + 1) * GROUP_WIDTH, :], preferred_element_type=F32)
        acc = t if acc is None else acc + t
    o_ref[...] = x_ref[...] + mod_ref[5:6, :] * acc


def _outproj(x, mod, oa, ob, hf, hb, proj, y2, lp, *, seq, tm_pref=256):
    n, d = x.shape
    groups = mod.shape[0]
    tm = _pick(seq if groups > 1 else n, tm_pref)
    per = seq // tm if groups > 1 else 1
    mod_map = (lambda i: (i // per, 0, 0)) if groups > 1 else (lambda i: (0, 0, 0))
    gw = GROUP_WIDTH
    rowblk = pl.BlockSpec((tm, gw), lambda i: (i, 0))
    full = lambda a: pl.BlockSpec(a.shape, lambda i: (0,) * a.ndim)
    ws = [lp["s5_d"], lp["s5_w_glu"], lp["out_norm"], lp["w_out"]]
    return pl.pallas_call(
        _outproj_kernel,
        out_shape=jax.ShapeDtypeStruct((n, d), F32),
        grid=(n // tm,),
        in_specs=[pl.BlockSpec((tm, d), lambda i: (i, 0)),
                  pl.BlockSpec((None, N_MOD, d), mod_map),
                  rowblk, rowblk, rowblk, rowblk,
                  pl.BlockSpec((tm, gw), lambda i: (i, _COL["mo"] // gw)),
                  pl.BlockSpec((None, tm, gw), lambda i: (0, i, 0)),
                  pl.BlockSpec((None, tm, gw), lambda i: (1, i, 0)),
                  pl.BlockSpec((tm, gw), lambda i: (i, _COL["su"] // gw))] + [full(w) for w in ws],
        out_specs=pl.BlockSpec((tm, d), lambda i: (i, 0)),
        compiler_params=_cparams(("parallel",)),
        name="merge_out_proj",
    )(x, mod, oa, ob, hf, hb, proj, y2, y2, proj, *ws)


def _final_norm_kernel(x_ref, g_ref, o_ref):
    o_ref[...] = _rms(x_ref[...], g_ref[...])


def _final_norm(x, g, tm_pref=1024):
    n, d = x.shape
    tm = _pick(n, tm_pref)
    return pl.pallas_call(
        _final_norm_kernel,
        out_shape=jax.ShapeDtypeStruct((n, d), F32),
        grid=(n // tm,),
        in_specs=[pl.BlockSpec((tm, d), lambda i: (i, 0)), pl.BlockSpec((1, d), lambda i: (0, 0))],
        out_specs=pl.BlockSpec((tm, d), lambda i: (i, 0)),
        compiler_params=_cparams(("parallel",)),
        name="final_norm",
    )(x, g)


def _to_time_major(a, batch, seq):
    return a.reshape(batch, seq, -1).transpose(1, 0, 2).reshape(seq * batch, -1)


def _trunk_layer(x, mod, lp, *, batch, seq, tables, cache):
    latent = cache is not None
    x = _ffn(x, mod, lp["norm_g"][0:1], lp["ffn_w13"][0], lp["ffn_w2"][0], seq=seq, row0=0)
    proj = _inproj(x, mod, lp["norm_g"][1:2], lp["w_in"], seq=seq)

    qa, ka, va, ckvn, qb, kb, vb, kbn = _attn_prep(proj, tables, lp, seq=seq)
    r3 = lambda a: a.reshape(batch, seq, a.shape[-1])
    segs_a = [(r3(ka), r3(va))]
    segs_b = [(r3(kb), r3(vb))]
    if latent:
        segs_a.insert(0, cache["mla_kv"])
        segs_b.insert(0, cache["gqa_kv"])
    oa = _attention(r3(qa), segs_a, heads=MLA_HEADS, kv_heads=MLA_HEADS, dk=2 * LANE, dv=MLA_V)
    ob = _attention(r3(qb), segs_b, heads=GQA_HEADS, kv_heads=GQA_KV_HEADS, dk=GQA_HEAD_DIM, dv=GQA_HEAD_DIM)
    oa = oa.reshape(batch * seq, -1)
    ob = ob.reshape(batch * seq, -1)

    m_init = cache["mlstm"] if latent else None
    mres = [_mlstm_dir(proj, lp["mlstm_bias"], m_init, batch=batch, seq=seq, direction=dr, emit_state=not latent)
            for dr in (0, 1)]

    su_tm = _to_time_major(proj[:, _COL["su"]:_COL["su"] + S5_CH], batch, seq)
    y_tm, xfin = _s5(su_tm, lp["s5_abar"], lp["s5_bd"], lp["s5_cd"], cache["s5"] if latent else None,
                     batch=batch, seq=seq)
    y2 = y_tm.reshape(2, seq, batch, S5_CH).transpose(0, 2, 1, 3).reshape(2, batch * seq, S5_CH)

    x = _outproj(x, mod, oa, ob, mres[0][0], mres[1][0], proj, y2, lp, seq=seq)
    x = _ffn(x, mod, lp["norm_g"][2:3], lp["ffn_w13"][1], lp["ffn_w2"][1], seq=seq, row0=6)

    new_ctx = None
    if not latent:
        nh, hd = MLSTM_HEADS, MLSTM_HEAD_DIM
        kr = proj[:, _COL["kr"]:_COL["kr"] + MLA_ROPE]
        gv = proj[:, _COL["gv"]:_COL["gv"] + GQA_KV_HEADS * GQA_HEAD_DIM]
        xs = xfin.reshape(2, batch, 2, S5_GROUPS, S5_STATE).transpose(1, 0, 2, 3, 4)
        new_ctx = (ckvn.reshape(batch, seq, MLA_KV_LORA),
                   kr.reshape(batch, seq, MLA_ROPE),
                   kbn.reshape(batch, seq, GQA_KV_HEADS, GQA_HEAD_DIM),
                   gv.reshape(batch, seq, GQA_KV_HEADS, GQA_HEAD_DIM),
                   jnp.stack([mres[0][1], mres[1][1]], axis=1),
                   jnp.stack([mres[0][2], mres[1][2]], axis=1),
                   jnp.stack([mres[0][3][..., 0], mres[1][3][..., 0]], axis=1),
                   xs[:, :, 0], xs[:, :, 1])
    return x, new_ctx


def _permute_w_in(w_in):
    perm = jnp.concatenate([jnp.arange(_ORIG[n][0], _ORIG[n][0] + _ORIG[n][1]) for n in _ORDER])
    w = jnp.take(w_in, perm, axis=-1)
    return jnp.pad(w, ((0, 0), (0, 0), (0, PROJ_COLS - w.shape[-1]))).astype(BF16)


def _permute_w_uq(w_uq):
    depth, k, _ = w_uq.shape
    w = w_uq.reshape(depth, k, MLA_HEADS, MLA_NOPE + MLA_ROPE)
    w = jnp.pad(w, ((0, 0), (0, 0), (0, 0), (0, 2 * LANE - MLA_NOPE - MLA_ROPE)))
    return w.reshape(depth, k, MLA_HEADS * 2 * LANE).astype(BF16)


def kernel(x_prompt, x_sample, cache_mla_ckv, cache_mla_krope, cache_gqa_k, cache_gqa_v, state_mlstm_c, state_mlstm_n, state_mlstm_m, state_s5_re, state_s5_im, c, c_ctx, ada_w, ada_b, norm_g, ffn_w13, ffn_w2, w_in, mla_q_norm, mla_kv_norm, mla_w_uq, mla_w_ukv, gqa_q_norm, gqa_k_norm, mlstm_gate_b, s5_a_re, s5_a_im, s5_log_dt, s5_b_re, s5_b_im, s5_c_re, s5_c_im, s5_d, s5_w_glu, out_norm, w_out, final_norm):
    bc, sc, d = x_prompt.shape
    bl, sl, _ = x_sample.shape
    depth = ada_w.shape[0]
    past = cache_mla_ckv.shape[2]

    rows = ((1 + bl + SUBLANE - 1) // SUBLANE) * SUBLANE
    cvecs = jnp.concatenate([c_ctx[None, :], c, jnp.zeros((rows - 1 - bl, d), F32)], axis=0)
    mod_all = _modulation(cvecs, ada_w, ada_b).reshape(depth, rows, N_MOD, d)

    tables = _rope_tables(sl)

    w13_b = ffn_w13.astype(BF16)
    w2_b = ffn_w2.astype(BF16)
    w_in_b = _permute_w_in(w_in)
    w_uq_b = _permute_w_uq(mla_w_uq)
    w_ukv_b = mla_w_ukv.astype(BF16)
    w_out_b = w_out.astype(BF16)
    w_glu_b = s5_w_glu.astype(BF16)
    nmg = MLSTM_HEADS * 4
    bias128 = jnp.pad(mlstm_gate_b.reshape(depth, 1, nmg), ((0, 0), (0, 0), (MG_LANE, LANE - MG_LANE - nmg)))

    x_ctx = x_prompt.reshape(bc * sc, d)
    x_lat = x_sample.reshape(bl * sl, d)
    per_layer = []
    for l in range(depth):
        abar, bd, cd = _s5_params(s5_a_re[l], s5_a_im[l], s5_log_dt[l], s5_b_re[l], s5_b_im[l],
                                  s5_c_re[l], s5_c_im[l])
        lp = {"norm_g": norm_g[l], "ffn_w13": w13_b[l], "ffn_w2": w2_b[l], "w_in": w_in_b[l],
              "mla_q_norm": mla_q_norm[l][None, :], "mla_kv_norm": mla_kv_norm[l][None, :],
              "mla_w_uq": w_uq_b[l], "mla_w_ukv": w_ukv_b[l],
              "gqa_q_norm": gqa_q_norm[l][None, :], "gqa_k_norm": gqa_k_norm[l][None, :],
              "mlstm_bias": bias128[l], "s5_abar": abar, "s5_bd": bd, "s5_cd": cd,
              "s5_d": s5_d[l][None, :], "s5_w_glu": w_glu_b[l],
              "out_norm": out_norm[l].reshape(4, GROUP_WIDTH), "w_out": w_out_b[l]}

        x_ctx, ctx_l = _trunk_layer(x_ctx, mod_all[l, 0:1], lp, batch=bc, seq=sc, tables=None, cache=None)
        per_layer.append(ctx_l)

        kr_pad = jnp.pad(cache_mla_krope[:, l].reshape(bl * past, MLA_ROPE), ((0, 0), (0, LANE - MLA_ROPE)))
        kc, vc = _cache_kv(cache_mla_ckv[:, l].reshape(bl * past, MLA_KV_LORA), kr_pad, w_ukv_b[l])
        gkv = GQA_KV_HEADS * GQA_HEAD_DIM
        x0 = jnp.concatenate([state_s5_re[:, l].reshape(bl, 2, S5_NS), state_s5_im[:, l].reshape(bl, 2, S5_NS)],
                             axis=-1).transpose(1, 0, 2)
        cache = {"mla_kv": (kc.reshape(bl, past, -1), vc.reshape(bl, past, -1)),
                 "gqa_kv": (cache_gqa_k[:, l].reshape(bl, past, gkv), cache_gqa_v[:, l].reshape(bl, past, gkv)),
                 "mlstm": (state_mlstm_c[:, l], state_mlstm_n[:, l],
                           jnp.broadcast_to(state_mlstm_m[:, l][..., None], state_mlstm_n[:, l].shape)),
                 "s5": x0}
        x_lat, _ = _trunk_layer(x_lat, mod_all[l, 1:1 + bl], lp, batch=bl, seq=sl, tables=tables, cache=cache)

    new_ctx = [jnp.stack([t[i] for t in per_layer], axis=1) for i in range(9)]
    y_prompt = _final_norm(x_ctx, final_norm[None, :]).reshape(bc, sc, d)
    y_sample = _final_norm(x_lat, final_norm[None, :]).reshape(bl, sl, d)
    return (y_prompt, y_sample, *new_ctx)
```

```python
import functools
import math

import jax
import jax.numpy as jnp
from jax import lax
from jax.experimental import pallas as pl
from jax.experimental.pallas import tpu as pltpu

F32 = jnp.float32
BF16 = jnp.bfloat16

EPS = 1e-6
ROPE_THETA = 10000.0
GRID_W = 64
N_MOD = 9
LOG2E = math.log2(math.e)

D_FF = 5632
MLA_HEADS, MLA_NOPE, MLA_ROPE, MLA_V = 4, 128, 64, 128
MLA_Q_LORA, MLA_KV_LORA = 384, 256
GQA_HEADS, GQA_KV_HEADS, GQA_HEAD_DIM = 4, 2, 128
MLSTM_HEADS, MLSTM_HEAD_DIM = 4, 128
S5_GROUPS, S5_GROUP, S5_STATE = 32, 16, 64
S5_CH = S5_GROUPS * S5_GROUP
S5_NS = S5_GROUPS * S5_STATE
GROUP_WIDTH = 512

LANE = 128
SUBLANE = 8
MXU_TILE = 256
NORM_ROWS = 128
VMEM_BYTES = 64 * 1024 * 1024
VMEM_LIMIT_BYTES = VMEM_BYTES - 8 * 1024 * 1024
FFN_VMEM_LIMIT_BYTES = VMEM_BYTES - 4 * 1024 * 1024

_ORIG = dict(cq=(0, 384), ckv=(384, 256), kr=(640, 64), gq=(704, 512), gk=(1216, 256), gv=(1472, 256),
             mq=(1728, 512), mk=(2240, 512), mv=(2752, 512), mo=(3264, 512), mg=(3776, 16), su=(3792, 512))
_ORDER = ("mq", "mk", "mv", "mo", "gq", "su", "gk", "gv", "ckv", "cq", "kr", "mg")
_COL = {}
_off = 0
for _n in _ORDER:
    _COL[_n] = _off
    _off += _ORIG[_n][1]
PROJ_COLS = ((_off + LANE - 1) // LANE) * LANE
KRMG_BLOCK = _COL["kr"] // LANE
MG_LANE = _COL["mg"] - _COL["kr"]


def _cparams(sem, vmem_limit=VMEM_LIMIT_BYTES):
    return pltpu.CompilerParams(dimension_semantics=sem, vmem_limit_bytes=vmem_limit)


def _pick(n, pref):
    for t in range(min(n, pref), 0, -1):
        if n % t == 0 and (t % SUBLANE == 0 or t == n):
            return t
    return n


def _layer_spec(arr, layer, nargs):
    zeros = (0,) * (arr.ndim - 1)
    imap = {1: lambda i: (layer,) + zeros, 2: lambda i, j: (layer,) + zeros,
            3: lambda i, j, k: (layer,) + zeros}[nargs]
    return pl.BlockSpec((None,) + arr.shape[1:], imap, pipeline_mode=pl.Buffered(1))


def _rms(x, g):
    return x * lax.rsqrt(jnp.mean(x * x, axis=-1, keepdims=True) + EPS) * g


def _modulate(x, g, scale, shift):
    return x * lax.rsqrt(jnp.mean(x * x, axis=-1, keepdims=True) + EPS) * (g * (1.0 + scale)) + shift


def _swap_halves(x, half):
    w = x.shape[-1]
    lane = lax.broadcasted_iota(jnp.int32, x.shape, x.ndim - 1)
    first = (lane & (2 * half - 1)) < half
    return jnp.where(first, pltpu.roll(x, w - half, axis=x.ndim - 1), pltpu.roll(x, half, axis=x.ndim - 1))


def _silu(x):
    return x * jax.nn.sigmoid(x)


def _mod_kernel(c_ref, w_ref, b_ref, o_ref):
    a = _silu(c_ref[...]).astype(BF16)
    o_ref[...] = jnp.dot(a, w_ref[...].astype(BF16), preferred_element_type=F32) + b_ref[...]


def _modulation(cvecs, ada_w, ada_b):
    depth, d, nm = ada_w.shape
    r = cvecs.shape[0]
    tn = _pick(nm, 1024)
    return pl.pallas_call(
        _mod_kernel,
        out_shape=jax.ShapeDtypeStruct((depth, r, nm), F32),
        grid=(depth, nm // tn),
        in_specs=[pl.BlockSpec((r, d), lambda l, j: (0, 0)),
                  pl.BlockSpec((None, d, tn), lambda l, j: (l, 0, j)),
                  pl.BlockSpec((None, 1, tn), lambda l, j: (l, 0, j))],
        out_specs=pl.BlockSpec((None, r, tn), lambda l, j: (l, 0, j)),
        compiler_params=_cparams(("parallel", "parallel")),
        name="adaln_mod",
    )(cvecs, ada_w, ada_b.reshape(depth, 1, nm))


def _rope_kernel(ca_ref, sa_ref, cb_ref, sb_ref, *, log2w):
    s = ca_ref.shape[0]
    t = lax.broadcasted_iota(jnp.int32, (s, 1), 0)
    row = lax.shift_right_logical(t, log2w).astype(F32)
    col = (t & ((1 << log2w) - 1)).astype(F32)

    def tables(width, off, rd):
        lane = lax.broadcasted_iota(jnp.int32, (1, width), 1)
        r = lane - off
        inr = (r >= 0) & (r < rd)
        half, quarter = rd // 2, rd // 4
        is_col = r >= half
        rr = jnp.where(is_col, r - half, r)
        second = rr >= quarter
        j = jnp.where(second, rr - quarter, rr).astype(F32)
        inv = jnp.exp(j * (-2.0 / half * math.log(ROPE_THETA)))
        ang = jnp.where(is_col, col, row) * inv
        sign = jnp.where(second, 1.0, -1.0)
        return jnp.where(inr, jnp.cos(ang), 1.0), jnp.where(inr, sign * jnp.sin(ang), 0.0)

    ca, sa = tables(2 * LANE, MLA_NOPE, MLA_ROPE)
    cb, sb = tables(GQA_HEAD_DIM, 0, GQA_HEAD_DIM)
    ca_ref[...] = ca
    sa_ref[...] = sa
    cb_ref[...] = cb
    sb_ref[...] = sb


def _rope_tables(s):
    log2w = GRID_W.bit_length() - 1
    assert 1 << log2w == GRID_W
    shp = lambda w: jax.ShapeDtypeStruct((s, w), F32)
    return pl.pallas_call(
        functools.partial(_rope_kernel, log2w=log2w),
        out_shape=(shp(2 * LANE), shp(2 * LANE), shp(GQA_HEAD_DIM), shp(GQA_HEAD_DIM)),
        name="rope_tables",
    )()


def _s5_params(a_re, a_im, log_dt, b_re, b_im, c_re, c_im):
    depth, nd = a_re.shape[:2]
    r = depth * nd
    are = a_re.reshape(r, S5_NS)
    aim = a_im.reshape(r, S5_NS)
    ldt = jnp.broadcast_to(log_dt[..., None], (depth, nd, S5_GROUPS, S5_STATE)).reshape(r, S5_NS)
    rep = lambda b: jnp.repeat(jnp.transpose(b, (0, 3, 1, 2)).reshape(depth, 1, S5_GROUP, S5_NS), nd, axis=1)
    bre = rep(b_re).reshape(r, S5_GROUP, S5_NS)
    bim = rep(b_im).reshape(r, S5_GROUP, S5_NS)
    v = jax.ShapeDtypeStruct((r, S5_NS), F32)
    m = jax.ShapeDtypeStruct((r, S5_GROUP, S5_NS), F32)
    ar, ai, bbr, bbi = pl.pallas_call(_s5_param_rows_kernel, out_shape=(v, v, m, m), name="s5_discretise")(
        are, aim, ldt, bre, bim)
    eye = jnp.eye(S5_GROUPS, dtype=F32)

    def dense_b(bb):
        bb = bb.reshape(r, S5_GROUP, S5_GROUPS, S5_STATE)
        return jnp.einsum("dcgp,gh->dgchp", bb, eye).reshape(r, S5_CH, S5_NS)

    def dense_c(cc):
        return jnp.einsum("lgcp,gh->lhpgc", cc, eye).reshape(depth, S5_NS, S5_CH)

    bd = jnp.concatenate([dense_b(bbr), dense_b(bbi)], axis=-1).astype(BF16).reshape(depth, nd, S5_CH, 2 * S5_NS)
    cd = jnp.concatenate([dense_c(c_re), -dense_c(c_im)], axis=1).astype(BF16)
    abar = jnp.concatenate([ar, ai], axis=-1).reshape(depth, nd, 1, 2 * S5_NS)
    return abar, bd, cd


def _s5_param_rows_kernel(are_ref, aim_ref, ldt_ref, bre_ref, bim_ref, oar_ref, oai_ref, obr_ref, obi_ref):
    lr, li = are_ref[...], aim_ref[...]
    dt = jnp.exp(ldt_ref[...])
    mag = jnp.exp(lr * dt)
    ar, ai = mag * jnp.cos(li * dt), mag * jnp.sin(li * dt)
    oar_ref[...] = ar
    oai_ref[...] = ai
    nr, ni = ar - 1.0, ai
    den = lr * lr + li * li
    fr, fi = (nr * lr + ni * li) / den, (ni * lr - nr * li) / den
    for d in range(fr.shape[0]):
        br, bi = bre_ref[d], bim_ref[d]
        obr_ref[d] = fr[d:d + 1] * br - fi[d:d + 1] * bi
        obi_ref[d] = fr[d:d + 1] * bi + fi[d:d + 1] * br


def _ffn_kernel(x_ref, mod_ref, g_ref, w1_ref, w3_ref, w2_ref, o_ref, h_ref, *, row0, slab):
    j = pl.program_id(1)

    @pl.when(j == 0)
    def _():
        nr = math.gcd(h_ref.shape[0], NORM_ROWS)
        for r in range(h_ref.shape[0] // nr):
            rows = slice(r * nr, (r + 1) * nr)
            h = _modulate(x_ref[rows, :], g_ref[...], mod_ref[row0 + 1:row0 + 2, :], mod_ref[row0:row0 + 1, :])
            h_ref[rows, :] = h.astype(BF16)
        o_ref[...] = jnp.zeros_like(o_ref)

    for r in range(h_ref.shape[0] // slab):
        rows = slice(r * slab, (r + 1) * slab)
        h = h_ref[rows, :]
        gate = jnp.dot(h, w1_ref[...], preferred_element_type=F32)
        up = jnp.dot(h, w3_ref[...], preferred_element_type=F32)
        act = (_silu(gate) * up).astype(BF16)
        o_ref[rows, :] += jnp.dot(act, w2_ref[...], preferred_element_type=F32)

    @pl.when(j == pl.num_programs(1) - 1)
    def _():
        o_ref[...] = x_ref[...] + (0.5 * mod_ref[row0 + 2:row0 + 3, :]) * o_ref[...]


def _ffn(x, mod, pr, *, layer, which, seq, tm_pref=1024, tf_pref=512):
    n, d = x.shape
    groups = mod.shape[0]
    tm = _pick(seq if groups > 1 else n, tm_pref)
    tf = _pick(D_FF, tf_pref)
    nf = D_FF // tf
    per = seq // tm if groups > 1 else 1
    mod_map = (lambda i, j: (i // per, 0, 0)) if groups > 1 else (lambda i, j: (0, 0, 0))
    return pl.pallas_call(
        functools.partial(_ffn_kernel, row0=6 * which, slab=_pick(tm, 512)),
        out_shape=jax.ShapeDtypeStruct((n, d), F32),
        grid=(n // tm, nf),
        in_specs=[pl.BlockSpec((tm, d), lambda i, j: (i, 0)),
                  pl.BlockSpec((None, N_MOD, d), mod_map),
                  pl.BlockSpec((None, None, 1, d), lambda i, j: (layer, 2 * which, 0, 0)),
                  pl.BlockSpec((None, None, d, tf), lambda i, j: (layer, which, 0, j)),
                  pl.BlockSpec((None, None, d, tf), lambda i, j: (layer, which, 0, nf + j)),
                  pl.BlockSpec((None, None, tf, d), lambda i, j: (layer, which, j, 0))],
        out_specs=pl.BlockSpec((tm, d), lambda i, j: (i, 0)),
        scratch_shapes=[pltpu.VMEM((tm, d), BF16)],
        compiler_params=_cparams(("parallel", "arbitrary"), FFN_VMEM_LIMIT_BYTES),
        name="ffn",
    )(x, mod, pr["norm_g"], pr["ffn_w13"], pr["ffn_w13"], pr["ffn_w2"])


def _inproj_kernel(x_ref, mod_ref, g_ref, w_ref, o_ref, su_ref):
    h = _modulate(x_ref[...], g_ref[...], mod_ref[4:5, :], mod_ref[3:4, :]).astype(BF16)
    o_ref[...] = jnp.dot(h, w_ref[...], preferred_element_type=F32)
    su_ref[...] = o_ref[:, _COL["su"]:_COL["su"] + S5_CH]


def _inproj(x, mod, pr, *, layer, batch, seq, tm_pref=512):
    n, d = x.shape
    groups = mod.shape[0]
    tm = _pick(seq, tm_pref)
    per = seq // tm
    mod_map = (lambda i: (i // per, 0, 0)) if groups > 1 else (lambda i: (0, 0, 0))
    return pl.pallas_call(
        _inproj_kernel,
        out_shape=(jax.ShapeDtypeStruct((n, PROJ_COLS), F32), jax.ShapeDtypeStruct((seq, batch * S5_CH), F32)),
        grid=(n // tm,),
        in_specs=[pl.BlockSpec((tm, d), lambda i: (i, 0)),
                  pl.BlockSpec((None, N_MOD, d), mod_map),
                  pl.BlockSpec((None, None, 1, d), lambda i: (layer, 1, 0, 0)),
                  _layer_spec(pr["w_in"], layer, 1)],
        out_specs=(pl.BlockSpec((tm, PROJ_COLS), lambda i: (i, 0)),
                   pl.BlockSpec((tm, S5_CH), lambda i: (i % per, i // per))),
        compiler_params=_cparams(("parallel",)),
        name="in_proj",
    )(x, mod, pr["norm_g"], pr["w_in"])


def _mla_kv(ckv_n, kr128, wukv_ref, ka_ref, va_ref):
    kv = jnp.dot(ckv_n.astype(BF16), wukv_ref[...], preferred_element_type=F32)
    krb = kr128.astype(BF16)
    hw = MLA_NOPE + MLA_V
    for h in range(MLA_HEADS):
        ka_ref[:, h * 2 * LANE:h * 2 * LANE + MLA_NOPE] = kv[:, h * hw:h * hw + MLA_NOPE].astype(BF16)
        ka_ref[:, h * 2 * LANE + MLA_NOPE:(h + 1) * 2 * LANE] = krb
        va_ref[:, h * MLA_V:(h + 1) * MLA_V] = kv[:, h * hw + MLA_NOPE:(h + 1) * hw].astype(BF16)


def _attn_prep_kernel(*refs, rope):
    if rope:
        (cq_ref, ckv_ref, krmg_ref, gq_ref, gk_ref, gv_ref, ca_ref, sa_ref, cb_ref, sb_ref,
         qng_ref, kvng_ref, wuq_ref, wukv_ref, gqg_ref, gkg_ref,
         qa_ref, ka_ref, va_ref, ckvn_ref, qb_ref, kb_ref, vb_ref, kbn_ref) = refs
    else:
        (cq_ref, ckv_ref, krmg_ref, gq_ref, gk_ref, gv_ref,
         qng_ref, kvng_ref, wuq_ref, wukv_ref, gqg_ref, gkg_ref,
         qa_ref, ka_ref, va_ref, ckvn_ref, qb_ref, kb_ref, vb_ref, kbn_ref) = refs

    cqn = _rms(cq_ref[...], qng_ref[...]).astype(BF16)
    qa = jnp.dot(cqn, wuq_ref[...], preferred_element_type=F32)
    if rope:
        ca = jnp.concatenate([ca_ref[...]] * MLA_HEADS, axis=1)
        sa = jnp.concatenate([sa_ref[...]] * MLA_HEADS, axis=1)
        qa = qa * ca + _swap_halves(qa, MLA_ROPE // 4) * sa
    qa_ref[...] = (qa * ((MLA_NOPE + MLA_ROPE) ** -0.5 * LOG2E)).astype(BF16)

    ckv_n = _rms(ckv_ref[...], kvng_ref[...])
    ckvn_ref[...] = ckv_n
    krmg = krmg_ref[...]
    lane = lax.broadcasted_iota(jnp.int32, krmg.shape, 1)
    if rope:
        krmg = krmg * ca_ref[:, LANE:] + _swap_halves(krmg, MLA_ROPE // 4) * sa_ref[:, LANE:]
    kr128 = jnp.where(lane < MLA_ROPE, krmg, 0.0)
    _mla_kv(ckv_n, kr128, wukv_ref, ka_ref, va_ref)

    hd = GQA_HEAD_DIM
    gq, gk = gq_ref[...], gk_ref[...]
    for h in range(GQA_HEADS):
        q = _rms(gq[:, h * hd:(h + 1) * hd], gqg_ref[...])
        if rope:
            q = q * cb_ref[...] + _swap_halves(q, hd // 4) * sb_ref[...]
        qb_ref[:, h * hd:(h + 1) * hd] = (q * (hd ** -0.5 * LOG2E)).astype(BF16)
    for h in range(GQA_KV_HEADS):
        k = _rms(gk[:, h * hd:(h + 1) * hd], gkg_ref[...])
        kbn_ref[:, h * hd:(h + 1) * hd] = k
        if rope:
            k = k * cb_ref[...] + _swap_halves(k, hd // 4) * sb_ref[...]
        kb_ref[:, h * hd:(h + 1) * hd] = k.astype(BF16)
    vb_ref[...] = gv_ref[...].astype(BF16)


def _attn_prep(proj, tables, pr, *, layer, seq, tm_pref=512):
    n = proj.shape[0]
    tm = _pick(seq, tm_pref)
    per = seq // tm
    rope = tables is not None

    def pblock(name, width):
        idx = _COL[name] // width
        assert idx * width == _COL[name]
        return pl.BlockSpec((tm, width), lambda i: (i, idx))

    in_specs = [pblock("cq", MLA_Q_LORA), pblock("ckv", MLA_KV_LORA),
                pl.BlockSpec((tm, LANE), lambda i: (i, KRMG_BLOCK)),
                pblock("gq", 512), pblock("gk", 256), pblock("gv", 256)]
    args = [proj] * 6
    if rope:
        in_specs += [pl.BlockSpec((tm, t.shape[1]), lambda i: (i % per, 0)) for t in tables]
        args += list(tables)
    ws = [pr["mla_q_norm"], pr["mla_kv_norm"], pr["mla_w_uq"], pr["mla_w_ukv"], pr["gqa_q_norm"], pr["gqa_k_norm"]]
    in_specs += [_layer_spec(w, layer, 1) for w in ws]
    args += ws
    widths = [(4 * 2 * LANE, BF16), (4 * 2 * LANE, BF16), (4 * MLA_V, BF16), (MLA_KV_LORA, F32),
              (GQA_HEADS * GQA_HEAD_DIM, BF16), (GQA_KV_HEADS * GQA_HEAD_DIM, BF16),
              (GQA_KV_HEADS * GQA_HEAD_DIM, BF16), (GQA_KV_HEADS * GQA_HEAD_DIM, F32)]
    return pl.pallas_call(
        functools.partial(_attn_prep_kernel, rope=rope),
        out_shape=tuple(jax.ShapeDtypeStruct((n, w), dt) for w, dt in widths),
        grid=(n // tm,),
        in_specs=in_specs,
        out_specs=tuple(pl.BlockSpec((tm, w), lambda i: (i, 0)) for w, _ in widths),
        compiler_params=_cparams(("parallel",)),
        name="attn_prep",
    )(*args)


def _cache_kv_kernel(ckv_ref, kr_ref, wukv_ref, ka_ref, va_ref):
    _mla_kv(ckv_ref[...], kr_ref[...], wukv_ref, ka_ref, va_ref)


def _cache_kv(ckv, kr128, wukv, *, layer):
    b, _, p, _ = ckv.shape
    return pl.pallas_call(
        _cache_kv_kernel,
        out_shape=(jax.ShapeDtypeStruct((b, p, 4 * 2 * LANE), BF16), jax.ShapeDtypeStruct((b, p, 4 * MLA_V), BF16)),
        grid=(b,),
        in_specs=[pl.BlockSpec((None, None, p, MLA_KV_LORA), lambda i: (i, layer, 0, 0)),
                  pl.BlockSpec((None, None, p, LANE), lambda i: (i, layer, 0, 0)),
                  _layer_spec(wukv, layer, 1)],
        out_specs=(pl.BlockSpec((None, p, 4 * 2 * LANE), lambda i: (i, 0, 0)),
                   pl.BlockSpec((None, p, 4 * MLA_V), lambda i: (i, 0, 0))),
        compiler_params=_cparams(("parallel",)),
        name="mla_cache_kv",
    )(ckv, kr128, wukv)


def _attn_kernel(q_ref, *refs, nseg, hp, g, dk, dv):
    o_ref = refs[-1]
    for j in range(hp):
        kv = j // g
        q = q_ref[:, j * dk:(j + 1) * dk]
        ks = [refs[2 * i][:, kv * dk:(kv + 1) * dk].astype(BF16) for i in range(nseg)]
        vs = [refs[2 * i + 1][:, kv * dv:(kv + 1) * dv].astype(BF16) for i in range(nseg)]
        ss = [lax.dot_general(q, k, (((1,), (1,)), ((), ())), preferred_element_type=F32) for k in ks]
        m = functools.reduce(jnp.maximum, [jnp.max(s, axis=-1, keepdims=True) for s in ss])
        ps = [jnp.exp2(s - m) for s in ss]
        l = functools.reduce(jnp.add, [jnp.sum(p, axis=-1, keepdims=True) for p in ps])
        o = functools.reduce(jnp.add, [jnp.dot(p.astype(BF16), v, preferred_element_type=F32)
                                       for p, v in zip(ps, vs)])
        o_ref[:, j * dv:(j + 1) * dv] = o / l


def _attention(q, segs, *, heads, kv_heads, dk, dv, tq_pref=512, hp=2):
    b, sq, _ = q.shape
    tq = _pick(sq, tq_pref)
    g = heads // kv_heads
    assert heads % hp == 0 and (hp % g == 0 or g % hp == 0)
    kvp = max(1, hp // g)
    kcol = (lambda hb: hb) if hp >= g else (lambda hb: hb * hp // g)
    in_specs = [pl.BlockSpec((None, tq, hp * dk), lambda bi, hb, i: (bi, i, hb))]
    args = [q]
    for k, v, layer in segs:
        sk = k.shape[-2]
        if layer is None:
            in_specs += [pl.BlockSpec((None, sk, kvp * dk), lambda bi, hb, i: (bi, 0, kcol(hb))),
                         pl.BlockSpec((None, sk, kvp * dv), lambda bi, hb, i: (bi, 0, kcol(hb)))]
        else:
            in_specs += [pl.BlockSpec((None, None, sk, kvp * dk),
                                      lambda bi, hb, i, layer=layer: (bi, layer, 0, kcol(hb))),
                         pl.BlockSpec((None, None, sk, kvp * dv),
                                      lambda bi, hb, i, layer=layer: (bi, layer, 0, kcol(hb)))]
        args += [k, v]
    return pl.pallas_call(
        functools.partial(_attn_kernel, nseg=len(segs), hp=hp, g=min(g, hp), dk=dk, dv=dv),
        out_shape=jax.ShapeDtypeStruct((b, sq, heads * dv), F32),
        grid=(b, heads // hp, sq // tq),
        in_specs=in_specs,
        out_specs=pl.BlockSpec((None, tq, hp * dv), lambda bi, hb, i: (bi, i, hb)),
        compiler_params=_cparams(("parallel", "parallel", "parallel")),
        name="attention",
    )(*args)


def _log_sigmoid(x):
    return jnp.minimum(x, 0.0) - jnp.log1p(jnp.exp(-jnp.abs(x)))


def _mlstm_kernel(*refs, rev, direction, has_init, emit_state):
    q_ref, k_ref, v_ref, gate_ref, bias_ref = refs[:5]
    pos = 5
    if has_init:
        c0_ref, n0_ref, m0_ref = refs[pos:pos + 3]
        pos += 3
    h_ref = refs[pos]
    pos += 1
    if emit_state:
        co_ref, no_ref, mo_ref = refs[pos:pos + 3]
        pos += 3
    c_s, n_s, m_s = refs[pos:pos + 3]

    c = pl.program_id(1)
    nh, hd = MLSTM_HEADS, MLSTM_HEAD_DIM
    cl = q_ref.shape[0]

    @pl.when(c == 0)
    def _():
        if has_init:
            c_s[...] = c0_ref[...]
            n_s[...] = n0_ref[...]
            m_s[...] = m0_ref[...]
        else:
            c_s[...] = jnp.zeros_like(c_s)
            n_s[...] = jnp.zeros_like(n_s)
            m_s[...] = jnp.zeros_like(m_s)

    g = gate_ref[...] + bias_ref[...]
    gt = g.T
    lf = _log_sigmoid(g)
    ri = lax.broadcasted_iota(jnp.int32, (cl, cl), 0)
    ci = lax.broadcasted_iota(jnp.int32, (cl, cl), 1)
    mask = (ri <= ci) if rev else (ri >= ci)
    cum = jnp.dot(mask.astype(F32), lf, preferred_element_type=F32, precision=lax.Precision.HIGHEST)
    cumt = cum.T
    last = 0 if rev else cl - 1
    li0 = MG_LANE + direction * 2 * nh
    lf0 = li0 + nh

    for h in range(nh):
        cum_c = cum[:, lf0 + h:lf0 + h + 1]
        cum_r = cumt[lf0 + h:lf0 + h + 1, :]
        li_r = gt[li0 + h:li0 + h + 1, :]
        li_c = g[:, li0 + h:li0 + h + 1]
        m_prev = m_s[h:h + 1, 0:1]
        dmat = jnp.where(mask, cum_c - cum_r + li_r, -jnp.inf)
        inter = cum_c + m_prev
        m_t = jnp.maximum(inter, jnp.max(dmat, axis=1, keepdims=True))
        w = jnp.exp(dmat - m_t)
        qh = q_ref[:, h * hd:(h + 1) * hd].astype(BF16)
        kh = k_ref[:, h * hd:(h + 1) * hd] * (hd ** -0.5)
        vh = v_ref[:, h * hd:(h + 1) * hd].astype(BF16)
        kt = kh.T
        sc = jnp.dot(qh, kt.astype(BF16), preferred_element_type=F32) * w
        inter_w = jnp.exp(inter - m_t)
        cmat = c_s[h]
        nrow = n_s[h:h + 1, :]
        num = (jnp.dot(sc.astype(BF16), vh, preferred_element_type=F32)
               + inter_w * jnp.dot(qh, cmat.astype(BF16), preferred_element_type=F32))
        qn = jnp.sum(q_ref[:, h * hd:(h + 1) * hd] * nrow, axis=1, keepdims=True)
        den = jnp.sum(sc, axis=1, keepdims=True) + inter_w * qn
        h_ref[:, h * hd:(h + 1) * hd] = num / jnp.maximum(jnp.abs(den), jnp.exp(-m_t))

        tot = cum_c[last:last + 1, :]
        g_r = tot - cum_r + li_r
        g_c = tot - cum_c + li_c
        m_new = jnp.maximum(tot + m_prev, jnp.max(g_r, axis=1, keepdims=True))
        decay = jnp.exp(tot + m_prev - m_new)
        ws_r = jnp.exp(g_r - m_new)
        ws_c = jnp.exp(g_c - m_new)
        c_s[h] = decay * cmat + jnp.dot((kt * ws_r).astype(BF16), vh, preferred_element_type=F32)
        n_s[h:h + 1, :] = decay * nrow + jnp.sum(kh * ws_c, axis=0, keepdims=True)
        m_s[h:h + 1, :] = jnp.broadcast_to(m_new, (1, hd))

    if emit_state:
        @pl.when(c == pl.num_programs(1) - 1)
        def _():
            co_ref[...] = c_s[...]
            no_ref[...] = n_s[...]
            mo_ref[...] = m_s[...]


def _mlstm_dir(proj, bias128, init, *, layer, batch, seq, direction, emit_state, chunk_pref=256):
    n = proj.shape[0]
    cl = _pick(seq, chunk_pref)
    nc = seq // cl
    rev = direction == 1
    nh, hd = MLSTM_HEADS, MLSTM_HEAD_DIM
    row = (lambda b, c: b * nc + (nc - 1 - c)) if rev else (lambda b, c: b * nc + c)
    qkv = lambda name: pl.BlockSpec((cl, nh * hd), lambda b, c: (row(b, c), _COL[name] // (nh * hd)))
    in_specs = [qkv("mq"), qkv("mk"), qkv("mv"),
                pl.BlockSpec((cl, LANE), lambda b, c: (row(b, c), KRMG_BLOCK)),
                pl.BlockSpec((None, 1, LANE), lambda b, c: (layer, 0, 0))]
    args = [proj, proj, proj, proj, bias128]
    has_init = init is not None
    if has_init:
        c0, n0, m0 = init
        in_specs += [pl.BlockSpec((None, None, None, nh, hd, hd), lambda b, c: (b, layer, direction, 0, 0, 0)),
                     pl.BlockSpec((None, None, None, nh, hd), lambda b, c: (b, layer, direction, 0, 0)),
                     pl.BlockSpec((None, None, None, nh, hd), lambda b, c: (b, layer, direction, 0, 0))]
        args += [c0, n0, m0]
    out_shape = [jax.ShapeDtypeStruct((n, nh * hd), F32)]
    out_specs = [pl.BlockSpec((cl, nh * hd), lambda b, c: (row(b, c), 0))]
    if emit_state:
        out_shape += [jax.ShapeDtypeStruct((batch, nh, hd, hd), F32), jax.ShapeDtypeStruct((batch, nh, hd), F32),
                      jax.ShapeDtypeStruct((batch, nh, hd), F32)]
        out_specs += [pl.BlockSpec((None, nh, hd, hd), lambda b, c: (b, 0, 0, 0)),
                      pl.BlockSpec((None, nh, hd), lambda b, c: (b, 0, 0)),
                      pl.BlockSpec((None, nh, hd), lambda b, c: (b, 0, 0))]
    return pl.pallas_call(
        functools.partial(_mlstm_kernel, rev=rev, direction=direction, has_init=has_init, emit_state=emit_state),
        out_shape=tuple(out_shape),
        grid=(batch, nc),
        in_specs=in_specs,
        out_specs=tuple(out_specs),
        scratch_shapes=[pltpu.VMEM((nh, hd, hd), F32), pltpu.VMEM((nh, hd), F32), pltpu.VMEM((nh, hd), F32)],
        compiler_params=_cparams(("parallel", "arbitrary")),
        name="mlstm",
    )(*args)


_S5_STATE_TILES = 2 * S5_NS // MXU_TILE
_S5_GROUPS_PER_TILE = MXU_TILE // S5_STATE


def _s5_channel_tile(state_tile):
    first_group = (state_tile % (S5_NS // MXU_TILE)) * _S5_GROUPS_PER_TILE
    return first_group * S5_GROUP // MXU_TILE


def _s5_kernel(*refs, batch, has_init):
    if has_init:
        u_ref, bd_ref, a_ref, cd_ref, x0_ref, y_ref, xf_ref, bu_s, x_s = refs
    else:
        u_ref, bd_ref, a_ref, cd_ref, y_ref, xf_ref, bu_s, x_s = refs
    d = pl.program_id(0)
    c = pl.program_id(1)
    ns = S5_NS
    tsteps = u_ref.shape[0] // batch
    sg = min(SUBLANE, batch)
    mt = MXU_TILE

    @pl.when(c == 0)
    def _():
        if has_init:
            x_s[...] = x0_ref[...]
        else:
            x_s[...] = jnp.zeros_like(x_s)

    u = u_ref[...].astype(BF16)
    for st in range(_S5_STATE_TILES):
        ct = _s5_channel_tile(st)
        bu_s[:, st * mt:(st + 1) * mt] = jnp.dot(u[:, ct * mt:(ct + 1) * mt],
                                                 bd_ref[ct * mt:(ct + 1) * mt, st * mt:(st + 1) * mt],
                                                 preferred_element_type=F32)
    a_re = jnp.broadcast_to(a_ref[:, :ns], (sg, ns))
    a_im = jnp.broadcast_to(a_ref[:, ns:], (sg, ns))

    for grp in range(batch // sg):
        def body(t, carry):
            xr, xi = carry
            tt = t + d * (tsteps - 1 - 2 * t)
            r0 = pl.multiple_of(tt * batch + grp * sg, sg)
            br = bu_s[pl.ds(r0, sg), :ns]
            bi = bu_s[pl.ds(r0, sg), ns:]
            nr = a_re * xr - a_im * xi + br
            ni = a_re * xi + a_im * xr + bi
            bu_s[pl.ds(r0, sg), :ns] = nr
            bu_s[pl.ds(r0, sg), ns:] = ni
            return nr, ni

        x0 = (x_s[grp * sg:(grp + 1) * sg, :ns], x_s[grp * sg:(grp + 1) * sg, ns:])
        xr, xi = lax.fori_loop(0, tsteps, body, x0)
        x_s[grp * sg:(grp + 1) * sg, :ns] = xr
        x_s[grp * sg:(grp + 1) * sg, ns:] = xi

    for ct in range(S5_CH // mt):
        acc = None
        for st in range(_S5_STATE_TILES):
            if _s5_channel_tile(st) != ct:
                continue
            t = jnp.dot(bu_s[:, st * mt:(st + 1) * mt].astype(BF16),
                        cd_ref[st * mt:(st + 1) * mt, ct * mt:(ct + 1) * mt], preferred_element_type=F32)
            acc = t if acc is None else acc + t
        y_ref[:, ct * mt:(ct + 1) * mt] = acc

    @pl.when(c == pl.num_programs(1) - 1)
    def _():
        xf_ref[...] = x_s[...]


def _s5(u_tm, pr, x0, *, layer, batch, seq, rows_pref=512):
    tsteps = _pick(seq, max(1, rows_pref // batch))
    rows = tsteps * batch
    nc = seq // tsteps
    cpos = lambda d, c: c + d * (nc - 1 - 2 * c)
    has_init = x0 is not None
    in_specs = [pl.BlockSpec((rows, S5_CH), lambda d, c: (cpos(d, c), 0)),
                pl.BlockSpec((None, None, S5_CH, 2 * S5_NS), lambda d, c: (layer, d, 0, 0)),
                pl.BlockSpec((None, None, 1, 2 * S5_NS), lambda d, c: (layer, d, 0, 0)),
                _layer_spec(pr["s5_cd"], layer, 2)]
    args = [u_tm, pr["s5_bd"], pr["s5_abar"], pr["s5_cd"]]
    if has_init:
        in_specs.append(pl.BlockSpec((None, batch, 2 * S5_NS), lambda d, c: (d, 0, 0)))
        args.append(x0)
    return pl.pallas_call(
        functools.partial(_s5_kernel, batch=batch, has_init=has_init),
        out_shape=(jax.ShapeDtypeStruct((2, seq * batch, S5_CH), F32),
                   jax.ShapeDtypeStruct((2, batch, 2 * S5_NS), F32)),
        grid=(2, nc),
        in_specs=in_specs,
        out_specs=(pl.BlockSpec((None, rows, S5_CH), lambda d, c: (d, cpos(d, c), 0)),
                   pl.BlockSpec((None, batch, 2 * S5_NS), lambda d, c: (d, 0, 0))),
        scratch_shapes=[pltpu.VMEM((rows, 2 * S5_NS), F32), pltpu.VMEM((batch, 2 * S5_NS), F32)],
        compiler_params=_cparams(("parallel", "arbitrary")),
        name="s5_scan",
    )(*args)


def _gelu_tanh(x):
    return 0.5 * x * (1.0 + jnp.tanh(math.sqrt(2.0 / math.pi) * (x + 0.044715 * (x * x * x))))


def _outproj_kernel(x_ref, mod_ref, oa_ref, ob_ref, hf_ref, hb_ref, mo_ref, yf_ref, yb_ref, su_ref,
                    d_ref, wglu_ref, on_ref, wout_ref, o_ref):
    oc = jax.nn.sigmoid(mo_ref[...]) * (hf_ref[...] + hb_ref[...])
    y = _gelu_tanh(yf_ref[...] + yb_ref[...] + d_ref[...] * su_ref[...])
    od = y * jax.nn.sigmoid(jnp.dot(y.astype(BF16), wglu_ref[...], preferred_element_type=F32))
    acc = None
    for gi, part in enumerate((oa_ref[...], ob_ref[...], oc, od)):
        nrm = _rms(part, on_ref[gi:gi + 1, :]).astype(BF16)
        t = jnp.dot(nrm, wout_ref[gi * GROUP_WIDTH:(gi + 1) * GROUP_WIDTH, :], preferred_element_type=F32)
        acc = t if acc is None else acc + t
    o_ref[...] = x_ref[...] + mod_ref[5:6, :] * acc


def _outproj(x, mod, oa, ob, hf, hb, proj, y_tm, pr, *, layer, seq, tm_pref=256):
    n, d = x.shape
    groups = mod.shape[0]
    tm = _pick(seq, tm_pref)
    per = seq // tm
    mod_map = (lambda i: (i // per, 0, 0)) if groups > 1 else (lambda i: (0, 0, 0))
    gw = GROUP_WIDTH
    rowblk = pl.BlockSpec((tm, gw), lambda i: (i, 0))
    ws = [pr["s5_d"], pr["s5_w_glu"], pr["out_norm"], pr["w_out"]]
    return pl.pallas_call(
        _outproj_kernel,
        out_shape=jax.ShapeDtypeStruct((n, d), F32),
        grid=(n // tm,),
        in_specs=[pl.BlockSpec((tm, d), lambda i: (i, 0)),
                  pl.BlockSpec((None, N_MOD, d), mod_map),
                  rowblk, rowblk, rowblk, rowblk,
                  pl.BlockSpec((tm, gw), lambda i: (i, _COL["mo"] // gw)),
                  pl.BlockSpec((None, tm, gw), lambda i: (0, i % per, i // per)),
                  pl.BlockSpec((None, tm, gw), lambda i: (1, i % per, i // per)),
                  pl.BlockSpec((tm, gw), lambda i: (i, _COL["su"] // gw))] + [_layer_spec(w, layer, 1) for w in ws],
        out_specs=pl.BlockSpec((tm, d), lambda i: (i, 0)),
        compiler_params=_cparams(("parallel",)),
        name="merge_out_proj",
    )(x, mod, oa, ob, hf, hb, proj, y_tm, y_tm, proj, *ws)


def _final_norm_kernel(x_ref, g_ref, o_ref):
    o_ref[...] = _rms(x_ref[...], g_ref[...])


def _final_norm(x, g, tm_pref=1024):
    n, d = x.shape
    tm = _pick(n, tm_pref)
    return pl.pallas_call(
        _final_norm_kernel,
        out_shape=jax.ShapeDtypeStruct((n, d), F32),
        grid=(n // tm,),
        in_specs=[pl.BlockSpec((tm, d), lambda i: (i, 0)), pl.BlockSpec((1, d), lambda i: (0, 0))],
        out_specs=pl.BlockSpec((tm, d), lambda i: (i, 0)),
        compiler_params=_cparams(("parallel",)),
        name="final_norm",
    )(x, g)


def _trunk_layer(x, mod, pr, *, layer, batch, seq, tables, cache):
    latent = cache is not None
    x = _ffn(x, mod, pr, layer=layer, which=0, seq=seq)
    proj, su_tm = _inproj(x, mod, pr, layer=layer, batch=batch, seq=seq)

    qa, ka, va, ckvn, qb, kb, vb, kbn = _attn_prep(proj, tables, pr, layer=layer, seq=seq)
    r3 = lambda a: a.reshape(batch, seq, a.shape[-1])
    segs_a = [(r3(ka), r3(va), None)]
    segs_b = [(r3(kb), r3(vb), None)]
    if latent:
        segs_a.insert(0, cache["mla_kv"] + (None,))
        segs_b.insert(0, cache["gqa_kv"] + (layer,))
    oa = _attention(r3(qa), segs_a, heads=MLA_HEADS, kv_heads=MLA_HEADS, dk=2 * LANE, dv=MLA_V)
    ob = _attention(r3(qb), segs_b, heads=GQA_HEADS, kv_heads=GQA_KV_HEADS, dk=GQA_HEAD_DIM, dv=GQA_HEAD_DIM)
    oa = oa.reshape(batch * seq, -1)
    ob = ob.reshape(batch * seq, -1)

    m_init = cache["mlstm"] if latent else None
    mres = [_mlstm_dir(proj, pr["mlstm_bias"], m_init, layer=layer, batch=batch, seq=seq, direction=dr,
                       emit_state=not latent) for dr in (0, 1)]

    y_tm, xfin = _s5(su_tm.reshape(seq * batch, S5_CH), pr, cache["s5"] if latent else None,
                     layer=layer, batch=batch, seq=seq)

    x = _outproj(x, mod, oa, ob, mres[0][0], mres[1][0], proj, y_tm.reshape(2, seq, batch * S5_CH), pr,
                 layer=layer, seq=seq)
    x = _ffn(x, mod, pr, layer=layer, which=1, seq=seq)

    new_ctx = None
    if not latent:
        kr = proj[:, _COL["kr"]:_COL["kr"] + MLA_ROPE]
        gv = proj[:, _COL["gv"]:_COL["gv"] + GQA_KV_HEADS * GQA_HEAD_DIM]
        xs = xfin.reshape(2, batch, 2, S5_GROUPS, S5_STATE).transpose(1, 0, 2, 3, 4)
        new_ctx = (ckvn.reshape(batch, seq, MLA_KV_LORA),
                   kr.reshape(batch, seq, MLA_ROPE),
                   kbn.reshape(batch, seq, GQA_KV_HEADS, GQA_HEAD_DIM),
                   gv.reshape(batch, seq, GQA_KV_HEADS, GQA_HEAD_DIM),
                   jnp.stack([mres[0][1], mres[1][1]], axis=1),
                   jnp.stack([mres[0][2], mres[1][2]], axis=1),
                   jnp.stack([mres[0][3][..., 0], mres[1][3][..., 0]], axis=1),
                   xs[:, :, 0], xs[:, :, 1])
    return x, new_ctx


def _permute_w_in(w_in):
    parts = [w_in[..., _ORIG[n][0]:_ORIG[n][0] + _ORIG[n][1]] for n in _ORDER]
    pad = PROJ_COLS - sum(p.shape[-1] for p in parts)
    parts.append(jnp.zeros(w_in.shape[:-1] + (pad,), w_in.dtype))
    return jnp.concatenate(parts, axis=-1).astype(BF16)


def _permute_w_uq(w_uq):
    depth, k, _ = w_uq.shape
    w = w_uq.reshape(depth, k, MLA_HEADS, MLA_NOPE + MLA_ROPE)
    w = jnp.pad(w, ((0, 0), (0, 0), (0, 0), (0, 2 * LANE - MLA_NOPE - MLA_ROPE)))
    return w.reshape(depth, k, MLA_HEADS * 2 * LANE).astype(BF16)


def kernel(x_prompt, x_sample, cache_mla_ckv, cache_mla_krope, cache_gqa_k, cache_gqa_v, state_mlstm_c, state_mlstm_n, state_mlstm_m, state_s5_re, state_s5_im, c, c_ctx, ada_w, ada_b, norm_g, ffn_w13, ffn_w2, w_in, mla_q_norm, mla_kv_norm, mla_w_uq, mla_w_ukv, gqa_q_norm, gqa_k_norm, mlstm_gate_b, s5_a_re, s5_a_im, s5_log_dt, s5_b_re, s5_b_im, s5_c_re, s5_c_im, s5_d, s5_w_glu, out_norm, w_out, final_norm):
    bc, sc, d = x_prompt.shape
    bl, sl, _ = x_sample.shape
    depth = ada_w.shape[0]
    past = cache_mla_ckv.shape[2]

    rows = ((1 + bl + SUBLANE - 1) // SUBLANE) * SUBLANE
    cvecs = jnp.concatenate([c_ctx[None, :], c, jnp.zeros((rows - 1 - bl, d), F32)], axis=0)
    mod_all = _modulation(cvecs, ada_w, ada_b).reshape(depth, rows, N_MOD, d)

    tables = _rope_tables(sl)

    abar, bd, cd = _s5_params(s5_a_re, s5_a_im, s5_log_dt, s5_b_re, s5_b_im, s5_c_re, s5_c_im)
    nmg = MLSTM_HEADS * 4
    row = lambda a: a.reshape(depth, 1, a.shape[-1])
    pr = {"norm_g": norm_g.reshape(depth, 3, 1, d),
          "ffn_w13": ffn_w13.astype(BF16), "ffn_w2": ffn_w2.astype(BF16),
          "w_in": _permute_w_in(w_in),
          "mla_q_norm": row(mla_q_norm), "mla_kv_norm": row(mla_kv_norm),
          "mla_w_uq": _permute_w_uq(mla_w_uq), "mla_w_ukv": mla_w_ukv.astype(BF16),
          "gqa_q_norm": row(gqa_q_norm), "gqa_k_norm": row(gqa_k_norm),
          "mlstm_bias": jnp.pad(mlstm_gate_b.reshape(depth, 1, nmg), ((0, 0), (0, 0), (MG_LANE, LANE - MG_LANE - nmg))),
          "s5_abar": abar, "s5_bd": bd, "s5_cd": cd,
          "s5_d": row(s5_d), "s5_w_glu": s5_w_glu.astype(BF16),
          "out_norm": out_norm.reshape(depth, 4, GROUP_WIDTH), "w_out": w_out.astype(BF16)}

    gkv = GQA_KV_HEADS * GQA_HEAD_DIM
    cache_k = cache_gqa_k.reshape(bl, depth, past, gkv)
    cache_v = cache_gqa_v.reshape(bl, depth, past, gkv)
    m0 = jnp.broadcast_to(state_mlstm_m[..., None], state_mlstm_n.shape)
    kr_pad = jnp.pad(cache_mla_krope, ((0, 0), (0, 0), (0, 0), (0, LANE - MLA_ROPE)))

    x_ctx = x_prompt.reshape(bc * sc, d)
    x_lat = x_sample.reshape(bl * sl, d)
    per_layer = []
    for l in range(depth):
        x_ctx, ctx_l = _trunk_layer(x_ctx, mod_all[l, 0:1], pr, layer=l, batch=bc, seq=sc, tables=None, cache=None)
        per_layer.append(ctx_l)

        x0 = jnp.concatenate([state_s5_re[:, l].reshape(bl, 2, S5_NS), state_s5_im[:, l].reshape(bl, 2, S5_NS)],
                             axis=-1).transpose(1, 0, 2)
        cache = {"mla_kv": tuple(_cache_kv(cache_mla_ckv, kr_pad, pr["mla_w_ukv"], layer=l)),
                 "gqa_kv": (cache_k, cache_v),
                 "mlstm": (state_mlstm_c, state_mlstm_n, m0),
                 "s5": x0}
        x_lat, _ = _trunk_layer(x_lat, mod_all[l, 1:1 + bl], pr, layer=l, batch=bl, seq=sl, tables=tables,
                                cache=cache)

    new_ctx = [jnp.stack([t[i] for t in per_layer], axis=1) for i in range(9)]
    y_prompt = _final_norm(x_ctx, final_norm[None, :]).reshape(bc, sc, d)
    y_sample = _final_norm(x_lat, final_norm[None, :]).reshape(bl, sl, d)
    return (y_prompt, y_sample, *new_ctx)
```

```python
import functools
import math

import jax
import jax.numpy as jnp
from jax import lax
from jax.experimental import pallas as pl
from jax.experimental.pallas import tpu as pltpu

F32 = jnp.float32
BF16 = jnp.bfloat16

EPS = 1e-6
ROPE_THETA = 10000.0
GRID_W = 64
N_MOD = 9
LOG2E = math.log2(math.e)

D_FF = 5632
MLA_HEADS, MLA_NOPE, MLA_ROPE, MLA_V = 4, 128, 64, 128
MLA_Q_LORA, MLA_KV_LORA = 384, 256
GQA_HEADS, GQA_KV_HEADS, GQA_HEAD_DIM = 4, 2, 128
MLSTM_HEADS, MLSTM_HEAD_DIM = 4, 128
S5_GROUPS, S5_GROUP, S5_STATE = 32, 16, 64
S5_CH = S5_GROUPS * S5_GROUP
S5_NS = S5_GROUPS * S5_STATE
GROUP_WIDTH = 512

LANE = 128
SUBLANE = 8
MXU_TILE = 256
NORM_ROWS = 128
VMEM_BYTES = 64 * 1024 * 1024
VMEM_LIMIT_BYTES = VMEM_BYTES - 8 * 1024 * 1024
FFN_VMEM_LIMIT_BYTES = VMEM_BYTES - 4 * 1024 * 1024

_ORIG = dict(cq=(0, 384), ckv=(384, 256), kr=(640, 64), gq=(704, 512), gk=(1216, 256), gv=(1472, 256),
             mq=(1728, 512), mk=(2240, 512), mv=(2752, 512), mo=(3264, 512), mg=(3776, 16), su=(3792, 512))
_ORDER = ("mq", "mk", "mv", "mo", "gq", "su", "gk", "gv", "ckv", "cq", "kr", "mg")
_COL = {}
_off = 0
for _n in _ORDER:
    _COL[_n] = _off
    _off += _ORIG[_n][1]
PROJ_COLS = ((_off + LANE - 1) // LANE) * LANE
KRMG_BLOCK = _COL["kr"] // LANE
MG_LANE = _COL["mg"] - _COL["kr"]


def _cparams(sem, vmem_limit=VMEM_LIMIT_BYTES):
    return pltpu.CompilerParams(dimension_semantics=sem, vmem_limit_bytes=vmem_limit)


def _pick(n, pref):
    for t in range(min(n, pref), 0, -1):
        if n % t == 0 and (t % SUBLANE == 0 or t == n):
            return t
    return n


def _layer_spec(arr, layer, nargs):
    zeros = (0,) * (arr.ndim - 1)
    imap = {1: lambda i: (layer,) + zeros, 2: lambda i, j: (layer,) + zeros,
            3: lambda i, j, k: (layer,) + zeros}[nargs]
    return pl.BlockSpec((None,) + arr.shape[1:], imap, pipeline_mode=pl.Buffered(1))


def _rms(x, g):
    return x * lax.rsqrt(jnp.mean(x * x, axis=-1, keepdims=True) + EPS) * g


def _modulate(x, g, scale, shift):
    return x * lax.rsqrt(jnp.mean(x * x, axis=-1, keepdims=True) + EPS) * (g * (1.0 + scale)) + shift


def _swap_halves(x, half):
    w = x.shape[-1]
    lane = lax.broadcasted_iota(jnp.int32, x.shape, x.ndim - 1)
    first = (lane & (2 * half - 1)) < half
    return jnp.where(first, pltpu.roll(x, w - half, axis=x.ndim - 1), pltpu.roll(x, half, axis=x.ndim - 1))


def _silu(x):
    return x * jax.nn.sigmoid(x)


def _mod_kernel(c_ref, w_ref, b_ref, o_ref):
    a = _silu(c_ref[...]).astype(BF16)
    o_ref[...] = jnp.dot(a, w_ref[...].astype(BF16), preferred_element_type=F32) + b_ref[...]


def _modulation(cvecs, ada_w, ada_b):
    depth, d, nm = ada_w.shape
    r = cvecs.shape[0]
    tn = _pick(nm, 1024)
    return pl.pallas_call(
        _mod_kernel,
        out_shape=jax.ShapeDtypeStruct((depth, r, nm), F32),
        grid=(depth, nm // tn),
        in_specs=[pl.BlockSpec((r, d), lambda l, j: (0, 0)),
                  pl.BlockSpec((None, d, tn), lambda l, j: (l, 0, j)),
                  pl.BlockSpec((None, 1, tn), lambda l, j: (l, 0, j))],
        out_specs=pl.BlockSpec((None, r, tn), lambda l, j: (l, 0, j)),
        compiler_params=_cparams(("parallel", "parallel")),
        name="adaln_mod",
    )(cvecs, ada_w, ada_b.reshape(depth, 1, nm))


def _rope_kernel(ca_ref, sa_ref, cb_ref, sb_ref, *, log2w):
    s = ca_ref.shape[0]
    t = lax.broadcasted_iota(jnp.int32, (s, 1), 0)
    row = lax.shift_right_logical(t, log2w).astype(F32)
    col = (t & ((1 << log2w) - 1)).astype(F32)

    def tables(width, off, rd):
        lane = lax.broadcasted_iota(jnp.int32, (1, width), 1)
        r = lane - off
        inr = (r >= 0) & (r < rd)
        half, quarter = rd // 2, rd // 4
        is_col = r >= half
        rr = jnp.where(is_col, r - half, r)
        second = rr >= quarter
        j = jnp.where(second, rr - quarter, rr).astype(F32)
        inv = jnp.exp(j * (-2.0 / half * math.log(ROPE_THETA)))
        ang = jnp.where(is_col, col, row) * inv
        sign = jnp.where(second, 1.0, -1.0)
        return jnp.where(inr, jnp.cos(ang), 1.0), jnp.where(inr, sign * jnp.sin(ang), 0.0)

    ca, sa = tables(2 * LANE, MLA_NOPE, MLA_ROPE)
    cb, sb = tables(GQA_HEAD_DIM, 0, GQA_HEAD_DIM)
    ca_ref[...] = ca
    sa_ref[...] = sa
    cb_ref[...] = cb
    sb_ref[...] = sb


def _rope_tables(s):
    log2w = GRID_W.bit_length() - 1
    assert 1 << log2w == GRID_W
    shp = lambda w: jax.ShapeDtypeStruct((s, w), F32)
    return pl.pallas_call(
        functools.partial(_rope_kernel, log2w=log2w),
        out_shape=(shp(2 * LANE), shp(2 * LANE), shp(GQA_HEAD_DIM), shp(GQA_HEAD_DIM)),
        name="rope_tables",
    )()


def _s5_params(a_re, a_im, log_dt, b_re, b_im, c_re, c_im):
    depth, nd = a_re.shape[:2]
    r = depth * nd
    are = a_re.reshape(r, S5_NS)
    aim = a_im.reshape(r, S5_NS)
    ldt = jnp.broadcast_to(log_dt[..., None], (depth, nd, S5_GROUPS, S5_STATE)).reshape(r, S5_NS)
    rep = lambda b: jnp.repeat(jnp.transpose(b, (0, 3, 1, 2)).reshape(depth, 1, S5_GROUP, S5_NS), nd, axis=1)
    bre = rep(b_re).reshape(r, S5_GROUP, S5_NS)
    bim = rep(b_im).reshape(r, S5_GROUP, S5_NS)
    v = jax.ShapeDtypeStruct((r, S5_NS), F32)
    m = jax.ShapeDtypeStruct((r, S5_GROUP, S5_NS), F32)
    ar, ai, bbr, bbi = pl.pallas_call(_s5_param_rows_kernel, out_shape=(v, v, m, m), name="s5_discretise")(
        are, aim, ldt, bre, bim)
    eye = jnp.eye(S5_GROUPS, dtype=F32)

    def dense_b(bb):
        bb = bb.reshape(r, S5_GROUP, S5_GROUPS, S5_STATE)
        return jnp.einsum("dcgp,gh->dgchp", bb, eye).reshape(r, S5_CH, S5_NS)

    def dense_c(cc):
        return jnp.einsum("lgcp,gh->lhpgc", cc, eye).reshape(depth, S5_NS, S5_CH)

    bd = jnp.concatenate([dense_b(bbr), dense_b(bbi)], axis=-1).astype(BF16).reshape(depth, nd, S5_CH, 2 * S5_NS)
    cd = jnp.concatenate([dense_c(c_re), -dense_c(c_im)], axis=1).astype(BF16)
    abar = jnp.concatenate([ar, ai], axis=-1).reshape(depth, nd, 1, 2 * S5_NS)
    return abar, bd, cd


def _s5_param_rows_kernel(are_ref, aim_ref, ldt_ref, bre_ref, bim_ref, oar_ref, oai_ref, obr_ref, obi_ref):
    lr, li = are_ref[...], aim_ref[...]
    dt = jnp.exp(ldt_ref[...])
    mag = jnp.exp(lr * dt)
    ar, ai = mag * jnp.cos(li * dt), mag * jnp.sin(li * dt)
    oar_ref[...] = ar
    oai_ref[...] = ai
    nr, ni = ar - 1.0, ai
    den = lr * lr + li * li
    fr, fi = (nr * lr + ni * li) / den, (ni * lr - nr * li) / den
    for d in range(fr.shape[0]):
        br, bi = bre_ref[d], bim_ref[d]
        obr_ref[d] = fr[d:d + 1] * br - fi[d:d + 1] * bi
        obi_ref[d] = fr[d:d + 1] * bi + fi[d:d + 1] * br


def _ffn_kernel(x_ref, mod_ref, g_ref, w1_ref, w3_ref, w2_ref, o_ref, h_ref, *, row0, slab):
    j = pl.program_id(1)
    nslab = h_ref.shape[0] // slab
    nr = math.gcd(slab, NORM_ROWS)

    def partial_ffn(rows):
        h = h_ref[rows, :]
        gate = jnp.dot(h, w1_ref[...], preferred_element_type=F32)
        up = jnp.dot(h, w3_ref[...], preferred_element_type=F32)
        act = (_silu(gate) * up).astype(BF16)
        return jnp.dot(act, w2_ref[...], preferred_element_type=F32)

    @pl.when(j == 0)
    def _():
        for r in range(nslab):
            for p in range(slab // nr):
                rows = slice(r * slab + p * nr, r * slab + (p + 1) * nr)
                h = _modulate(x_ref[rows, :], g_ref[...], mod_ref[row0 + 1:row0 + 2, :], mod_ref[row0:row0 + 1, :])
                h_ref[rows, :] = h.astype(BF16)
            rows = slice(r * slab, (r + 1) * slab)
            o_ref[rows, :] = partial_ffn(rows)

    @pl.when(j > 0)
    def _():
        for r in range(nslab):
            rows = slice(r * slab, (r + 1) * slab)
            o_ref[rows, :] += partial_ffn(rows)

    @pl.when(j == pl.num_programs(1) - 1)
    def _():
        o_ref[...] = x_ref[...] + (0.5 * mod_ref[row0 + 2:row0 + 3, :]) * o_ref[...]


def _ffn(x, mod, pr, *, layer, which, seq, tm_pref=1024, tf_pref=512):
    n, d = x.shape
    groups = mod.shape[0]
    tm = _pick(seq if groups > 1 else n, tm_pref)
    tf = _pick(D_FF, tf_pref)
    nf = D_FF // tf
    per = seq // tm if groups > 1 else 1
    mod_map = (lambda i, j: (i // per, 0, 0)) if groups > 1 else (lambda i, j: (0, 0, 0))
    return pl.pallas_call(
        functools.partial(_ffn_kernel, row0=6 * which, slab=_pick(tm, 512)),
        out_shape=jax.ShapeDtypeStruct((n, d), F32),
        grid=(n // tm, nf),
        in_specs=[pl.BlockSpec((tm, d), lambda i, j: (i, 0)),
                  pl.BlockSpec((None, N_MOD, d), mod_map),
                  pl.BlockSpec((None, None, 1, d), lambda i, j: (layer, 2 * which, 0, 0)),
                  pl.BlockSpec((None, None, d, tf), lambda i, j: (layer, which, 0, j)),
                  pl.BlockSpec((None, None, d, tf), lambda i, j: (layer, which, 0, nf + j)),
                  pl.BlockSpec((None, None, tf, d), lambda i, j: (layer, which, j, 0))],
        out_specs=pl.BlockSpec((tm, d), lambda i, j: (i, 0)),
        scratch_shapes=[pltpu.VMEM((tm, d), BF16)],
        compiler_params=_cparams(("parallel", "arbitrary"), FFN_VMEM_LIMIT_BYTES),
        name="ffn",
    )(x, mod, pr["norm_g"], pr["ffn_w13"], pr["ffn_w13"], pr["ffn_w2"])


def _inproj_kernel(x_ref, mod_ref, g_ref, w_ref, o_ref):
    h = _modulate(x_ref[...], g_ref[...], mod_ref[4:5, :], mod_ref[3:4, :]).astype(BF16)
    o_ref[...] = jnp.dot(h, w_ref[...], preferred_element_type=F32)


def _inproj(x, mod, pr, *, layer, seq, tm_pref=512):
    n, d = x.shape
    groups = mod.shape[0]
    tm = _pick(seq if groups > 1 else n, tm_pref)
    per = seq // tm if groups > 1 else 1
    mod_map = (lambda i: (i // per, 0, 0)) if groups > 1 else (lambda i: (0, 0, 0))
    return pl.pallas_call(
        _inproj_kernel,
        out_shape=jax.ShapeDtypeStruct((n, PROJ_COLS), F32),
        grid=(n // tm,),
        in_specs=[pl.BlockSpec((tm, d), lambda i: (i, 0)),
                  pl.BlockSpec((None, N_MOD, d), mod_map),
                  pl.BlockSpec((None, None, 1, d), lambda i: (layer, 1, 0, 0)),
                  _layer_spec(pr["w_in"], layer, 1)],
        out_specs=pl.BlockSpec((tm, PROJ_COLS), lambda i: (i, 0)),
        compiler_params=_cparams(("parallel",)),
        name="in_proj",
    )(x, mod, pr["norm_g"], pr["w_in"])


def _mla_kv(ckv_n, kr128, wukv_ref, ka_ref, va_ref):
    kv = jnp.dot(ckv_n.astype(BF16), wukv_ref[...], preferred_element_type=F32)
    krb = kr128.astype(BF16)
    hw = MLA_NOPE + MLA_V
    for h in range(MLA_HEADS):
        ka_ref[:, h * 2 * LANE:h * 2 * LANE + MLA_NOPE] = kv[:, h * hw:h * hw + MLA_NOPE].astype(BF16)
        ka_ref[:, h * 2 * LANE + MLA_NOPE:(h + 1) * 2 * LANE] = krb
        va_ref[:, h * MLA_V:(h + 1) * MLA_V] = kv[:, h * hw + MLA_NOPE:(h + 1) * hw].astype(BF16)


def _attn_prep_kernel(*refs, rope):
    if rope:
        (cq_ref, ckv_ref, krmg_ref, gq_ref, gk_ref, gv_ref, ca_ref, sa_ref, cb_ref, sb_ref,
         qng_ref, kvng_ref, wuq_ref, wukv_ref, gqg_ref, gkg_ref,
         qa_ref, ka_ref, va_ref, ckvn_ref, qb_ref, kb_ref, vb_ref, kbn_ref) = refs
    else:
        (cq_ref, ckv_ref, krmg_ref, gq_ref, gk_ref, gv_ref,
         qng_ref, kvng_ref, wuq_ref, wukv_ref, gqg_ref, gkg_ref,
         qa_ref, ka_ref, va_ref, ckvn_ref, qb_ref, kb_ref, vb_ref, kbn_ref) = refs

    cqn = _rms(cq_ref[...], qng_ref[...]).astype(BF16)
    qa = jnp.dot(cqn, wuq_ref[...], preferred_element_type=F32)
    if rope:
        ca = jnp.concatenate([ca_ref[...]] * MLA_HEADS, axis=1)
        sa = jnp.concatenate([sa_ref[...]] * MLA_HEADS, axis=1)
        qa = qa * ca + _swap_halves(qa, MLA_ROPE // 4) * sa
    qa_ref[...] = (qa * ((MLA_NOPE + MLA_ROPE) ** -0.5 * LOG2E)).astype(BF16)

    ckv_n = _rms(ckv_ref[...], kvng_ref[...])
    ckvn_ref[...] = ckv_n
    krmg = krmg_ref[...]
    lane = lax.broadcasted_iota(jnp.int32, krmg.shape, 1)
    if rope:
        krmg = krmg * ca_ref[:, LANE:] + _swap_halves(krmg, MLA_ROPE // 4) * sa_ref[:, LANE:]
    kr128 = jnp.where(lane < MLA_ROPE, krmg, 0.0)
    _mla_kv(ckv_n, kr128, wukv_ref, ka_ref, va_ref)

    hd = GQA_HEAD_DIM
    gq, gk = gq_ref[...], gk_ref[...]
    for h in range(GQA_HEADS):
        q = _rms(gq[:, h * hd:(h + 1) * hd], gqg_ref[...])
        if rope:
            q = q * cb_ref[...] + _swap_halves(q, hd // 4) * sb_ref[...]
        qb_ref[:, h * hd:(h + 1) * hd] = (q * (hd ** -0.5 * LOG2E)).astype(BF16)
    for h in range(GQA_KV_HEADS):
        k = _rms(gk[:, h * hd:(h + 1) * hd], gkg_ref[...])
        kbn_ref[:, h * hd:(h + 1) * hd] = k
        if rope:
            k = k * cb_ref[...] + _swap_halves(k, hd // 4) * sb_ref[...]
        kb_ref[:, h * hd:(h + 1) * hd] = k.astype(BF16)
    vb_ref[...] = gv_ref[...].astype(BF16)


def _attn_prep(proj, tables, pr, *, layer, seq, tm_pref=512):
    n = proj.shape[0]
    tm = _pick(seq, tm_pref)
    per = seq // tm
    rope = tables is not None

    def pblock(name, width):
        idx = _COL[name] // width
        assert idx * width == _COL[name]
        return pl.BlockSpec((tm, width), lambda i: (i, idx))

    in_specs = [pblock("cq", MLA_Q_LORA), pblock("ckv", MLA_KV_LORA),
                pl.BlockSpec((tm, LANE), lambda i: (i, KRMG_BLOCK)),
                pblock("gq", 512), pblock("gk", 256), pblock("gv", 256)]
    args = [proj] * 6
    if rope:
        in_specs += [pl.BlockSpec((tm, t.shape[1]), lambda i: (i % per, 0)) for t in tables]
        args += list(tables)
    ws = [pr["mla_q_norm"], pr["mla_kv_norm"], pr["mla_w_uq"], pr["mla_w_ukv"], pr["gqa_q_norm"], pr["gqa_k_norm"]]
    in_specs += [_layer_spec(w, layer, 1) for w in ws]
    args += ws
    widths = [(4 * 2 * LANE, BF16), (4 * 2 * LANE, BF16), (4 * MLA_V, BF16), (MLA_KV_LORA, F32),
              (GQA_HEADS * GQA_HEAD_DIM, BF16), (GQA_KV_HEADS * GQA_HEAD_DIM, BF16),
              (GQA_KV_HEADS * GQA_HEAD_DIM, BF16), (GQA_KV_HEADS * GQA_HEAD_DIM, F32)]
    return pl.pallas_call(
        functools.partial(_attn_prep_kernel, rope=rope),
        out_shape=tuple(jax.ShapeDtypeStruct((n, w), dt) for w, dt in widths),
        grid=(n // tm,),
        in_specs=in_specs,
        out_specs=tuple(pl.BlockSpec((tm, w), lambda i: (i, 0)) for w, _ in widths),
        compiler_params=_cparams(("parallel",)),
        name="attn_prep",
    )(*args)


def _cache_kv_kernel(ckv_ref, kr_ref, wukv_ref, ka_ref, va_ref):
    _mla_kv(ckv_ref[...], kr_ref[...], wukv_ref, ka_ref, va_ref)


def _cache_kv(ckv, kr128, wukv, *, layer):
    b, _, p, _ = ckv.shape
    return pl.pallas_call(
        _cache_kv_kernel,
        out_shape=(jax.ShapeDtypeStruct((b, p, 4 * 2 * LANE), BF16), jax.ShapeDtypeStruct((b, p, 4 * MLA_V), BF16)),
        grid=(b,),
        in_specs=[pl.BlockSpec((None, None, p, MLA_KV_LORA), lambda i: (i, layer, 0, 0)),
                  pl.BlockSpec((None, None, p, LANE), lambda i: (i, layer, 0, 0)),
                  _layer_spec(wukv, layer, 1)],
        out_specs=(pl.BlockSpec((None, p, 4 * 2 * LANE), lambda i: (i, 0, 0)),
                   pl.BlockSpec((None, p, 4 * MLA_V), lambda i: (i, 0, 0))),
        compiler_params=_cparams(("parallel",)),
        name="mla_cache_kv",
    )(ckv, kr128, wukv)


def _attn_kernel(q_ref, *refs, nseg, hp, g, dk, dv):
    o_ref = refs[-1]
    for j in range(hp):
        kv = j // g
        q = q_ref[:, j * dk:(j + 1) * dk]
        ks = [refs[2 * i][:, kv * dk:(kv + 1) * dk].astype(BF16) for i in range(nseg)]
        vs = [refs[2 * i + 1][:, kv * dv:(kv + 1) * dv].astype(BF16) for i in range(nseg)]
        ss = [lax.dot_general(q, k, (((1,), (1,)), ((), ())), preferred_element_type=F32) for k in ks]
        m = functools.reduce(jnp.maximum, [jnp.max(s, axis=-1, keepdims=True) for s in ss])
        ps = [jnp.exp2(s - m) for s in ss]
        l = functools.reduce(jnp.add, [jnp.sum(p, axis=-1, keepdims=True) for p in ps])
        o = functools.reduce(jnp.add, [jnp.dot(p.astype(BF16), v, preferred_element_type=F32)
                                       for p, v in zip(ps, vs)])
        o_ref[:, j * dv:(j + 1) * dv] = o / l


def _attention(q, segs, *, heads, kv_heads, dk, dv, tq_pref=512, hp=2):
    b, sq, _ = q.shape
    tq = _pick(sq, tq_pref)
    g = heads // kv_heads
    assert heads % hp == 0 and (hp % g == 0 or g % hp == 0)
    kvp = max(1, hp // g)
    kcol = (lambda hb: hb) if hp >= g else (lambda hb: hb * hp // g)
    in_specs = [pl.BlockSpec((None, tq, hp * dk), lambda bi, hb, i: (bi, i, hb))]
    args = [q]
    for k, v, layer in segs:
        sk = k.shape[-2]
        if layer is None:
            in_specs += [pl.BlockSpec((None, sk, kvp * dk), lambda bi, hb, i: (bi, 0, kcol(hb))),
                         pl.BlockSpec((None, sk, kvp * dv), lambda bi, hb, i: (bi, 0, kcol(hb)))]
        else:
            in_specs += [pl.BlockSpec((None, None, sk, kvp * dk),
                                      lambda bi, hb, i, layer=layer: (bi, layer, 0, kcol(hb))),
                         pl.BlockSpec((None, None, sk, kvp * dv),
                                      lambda bi, hb, i, layer=layer: (bi, layer, 0, kcol(hb)))]
        args += [k, v]
    return pl.pallas_call(
        functools.partial(_attn_kernel, nseg=len(segs), hp=hp, g=min(g, hp), dk=dk, dv=dv),
        out_shape=jax.ShapeDtypeStruct((b, sq, heads * dv), F32),
        grid=(b, heads // hp, sq // tq),
        in_specs=in_specs,
        out_specs=pl.BlockSpec((None, tq, hp * dv), lambda bi, hb, i: (bi, i, hb)),
        compiler_params=_cparams(("parallel", "parallel", "parallel")),
        name="attention",
    )(*args)


def _log_sigmoid(x):
    return jnp.minimum(x, 0.0) - jnp.log1p(jnp.exp(-jnp.abs(x)))


def _mlstm_kernel(*refs, rev, direction, has_init, emit_state):
    q_ref, k_ref, v_ref, gate_ref, bias_ref = refs[:5]
    pos = 5
    if has_init:
        c0_ref, n0_ref, m0_ref = refs[pos:pos + 3]
        pos += 3
    h_ref = refs[pos]
    pos += 1
    if emit_state:
        co_ref, no_ref, mo_ref = refs[pos:pos + 3]
        pos += 3
    c_s, n_s, m_s = refs[pos:pos + 3]

    c = pl.program_id(1)
    nh, hd = MLSTM_HEADS, MLSTM_HEAD_DIM
    cl = q_ref.shape[0]

    @pl.when(c == 0)
    def _():
        if has_init:
            c_s[...] = c0_ref[...]
            n_s[...] = n0_ref[...]
            m_s[...] = m0_ref[...]
        else:
            c_s[...] = jnp.zeros_like(c_s)
            n_s[...] = jnp.zeros_like(n_s)
            m_s[...] = jnp.zeros_like(m_s)

    g = gate_ref[...] + bias_ref[...]
    gt = g.T
    lf = _log_sigmoid(g)
    ri = lax.broadcasted_iota(jnp.int32, (cl, cl), 0)
    ci = lax.broadcasted_iota(jnp.int32, (cl, cl), 1)
    mask = (ri <= ci) if rev else (ri >= ci)
    tri = jnp.where(mask, 1.0, 0.0).astype(BF16)
    lf_hi = lf.astype(BF16)
    rem = lf - lf_hi.astype(F32)
    lf_mid = rem.astype(BF16)
    lf_lo = (rem - lf_mid.astype(F32)).astype(BF16)
    cum = (jnp.dot(tri, lf_hi, preferred_element_type=F32) + jnp.dot(tri, lf_mid, preferred_element_type=F32)
           + jnp.dot(tri, lf_lo, preferred_element_type=F32))
    cumt = cum.T
    last = 0 if rev else cl - 1
    li0 = MG_LANE + direction * 2 * nh
    lf0 = li0 + nh

    for h in range(nh):
        cum_c = cum[:, lf0 + h:lf0 + h + 1]
        cum_r = cumt[lf0 + h:lf0 + h + 1, :]
        li_r = gt[li0 + h:li0 + h + 1, :]
        li_c = g[:, li0 + h:li0 + h + 1]
        m_prev = m_s[h:h + 1, 0:1]
        dmat = jnp.where(mask, cum_c - cum_r + li_r, -jnp.inf)
        inter = cum_c + m_prev
        m_t = jnp.maximum(inter, jnp.max(dmat, axis=1, keepdims=True))
        w = jnp.exp(dmat - m_t)
        qh = q_ref[:, h * hd:(h + 1) * hd].astype(BF16)
        kh = k_ref[:, h * hd:(h + 1) * hd] * (hd ** -0.5)
        vh = v_ref[:, h * hd:(h + 1) * hd].astype(BF16)
        kt = kh.T
        sc = jnp.dot(qh, kt.astype(BF16), preferred_element_type=F32) * w
        inter_w = jnp.exp(inter - m_t)
        cmat = c_s[h]
        nrow = n_s[h:h + 1, :]
        num = (jnp.dot(sc.astype(BF16), vh, preferred_element_type=F32)
               + inter_w * jnp.dot(qh, cmat.astype(BF16), preferred_element_type=F32))
        qn = jnp.sum(q_ref[:, h * hd:(h + 1) * hd] * nrow, axis=1, keepdims=True)
        den = jnp.sum(sc, axis=1, keepdims=True) + inter_w * qn
        h_ref[:, h * hd:(h + 1) * hd] = num / jnp.maximum(jnp.abs(den), jnp.exp(-m_t))

        tot = cum_c[last:last + 1, :]
        g_r = tot - cum_r + li_r
        g_c = tot - cum_c + li_c
        m_new = jnp.maximum(tot + m_prev, jnp.max(g_r, axis=1, keepdims=True))
        decay = jnp.exp(tot + m_prev - m_new)
        ws_r = jnp.exp(g_r - m_new)
        ws_c = jnp.exp(g_c - m_new)
        c_s[h] = decay * cmat + jnp.dot((kt * ws_r).astype(BF16), vh, preferred_element_type=F32)
        n_s[h:h + 1, :] = decay * nrow + jnp.sum(kh * ws_c, axis=0, keepdims=True)
        m_s[h:h + 1, :] = jnp.broadcast_to(m_new, (1, hd))

    if emit_state:
        @pl.when(c == pl.num_programs(1) - 1)
        def _():
            co_ref[...] = c_s[...]
            no_ref[...] = n_s[...]
            mo_ref[...] = m_s[...]


def _mlstm_dir(proj, bias128, init, *, layer, batch, seq, direction, emit_state, chunk_pref=256):
    n = proj.shape[0]
    cl = _pick(seq, chunk_pref)
    nc = seq // cl
    rev = direction == 1
    nh, hd = MLSTM_HEADS, MLSTM_HEAD_DIM
    row = (lambda b, c: b * nc + (nc - 1 - c)) if rev else (lambda b, c: b * nc + c)
    qkv = lambda name: pl.BlockSpec((cl, nh * hd), lambda b, c: (row(b, c), _COL[name] // (nh * hd)))
    in_specs = [qkv("mq"), qkv("mk"), qkv("mv"),
                pl.BlockSpec((cl, LANE), lambda b, c: (row(b, c), KRMG_BLOCK)),
                pl.BlockSpec((None, 1, LANE), lambda b, c: (layer, 0, 0))]
    args = [proj, proj, proj, proj, bias128]
    has_init = init is not None
    if has_init:
        c0, n0, m0 = init
        in_specs += [pl.BlockSpec((None, None, None, nh, hd, hd), lambda b, c: (b, layer, direction, 0, 0, 0)),
                     pl.BlockSpec((None, None, None, nh, hd), lambda b, c: (b, layer, direction, 0, 0)),
                     pl.BlockSpec((None, None, None, nh, hd), lambda b, c: (b, layer, direction, 0, 0))]
        args += [c0, n0, m0]
    out_shape = [jax.ShapeDtypeStruct((n, nh * hd), F32)]
    out_specs = [pl.BlockSpec((cl, nh * hd), lambda b, c: (row(b, c), 0))]
    if emit_state:
        out_shape += [jax.ShapeDtypeStruct((batch, nh, hd, hd), F32), jax.ShapeDtypeStruct((batch, nh, hd), F32),
                      jax.ShapeDtypeStruct((batch, nh, hd), F32)]
        out_specs += [pl.BlockSpec((None, nh, hd, hd), lambda b, c: (b, 0, 0, 0)),
                      pl.BlockSpec((None, nh, hd), lambda b, c: (b, 0, 0)),
                      pl.BlockSpec((None, nh, hd), lambda b, c: (b, 0, 0))]
    return pl.pallas_call(
        functools.partial(_mlstm_kernel, rev=rev, direction=direction, has_init=has_init, emit_state=emit_state),
        out_shape=tuple(out_shape),
        grid=(batch, nc),
        in_specs=in_specs,
        out_specs=tuple(out_specs),
        scratch_shapes=[pltpu.VMEM((nh, hd, hd), F32), pltpu.VMEM((nh, hd), F32), pltpu.VMEM((nh, hd), F32)],
        compiler_params=_cparams(("parallel", "arbitrary")),
        name="mlstm",
    )(*args)


_S5_STATE_TILES = 2 * S5_NS // MXU_TILE
_S5_GROUPS_PER_TILE = MXU_TILE // S5_STATE


def _s5_channel_tile(state_tile):
    first_group = (state_tile % (S5_NS // MXU_TILE)) * _S5_GROUPS_PER_TILE
    return first_group * S5_GROUP // MXU_TILE


def _to_time_major_kernel(u_ref, p_ref, o_ref):
    sg, tsteps, ch = u_ref.shape
    u = u_ref[...].reshape(sg * tsteps, ch).astype(BF16)
    o_ref[...] = jnp.dot(p_ref[...], u, preferred_element_type=F32).astype(BF16)


def _from_time_major_kernel(yf_ref, yb_ref, pt_ref, o_ref):
    sg, tsteps, ch = o_ref.shape
    y = yf_ref[...] + yb_ref[...]
    hi = y.astype(BF16)
    lo = (y - hi.astype(F32)).astype(BF16)
    pt = pt_ref[...]
    out = jnp.dot(pt, hi, preferred_element_type=F32) + jnp.dot(pt, lo, preferred_element_type=F32)
    o_ref[...] = out.reshape(sg, tsteps, ch)


def _s5_kernel(*refs, has_init):
    if has_init:
        u_ref, bd_ref, a_ref, cd_ref, x0_ref, y_ref, xf_ref, bu_s, x_s = refs
    else:
        u_ref, bd_ref, a_ref, cd_ref, y_ref, xf_ref, bu_s, x_s = refs
    d = pl.program_id(0)
    c = pl.program_id(2)
    ns = S5_NS
    sg = x_s.shape[0]
    tsteps = u_ref.shape[0] // sg
    mt = MXU_TILE

    @pl.when(c == 0)
    def _():
        if has_init:
            x_s[...] = x0_ref[...]
        else:
            x_s[...] = jnp.zeros_like(x_s)

    u = u_ref[...]
    for st in range(_S5_STATE_TILES):
        ct = _s5_channel_tile(st)
        bu_s[:, st * mt:(st + 1) * mt] = jnp.dot(u[:, ct * mt:(ct + 1) * mt],
                                                 bd_ref[ct * mt:(ct + 1) * mt, st * mt:(st + 1) * mt],
                                                 preferred_element_type=F32)
    a_re = jnp.broadcast_to(a_ref[:, :ns], (sg, ns))
    a_im = jnp.broadcast_to(a_ref[:, ns:], (sg, ns))

    def body(t, carry):
        xr, xi = carry
        tt = t + d * (tsteps - 1 - 2 * t)
        r0 = pl.multiple_of(tt * sg, sg)
        br = bu_s[pl.ds(r0, sg), :ns]
        bi = bu_s[pl.ds(r0, sg), ns:]
        nr = a_re * xr - a_im * xi + br
        ni = a_re * xi + a_im * xr + bi
        bu_s[pl.ds(r0, sg), :ns] = nr
        bu_s[pl.ds(r0, sg), ns:] = ni
        return nr, ni

    xr, xi = lax.fori_loop(0, tsteps, body, (x_s[:, :ns], x_s[:, ns:]))
    x_s[:, :ns] = xr
    x_s[:, ns:] = xi

    for ct in range(S5_CH // mt):
        acc = None
        for st in range(_S5_STATE_TILES):
            if _s5_channel_tile(st) != ct:
                continue
            t = jnp.dot(bu_s[:, st * mt:(st + 1) * mt].astype(BF16),
                        cd_ref[st * mt:(st + 1) * mt, ct * mt:(ct + 1) * mt], preferred_element_type=F32)
            acc = t if acc is None else acc + t
        y_ref[:, ct * mt:(ct + 1) * mt] = acc

    @pl.when(c == pl.num_programs(2) - 1)
    def _():
        xf_ref[...] = x_s[...]


def _s5(proj3, pr, x0, *, layer, rows_pref=512):
    batch, seq, _ = proj3.shape
    sg = SUBLANE if batch % SUBLANE == 0 else batch
    ng = batch // sg
    tsteps = _pick(seq, max(SUBLANE, rows_pref // sg))
    rows = sg * tsteps
    nc = seq // tsteps
    r = jnp.arange(rows)
    perm = (jnp.arange(rows)[None, :] == ((r % sg) * tsteps + r // sg)[:, None]).astype(BF16)
    const2 = lambda a: pl.BlockSpec(a.shape, lambda g, c: (0,) * a.ndim, pipeline_mode=pl.Buffered(1))

    u_tm = pl.pallas_call(
        _to_time_major_kernel,
        out_shape=jax.ShapeDtypeStruct((ng, seq * sg, S5_CH), BF16),
        grid=(ng, nc),
        in_specs=[pl.BlockSpec((sg, tsteps, S5_CH), lambda g, c: (g, c, _COL["su"] // S5_CH)), const2(perm)],
        out_specs=pl.BlockSpec((None, rows, S5_CH), lambda g, c: (g, c, 0)),
        compiler_params=_cparams(("parallel", "parallel")),
        name="s5_to_time_major",
    )(proj3, perm)

    cpos = lambda d, c: c + d * (nc - 1 - 2 * c)
    has_init = x0 is not None
    in_specs = [pl.BlockSpec((None, rows, S5_CH), lambda d, g, c: (g, cpos(d, c), 0)),
                pl.BlockSpec((None, None, S5_CH, 2 * S5_NS), lambda d, g, c: (layer, d, 0, 0)),
                pl.BlockSpec((None, None, 1, 2 * S5_NS), lambda d, g, c: (layer, d, 0, 0)),
                _layer_spec(pr["s5_cd"], layer, 3)]
    args = [u_tm, pr["s5_bd"], pr["s5_abar"], pr["s5_cd"]]
    if has_init:
        in_specs.append(pl.BlockSpec((None, sg, 2 * S5_NS), lambda d, g, c: (d, g, 0)))
        args.append(x0)
    y_tm, xfin = pl.pallas_call(
        functools.partial(_s5_kernel, has_init=has_init),
        out_shape=(jax.ShapeDtypeStruct((2, ng, seq * sg, S5_CH), F32),
                   jax.ShapeDtypeStruct((2, batch, 2 * S5_NS), F32)),
        grid=(2, ng, nc),
        in_specs=in_specs,
        out_specs=(pl.BlockSpec((None, None, rows, S5_CH), lambda d, g, c: (d, g, cpos(d, c), 0)),
                   pl.BlockSpec((None, sg, 2 * S5_NS), lambda d, g, c: (d, g, 0))),
        scratch_shapes=[pltpu.VMEM((rows, 2 * S5_NS), F32), pltpu.VMEM((sg, 2 * S5_NS), F32)],
        compiler_params=_cparams(("parallel", "parallel", "arbitrary")),
        name="s5_scan",
    )(*args)

    y = pl.pallas_call(
        _from_time_major_kernel,
        out_shape=jax.ShapeDtypeStruct((batch, seq, S5_CH), F32),
        grid=(ng, nc),
        in_specs=[pl.BlockSpec((None, None, rows, S5_CH), lambda g, c: (0, g, c, 0)),
                  pl.BlockSpec((None, None, rows, S5_CH), lambda g, c: (1, g, c, 0)),
                  const2(perm)],
        out_specs=pl.BlockSpec((sg, tsteps, S5_CH), lambda g, c: (g, c, 0)),
        compiler_params=_cparams(("parallel", "parallel")),
        name="s5_from_time_major",
    )(y_tm, y_tm, perm.T)
    return y, xfin


def _gelu_tanh(x):
    return 0.5 * x * (1.0 + jnp.tanh(math.sqrt(2.0 / math.pi) * (x + 0.044715 * (x * x * x))))


def _outproj_kernel(x_ref, mod_ref, oa_ref, ob_ref, hf_ref, hb_ref, mo_ref, y_ref, su_ref,
                    d_ref, wglu_ref, on_ref, wout_ref, o_ref):
    oc = jax.nn.sigmoid(mo_ref[...]) * (hf_ref[...] + hb_ref[...])
    y = _gelu_tanh(y_ref[...] + d_ref[...] * su_ref[...])
    od = y * jax.nn.sigmoid(jnp.dot(y.astype(BF16), wglu_ref[...], preferred_element_type=F32))
    acc = None
    for gi, part in enumerate((oa_ref[...], ob_ref[...], oc, od)):
        nrm = _rms(part, on_ref[gi:gi + 1, :]).astype(BF16)
        t = jnp.dot(nrm, wout_ref[gi * GROUP_WIDTH:(gi + 1) * GROUP_WIDTH, :], preferred_element_type=F32)
        acc = t if acc is None else acc + t
    o_ref[...] = x_ref[...] + mod_ref[5:6, :] * acc


def _outproj(x, mod, oa, ob, hf, hb, proj, y, pr, *, layer, seq, tm_pref=512):
    n, d = x.shape
    groups = mod.shape[0]
    tm = _pick(seq if groups > 1 else n, tm_pref)
    per = seq // tm if groups > 1 else 1
    mod_map = (lambda i: (i // per, 0, 0)) if groups > 1 else (lambda i: (0, 0, 0))
    gw = GROUP_WIDTH
    rowblk = pl.BlockSpec((tm, gw), lambda i: (i, 0))
    ws = [pr["s5_d"], pr["s5_w_glu"], pr["out_norm"], pr["w_out"]]
    return pl.pallas_call(
        _outproj_kernel,
        out_shape=jax.ShapeDtypeStruct((n, d), F32),
        grid=(n // tm,),
        in_specs=[pl.BlockSpec((tm, d), lambda i: (i, 0)),
                  pl.BlockSpec((None, N_MOD, d), mod_map),
                  rowblk, rowblk, rowblk, rowblk,
                  pl.BlockSpec((tm, gw), lambda i: (i, _COL["mo"] // gw)),
                  rowblk,
                  pl.BlockSpec((tm, gw), lambda i: (i, _COL["su"] // gw))] + [_layer_spec(w, layer, 1) for w in ws],
        out_specs=pl.BlockSpec((tm, d), lambda i: (i, 0)),
        compiler_params=_cparams(("parallel",)),
        name="merge_out_proj",
    )(x, mod, oa, ob, hf, hb, proj, y, proj, *ws)


def _final_norm_kernel(x_ref, g_ref, o_ref):
    o_ref[...] = _rms(x_ref[...], g_ref[...])


def _final_norm(x, g, tm_pref=1024):
    n, d = x.shape
    tm = _pick(n, tm_pref)
    return pl.pallas_call(
        _final_norm_kernel,
        out_shape=jax.ShapeDtypeStruct((n, d), F32),
        grid=(n // tm,),
        in_specs=[pl.BlockSpec((tm, d), lambda i: (i, 0)), pl.BlockSpec((1, d), lambda i: (0, 0))],
        out_specs=pl.BlockSpec((tm, d), lambda i: (i, 0)),
        compiler_params=_cparams(("parallel",)),
        name="final_norm",
    )(x, g)


def _trunk_layer(x, mod, pr, *, layer, batch, seq, tables, cache):
    latent = cache is not None
    x = _ffn(x, mod, pr, layer=layer, which=0, seq=seq)
    proj = _inproj(x, mod, pr, layer=layer, seq=seq)

    qa, ka, va, ckvn, qb, kb, vb, kbn = _attn_prep(proj, tables, pr, layer=layer, seq=seq)
    r3 = lambda a: a.reshape(batch, seq, a.shape[-1])
    segs_a = [(r3(ka), r3(va), None)]
    segs_b = [(r3(kb), r3(vb), None)]
    if latent:
        segs_a.insert(0, cache["mla_kv"] + (None,))
        segs_b.insert(0, cache["gqa_kv"] + (layer,))
    oa = _attention(r3(qa), segs_a, heads=MLA_HEADS, kv_heads=MLA_HEADS, dk=2 * LANE, dv=MLA_V)
    ob = _attention(r3(qb), segs_b, heads=GQA_HEADS, kv_heads=GQA_KV_HEADS, dk=GQA_HEAD_DIM, dv=GQA_HEAD_DIM)
    oa = oa.reshape(batch * seq, -1)
    ob = ob.reshape(batch * seq, -1)

    m_init = cache["mlstm"] if latent else None
    mres = [_mlstm_dir(proj, pr["mlstm_bias"], m_init, layer=layer, batch=batch, seq=seq, direction=dr,
                       emit_state=not latent) for dr in (0, 1)]

    y, xfin = _s5(proj.reshape(batch, seq, PROJ_COLS), pr, cache["s5"] if latent else None, layer=layer)

    x = _outproj(x, mod, oa, ob, mres[0][0], mres[1][0], proj, y.reshape(batch * seq, S5_CH), pr,
                 layer=layer, seq=seq)
    x = _ffn(x, mod, pr, layer=layer, which=1, seq=seq)

    new_ctx = None
    if not latent:
        kr = proj[:, _COL["kr"]:_COL["kr"] + MLA_ROPE]
        gv = proj[:, _COL["gv"]:_COL["gv"] + GQA_KV_HEADS * GQA_HEAD_DIM]
        xs = xfin.reshape(2, batch, 2, S5_GROUPS, S5_STATE).transpose(1, 0, 2, 3, 4)
        new_ctx = (ckvn.reshape(batch, seq, MLA_KV_LORA),
                   kr.reshape(batch, seq, MLA_ROPE),
                   kbn.reshape(batch, seq, GQA_KV_HEADS, GQA_HEAD_DIM),
                   gv.reshape(batch, seq, GQA_KV_HEADS, GQA_HEAD_DIM),
                   jnp.stack([mres[0][1], mres[1][1]], axis=1),
                   jnp.stack([mres[0][2], mres[1][2]], axis=1),
                   jnp.stack([mres[0][3][..., 0], mres[1][3][..., 0]], axis=1),
                   xs[:, :, 0], xs[:, :, 1])
    return x, new_ctx


def _permute_w_in(w_in):
    parts = [w_in[..., _ORIG[n][0]:_ORIG[n][0] + _ORIG[n][1]] for n in _ORDER]
    pad = PROJ_COLS - sum(p.shape[-1] for p in parts)
    parts.append(jnp.zeros(w_in.shape[:-1] + (pad,), w_in.dtype))
    return jnp.concatenate(parts, axis=-1).astype(BF16)


def _permute_w_uq(w_uq):
    depth, k, _ = w_uq.shape
    w = w_uq.reshape(depth, k, MLA_HEADS, MLA_NOPE + MLA_ROPE)
    w = jnp.pad(w, ((0, 0), (0, 0), (0, 0), (0, 2 * LANE - MLA_NOPE - MLA_ROPE)))
    return w.reshape(depth, k, MLA_HEADS * 2 * LANE).astype(BF16)


def kernel(x_prompt, x_sample, cache_mla_ckv, cache_mla_krope, cache_gqa_k, cache_gqa_v, state_mlstm_c, state_mlstm_n, state_mlstm_m, state_s5_re, state_s5_im, c, c_ctx, ada_w, ada_b, norm_g, ffn_w13, ffn_w2, w_in, mla_q_norm, mla_kv_norm, mla_w_uq, mla_w_ukv, gqa_q_norm, gqa_k_norm, mlstm_gate_b, s5_a_re, s5_a_im, s5_log_dt, s5_b_re, s5_b_im, s5_c_re, s5_c_im, s5_d, s5_w_glu, out_norm, w_out, final_norm):
    bc, sc, d = x_prompt.shape
    bl, sl, _ = x_sample.shape
    depth = ada_w.shape[0]
    past = cache_mla_ckv.shape[2]

    rows = ((1 + bl + SUBLANE - 1) // SUBLANE) * SUBLANE
    cvecs = jnp.concatenate([c_ctx[None, :], c, jnp.zeros((rows - 1 - bl, d), F32)], axis=0)
    mod_all = _modulation(cvecs, ada_w, ada_b).reshape(depth, rows, N_MOD, d)

    tables = _rope_tables(sl)

    abar, bd, cd = _s5_params(s5_a_re, s5_a_im, s5_log_dt, s5_b_re, s5_b_im, s5_c_re, s5_c_im)
    nmg = MLSTM_HEADS * 4
    row = lambda a: a.reshape(depth, 1, a.shape[-1])
    pr = {"norm_g": norm_g.reshape(depth, 3, 1, d),
          "ffn_w13": ffn_w13.astype(BF16), "ffn_w2": ffn_w2.astype(BF16),
          "w_in": _permute_w_in(w_in),
          "mla_q_norm": row(mla_q_norm), "mla_kv_norm": row(mla_kv_norm),
          "mla_w_uq": _permute_w_uq(mla_w_uq), "mla_w_ukv": mla_w_ukv.astype(BF16),
          "gqa_q_norm": row(gqa_q_norm), "gqa_k_norm": row(gqa_k_norm),
          "mlstm_bias": jnp.pad(mlstm_gate_b.reshape(depth, 1, nmg), ((0, 0), (0, 0), (MG_LANE, LANE - MG_LANE - nmg))),
          "s5_abar": abar, "s5_bd": bd, "s5_cd": cd,
          "s5_d": row(s5_d), "s5_w_glu": s5_w_glu.astype(BF16),
          "out_norm": out_norm.reshape(depth, 4, GROUP_WIDTH), "w_out": w_out.astype(BF16)}

    gkv = GQA_KV_HEADS * GQA_HEAD_DIM
    cache_k = cache_gqa_k.reshape(bl, depth, past, gkv)
    cache_v = cache_gqa_v.reshape(bl, depth, past, gkv)
    m0 = jnp.broadcast_to(state_mlstm_m[..., None], state_mlstm_n.shape)
    kr_pad = jnp.pad(cache_mla_krope, ((0, 0), (0, 0), (0, 0), (0, LANE - MLA_ROPE)))

    x_ctx = x_prompt.reshape(bc * sc, d)
    x_lat = x_sample.reshape(bl * sl, d)
    per_layer = []
    for l in range(depth):
        x_ctx, ctx_l = _trunk_layer(x_ctx, mod_all[l, 0:1], pr, layer=l, batch=bc, seq=sc, tables=None, cache=None)
        per_layer.append(ctx_l)

        x0 = jnp.concatenate([state_s5_re[:, l].reshape(bl, 2, S5_NS), state_s5_im[:, l].reshape(bl, 2, S5_NS)],
                             axis=-1).transpose(1, 0, 2)
        cache = {"mla_kv": tuple(_cache_kv(cache_mla_ckv, kr_pad, pr["mla_w_ukv"], layer=l)),
                 "gqa_kv": (cache_k, cache_v),
                 "mlstm": (state_mlstm_c, state_mlstm_n, m0),
                 "s5": x0}
        x_lat, _ = _trunk_layer(x_lat, mod_all[l, 1:1 + bl], pr, layer=l, batch=bl, seq=sl, tables=tables,
                                cache=cache)

    new_ctx = [jnp.stack([t[i] for t in per_layer], axis=1) for i in range(9)]
    y_prompt = _final_norm(x_ctx, final_norm[None, :]).reshape(bc, sc, d)
    y_sample = _final_norm(x_lat, final_norm[None, :]).reshape(bl, sl, d)
    return (y_prompt, y_sample, *new_ctx)
```

```python
import functools
import math

import jax
import jax.numpy as jnp
from jax import lax
from jax.experimental import pallas as pl
from jax.experimental.pallas import tpu as pltpu

F32 = jnp.float32
BF16 = jnp.bfloat16

EPS = 1e-6
ROPE_THETA = 10000.0
GRID_W = 64
N_MOD = 9
LOG2E = math.log2(math.e)

D_FF = 5632
MLA_HEADS, MLA_NOPE, MLA_ROPE, MLA_V = 4, 128, 64, 128
MLA_Q_LORA, MLA_KV_LORA = 384, 256
GQA_HEADS, GQA_KV_HEADS, GQA_HEAD_DIM = 4, 2, 128
MLSTM_HEADS, MLSTM_HEAD_DIM = 4, 128
S5_GROUPS, S5_GROUP, S5_STATE = 32, 16, 64
S5_CH = S5_GROUPS * S5_GROUP
S5_NS = S5_GROUPS * S5_STATE
GROUP_WIDTH = 512

LANE = 128
SUBLANE = 8
MXU_TILE = 256
NORM_ROWS = 128
FFN_TILE = 512
VMEM_BYTES = 64 * 1024 * 1024
VMEM_LIMIT_BYTES = VMEM_BYTES - 8 * 1024 * 1024
FFN_VMEM_LIMIT_BYTES = VMEM_BYTES - 4 * 1024 * 1024

_ORIG = dict(cq=(0, 384), ckv=(384, 256), kr=(640, 64), gq=(704, 512), gk=(1216, 256), gv=(1472, 256),
             mq=(1728, 512), mk=(2240, 512), mv=(2752, 512), mo=(3264, 512), mg=(3776, 16), su=(3792, 512))
_ORDER = ("mq", "mk", "mv", "mo", "gq", "su", "gk", "gv", "ckv", "cq", "kr", "mg")
_COL = {}
_off = 0
for _n in _ORDER:
    _COL[_n] = _off
    _off += _ORIG[_n][1]
PROJ_COLS = ((_off + LANE - 1) // LANE) * LANE
KRMG_BLOCK = _COL["kr"] // LANE
MG_LANE = _COL["mg"] - _COL["kr"]


def _cparams(sem, vmem_limit=VMEM_LIMIT_BYTES):
    return pltpu.CompilerParams(dimension_semantics=sem, vmem_limit_bytes=vmem_limit)


def _pick(n, pref):
    for t in range(min(n, pref), 0, -1):
        if n % t == 0 and (t % SUBLANE == 0 or t == n):
            return t
    return n


def _layer_spec(arr, layer, nargs):
    zeros = (0,) * (arr.ndim - 1)
    imap = {1: lambda i: (layer,) + zeros, 2: lambda i, j: (layer,) + zeros,
            3: lambda i, j, k: (layer,) + zeros}[nargs]
    return pl.BlockSpec((None,) + arr.shape[1:], imap, pipeline_mode=pl.Buffered(1))


def _rms(x, g):
    return x * lax.rsqrt(jnp.mean(x * x, axis=-1, keepdims=True) + EPS) * g


def _modulate(x, g, scale, shift):
    return x * lax.rsqrt(jnp.mean(x * x, axis=-1, keepdims=True) + EPS) * (g * (1.0 + scale)) + shift


def _swap_halves(x, half):
    w = x.shape[-1]
    lane = lax.broadcasted_iota(jnp.int32, x.shape, x.ndim - 1)
    first = (lane & (2 * half - 1)) < half
    return jnp.where(first, pltpu.roll(x, w - half, axis=x.ndim - 1), pltpu.roll(x, half, axis=x.ndim - 1))


def _silu(x):
    return x * jax.nn.sigmoid(x)


def _mod_kernel(c_ref, w_ref, b_ref, o_ref):
    a = _silu(c_ref[...]).astype(BF16)
    o_ref[...] = jnp.dot(a, w_ref[...].astype(BF16), preferred_element_type=F32) + b_ref[...]


def _modulation(cvecs, ada_w, ada_b):
    depth, d, nm = ada_w.shape
    r = cvecs.shape[0]
    tn = _pick(nm, 1024)
    return pl.pallas_call(
        _mod_kernel,
        out_shape=jax.ShapeDtypeStruct((depth, r, nm), F32),
        grid=(depth, nm // tn),
        in_specs=[pl.BlockSpec((r, d), lambda l, j: (0, 0)),
                  pl.BlockSpec((None, d, tn), lambda l, j: (l, 0, j)),
                  pl.BlockSpec((None, 1, tn), lambda l, j: (l, 0, j))],
        out_specs=pl.BlockSpec((None, r, tn), lambda l, j: (l, 0, j)),
        compiler_params=_cparams(("parallel", "parallel")),
        name="adaln_mod",
    )(cvecs, ada_w, ada_b.reshape(depth, 1, nm))


def _rope_kernel(ca_ref, sa_ref, cb_ref, sb_ref, *, log2w):
    s = ca_ref.shape[0]
    t = lax.broadcasted_iota(jnp.int32, (s, 1), 0)
    row = lax.shift_right_logical(t, log2w).astype(F32)
    col = (t & ((1 << log2w) - 1)).astype(F32)

    def tables(width, off, rd):
        lane = lax.broadcasted_iota(jnp.int32, (1, width), 1)
        r = lane - off
        inr = (r >= 0) & (r < rd)
        half, quarter = rd // 2, rd // 4
        is_col = r >= half
        rr = jnp.where(is_col, r - half, r)
        second = rr >= quarter
        j = jnp.where(second, rr - quarter, rr).astype(F32)
        inv = jnp.exp(j * (-2.0 / half * math.log(ROPE_THETA)))
        ang = jnp.where(is_col, col, row) * inv
        sign = jnp.where(second, 1.0, -1.0)
        return jnp.where(inr, jnp.cos(ang), 1.0), jnp.where(inr, sign * jnp.sin(ang), 0.0)

    ca, sa = tables(2 * LANE, MLA_NOPE, MLA_ROPE)
    cb, sb = tables(GQA_HEAD_DIM, 0, GQA_HEAD_DIM)
    ca_ref[...] = ca
    sa_ref[...] = sa
    cb_ref[...] = cb
    sb_ref[...] = sb


def _rope_tables(s):
    log2w = GRID_W.bit_length() - 1
    assert 1 << log2w == GRID_W
    shp = lambda w: jax.ShapeDtypeStruct((s, w), F32)
    return pl.pallas_call(
        functools.partial(_rope_kernel, log2w=log2w),
        out_shape=(shp(2 * LANE), shp(2 * LANE), shp(GQA_HEAD_DIM), shp(GQA_HEAD_DIM)),
        name="rope_tables",
    )()


def _s5_params(a_re, a_im, log_dt, b_re, b_im, c_re, c_im):
    depth, nd = a_re.shape[:2]
    r = depth * nd
    are = a_re.reshape(r, S5_NS)
    aim = a_im.reshape(r, S5_NS)
    ldt = jnp.broadcast_to(log_dt[..., None], (depth, nd, S5_GROUPS, S5_STATE)).reshape(r, S5_NS)
    rep = lambda b: jnp.repeat(jnp.transpose(b, (0, 3, 1, 2)).reshape(depth, 1, S5_GROUP, S5_NS), nd, axis=1)
    bre = rep(b_re).reshape(r, S5_GROUP, S5_NS)
    bim = rep(b_im).reshape(r, S5_GROUP, S5_NS)
    v = jax.ShapeDtypeStruct((r, S5_NS), F32)
    m = jax.ShapeDtypeStruct((r, S5_GROUP, S5_NS), F32)
    ar, ai, bbr, bbi = pl.pallas_call(_s5_param_rows_kernel, out_shape=(v, v, m, m), name="s5_discretise")(
        are, aim, ldt, bre, bim)
    eye = jnp.eye(S5_GROUPS, dtype=F32)

    def dense_b(bb):
        bb = bb.reshape(r, S5_GROUP, S5_GROUPS, S5_STATE)
        return jnp.einsum("dcgp,gh->dgchp", bb, eye).reshape(r, S5_CH, S5_NS)

    def dense_c(cc):
        return jnp.einsum("lgcp,gh->lhpgc", cc, eye).reshape(depth, S5_NS, S5_CH)

    bd = jnp.concatenate([dense_b(bbr), dense_b(bbi)], axis=-1).astype(BF16).reshape(depth, nd, S5_CH, 2 * S5_NS)
    cd = jnp.concatenate([dense_c(c_re), -dense_c(c_im)], axis=1).astype(BF16)
    abar = jnp.concatenate([ar, ai], axis=-1).reshape(depth, nd, 1, 2 * S5_NS)
    return abar, bd, cd


def _s5_param_rows_kernel(are_ref, aim_ref, ldt_ref, bre_ref, bim_ref, oar_ref, oai_ref, obr_ref, obi_ref):
    lr, li = are_ref[...], aim_ref[...]
    dt = jnp.exp(ldt_ref[...])
    mag = jnp.exp(lr * dt)
    ar, ai = mag * jnp.cos(li * dt), mag * jnp.sin(li * dt)
    oar_ref[...] = ar
    oai_ref[...] = ai
    nr, ni = ar - 1.0, ai
    den = lr * lr + li * li
    fr, fi = (nr * lr + ni * li) / den, (ni * lr - nr * li) / den
    for d in range(fr.shape[0]):
        br, bi = bre_ref[d], bim_ref[d]
        obr_ref[d] = fr[d:d + 1] * br - fi[d:d + 1] * bi
        obi_ref[d] = fr[d:d + 1] * bi + fi[d:d + 1] * br


def _ffn_kernel(x_ref, mod_ref, g_ref, w13_ref, w2_ref, o_ref, h_ref, *, row0, slab):
    j = pl.program_id(1)
    nslab = h_ref.shape[0] // slab
    nr = math.gcd(slab, NORM_ROWS)
    tf = w2_ref.shape[0]

    def partial_ffn(rows):
        gu = jnp.dot(h_ref[rows, :], w13_ref[...], preferred_element_type=F32)
        act = (_silu(gu[:, :tf]) * gu[:, tf:]).astype(BF16)
        return jnp.dot(act, w2_ref[...], preferred_element_type=F32)

    @pl.when(j == 0)
    def _():
        for r in range(nslab):
            for p in range(slab // nr):
                rows = slice(r * slab + p * nr, r * slab + (p + 1) * nr)
                h = _modulate(x_ref[rows, :], g_ref[...], mod_ref[row0 + 1:row0 + 2, :], mod_ref[row0:row0 + 1, :])
                h_ref[rows, :] = h.astype(BF16)
            rows = slice(r * slab, (r + 1) * slab)
            o_ref[rows, :] = partial_ffn(rows)

    @pl.when(j > 0)
    def _():
        for r in range(nslab):
            rows = slice(r * slab, (r + 1) * slab)
            o_ref[rows, :] += partial_ffn(rows)

    @pl.when(j == pl.num_programs(1) - 1)
    def _():
        o_ref[...] = x_ref[...] + (0.5 * mod_ref[row0 + 2:row0 + 3, :]) * o_ref[...]


def _ffn(x, mod, pr, *, layer, which, seq, tm_pref=1024):
    n, d = x.shape
    groups = mod.shape[0]
    tm = _pick(seq if groups > 1 else n, tm_pref)
    tf = FFN_TILE
    nf = D_FF // tf
    per = seq // tm if groups > 1 else 1
    mod_map = (lambda i, j: (i // per, 0, 0)) if groups > 1 else (lambda i, j: (0, 0, 0))
    return pl.pallas_call(
        functools.partial(_ffn_kernel, row0=6 * which, slab=_pick(tm, 512)),
        out_shape=jax.ShapeDtypeStruct((n, d), F32),
        grid=(n // tm, nf),
        in_specs=[pl.BlockSpec((tm, d), lambda i, j: (i, 0)),
                  pl.BlockSpec((None, N_MOD, d), mod_map),
                  pl.BlockSpec((None, None, 1, d), lambda i, j: (layer, 2 * which, 0, 0)),
                  pl.BlockSpec((None, None, d, 2 * tf), lambda i, j: (layer, which, 0, j)),
                  pl.BlockSpec((None, None, tf, d), lambda i, j: (layer, which, j, 0))],
        out_specs=pl.BlockSpec((tm, d), lambda i, j: (i, 0)),
        scratch_shapes=[pltpu.VMEM((tm, d), BF16)],
        compiler_params=_cparams(("parallel", "arbitrary"), FFN_VMEM_LIMIT_BYTES),
        name="ffn",
    )(x, mod, pr["norm_g"], pr["ffn_w13"], pr["ffn_w2"])


def _inproj_kernel(x_ref, mod_ref, g_ref, w_ref, o_ref):
    h = _modulate(x_ref[...], g_ref[...], mod_ref[4:5, :], mod_ref[3:4, :]).astype(BF16)
    o_ref[...] = jnp.dot(h, w_ref[...], preferred_element_type=F32)


def _inproj(x, mod, pr, *, layer, seq, tm_pref=512):
    n, d = x.shape
    groups = mod.shape[0]
    tm = _pick(seq if groups > 1 else n, tm_pref)
    per = seq // tm if groups > 1 else 1
    mod_map = (lambda i: (i // per, 0, 0)) if groups > 1 else (lambda i: (0, 0, 0))
    return pl.pallas_call(
        _inproj_kernel,
        out_shape=jax.ShapeDtypeStruct((n, PROJ_COLS), F32),
        grid=(n // tm,),
        in_specs=[pl.BlockSpec((tm, d), lambda i: (i, 0)),
                  pl.BlockSpec((None, N_MOD, d), mod_map),
                  pl.BlockSpec((None, None, 1, d), lambda i: (layer, 1, 0, 0)),
                  _layer_spec(pr["w_in"], layer, 1)],
        out_specs=pl.BlockSpec((tm, PROJ_COLS), lambda i: (i, 0)),
        compiler_params=_cparams(("parallel",)),
        name="in_proj",
    )(x, mod, pr["norm_g"], pr["w_in"])


def _mla_kv(ckv_n, kr128, wukv_ref, ka_ref, va_ref):
    kv = jnp.dot(ckv_n.astype(BF16), wukv_ref[...], preferred_element_type=F32)
    krb = kr128.astype(BF16)
    hw = MLA_NOPE + MLA_V
    for h in range(MLA_HEADS):
        ka_ref[:, h * 2 * LANE:h * 2 * LANE + MLA_NOPE] = kv[:, h * hw:h * hw + MLA_NOPE].astype(BF16)
        ka_ref[:, h * 2 * LANE + MLA_NOPE:(h + 1) * 2 * LANE] = krb
        va_ref[:, h * MLA_V:(h + 1) * MLA_V] = kv[:, h * hw + MLA_NOPE:(h + 1) * hw].astype(BF16)


def _attn_prep_kernel(*refs, rope):
    if rope:
        (cq_ref, ckv_ref, krmg_ref, gq_ref, gk_ref, gv_ref, ca_ref, sa_ref, cb_ref, sb_ref,
         qng_ref, kvng_ref, wuq_ref, wukv_ref, gqg_ref, gkg_ref,
         qa_ref, ka_ref, va_ref, ckvn_ref, qb_ref, kb_ref, vb_ref, kbn_ref) = refs
    else:
        (cq_ref, ckv_ref, krmg_ref, gq_ref, gk_ref, gv_ref,
         qng_ref, kvng_ref, wuq_ref, wukv_ref, gqg_ref, gkg_ref,
         qa_ref, ka_ref, va_ref, ckvn_ref, qb_ref, kb_ref, vb_ref, kbn_ref) = refs

    cqn = _rms(cq_ref[...], qng_ref[...]).astype(BF16)
    qa = jnp.dot(cqn, wuq_ref[...], preferred_element_type=F32)
    if rope:
        ca = jnp.concatenate([ca_ref[...]] * MLA_HEADS, axis=1)
        sa = jnp.concatenate([sa_ref[...]] * MLA_HEADS, axis=1)
        qa = qa * ca + _swap_halves(qa, MLA_ROPE // 4) * sa
    qa_ref[...] = (qa * ((MLA_NOPE + MLA_ROPE) ** -0.5 * LOG2E)).astype(BF16)

    ckv_n = _rms(ckv_ref[...], kvng_ref[...])
    ckvn_ref[...] = ckv_n
    krmg = krmg_ref[...]
    lane = lax.broadcasted_iota(jnp.int32, krmg.shape, 1)
    if rope:
        krmg = krmg * ca_ref[:, LANE:] + _swap_halves(krmg, MLA_ROPE // 4) * sa_ref[:, LANE:]
    kr128 = jnp.where(lane < MLA_ROPE, krmg, 0.0)
    _mla_kv(ckv_n, kr128, wukv_ref, ka_ref, va_ref)

    hd = GQA_HEAD_DIM
    gq, gk = gq_ref[...], gk_ref[...]
    for h in range(GQA_HEADS):
        q = _rms(gq[:, h * hd:(h + 1) * hd], gqg_ref[...])
        if rope:
            q = q * cb_ref[...] + _swap_halves(q, hd // 4) * sb_ref[...]
        qb_ref[:, h * hd:(h + 1) * hd] = (q * (hd ** -0.5 * LOG2E)).astype(BF16)
    for h in range(GQA_KV_HEADS):
        k = _rms(gk[:, h * hd:(h + 1) * hd], gkg_ref[...])
        kbn_ref[:, h * hd:(h + 1) * hd] = k
        if rope:
            k = k * cb_ref[...] + _swap_halves(k, hd // 4) * sb_ref[...]
        kb_ref[:, h * hd:(h + 1) * hd] = k.astype(BF16)
    vb_ref[...] = gv_ref[...].astype(BF16)


def _attn_prep(proj, tables, pr, *, layer, seq, tm_pref=512):
    n = proj.shape[0]
    tm = _pick(seq, tm_pref)
    per = seq // tm
    rope = tables is not None

    def pblock(name, width):
        idx = _COL[name] // width
        assert idx * width == _COL[name]
        return pl.BlockSpec((tm, width), lambda i: (i, idx))

    in_specs = [pblock("cq", MLA_Q_LORA), pblock("ckv", MLA_KV_LORA),
                pl.BlockSpec((tm, LANE), lambda i: (i, KRMG_BLOCK)),
                pblock("gq", 512), pblock("gk", 256), pblock("gv", 256)]
    args = [proj] * 6
    if rope:
        in_specs += [pl.BlockSpec((tm, t.shape[1]), lambda i: (i % per, 0)) for t in tables]
        args += list(tables)
    ws = [pr["mla_q_norm"], pr["mla_kv_norm"], pr["mla_w_uq"], pr["mla_w_ukv"], pr["gqa_q_norm"], pr["gqa_k_norm"]]
    in_specs += [_layer_spec(w, layer, 1) for w in ws]
    args += ws
    widths = [(4 * 2 * LANE, BF16), (4 * 2 * LANE, BF16), (4 * MLA_V, BF16), (MLA_KV_LORA, F32),
              (GQA_HEADS * GQA_HEAD_DIM, BF16), (GQA_KV_HEADS * GQA_HEAD_DIM, BF16),
              (GQA_KV_HEADS * GQA_HEAD_DIM, BF16), (GQA_KV_HEADS * GQA_HEAD_DIM, F32)]
    return pl.pallas_call(
        functools.partial(_attn_prep_kernel, rope=rope),
        out_shape=tuple(jax.ShapeDtypeStruct((n, w), dt) for w, dt in widths),
        grid=(n // tm,),
        in_specs=in_specs,
        out_specs=tuple(pl.BlockSpec((tm, w), lambda i: (i, 0)) for w, _ in widths),
        compiler_params=_cparams(("parallel",)),
        name="attn_prep",
    )(*args)


def _cache_kv_kernel(ckv_ref, kr_ref, wukv_ref, ka_ref, va_ref):
    _mla_kv(ckv_ref[...], kr_ref[...], wukv_ref, ka_ref, va_ref)


def _cache_kv(ckv, kr128, wukv, *, layer):
    b, _, p, _ = ckv.shape
    return pl.pallas_call(
        _cache_kv_kernel,
        out_shape=(jax.ShapeDtypeStruct((b, p, 4 * 2 * LANE), BF16), jax.ShapeDtypeStruct((b, p, 4 * MLA_V), BF16)),
        grid=(b,),
        in_specs=[pl.BlockSpec((None, None, p, MLA_KV_LORA), lambda i: (i, layer, 0, 0)),
                  pl.BlockSpec((None, None, p, LANE), lambda i: (i, layer, 0, 0)),
                  _layer_spec(wukv, layer, 1)],
        out_specs=(pl.BlockSpec((None, p, 4 * 2 * LANE), lambda i: (i, 0, 0)),
                   pl.BlockSpec((None, p, 4 * MLA_V), lambda i: (i, 0, 0))),
        compiler_params=_cparams(("parallel",)),
        name="mla_cache_kv",
    )(ckv, kr128, wukv)


def _attn_kernel(q_ref, *refs, nseg, hp, g, dk, dv):
    o_ref = refs[-1]
    for j in range(hp):
        kv = j // g
        q = q_ref[:, j * dk:(j + 1) * dk]
        ks = [refs[2 * i][:, kv * dk:(kv + 1) * dk].astype(BF16) for i in range(nseg)]
        vs = [refs[2 * i + 1][:, kv * dv:(kv + 1) * dv].astype(BF16) for i in range(nseg)]
        ss = [lax.dot_general(q, k, (((1,), (1,)), ((), ())), preferred_element_type=F32) for k in ks]
        m = functools.reduce(jnp.maximum, [jnp.max(s, axis=-1, keepdims=True) for s in ss])
        ps = [jnp.exp2(s - m) for s in ss]
        l = functools.reduce(jnp.add, [jnp.sum(p, axis=-1, keepdims=True) for p in ps])
        o = functools.reduce(jnp.add, [jnp.dot(p.astype(BF16), v, preferred_element_type=F32)
                                       for p, v in zip(ps, vs)])
        o_ref[:, j * dv:(j + 1) * dv] = o / l


def _attention(q, segs, *, heads, kv_heads, dk, dv, tq_pref=1024, hp=2):
    b, sq, _ = q.shape
    tq = _pick(sq, tq_pref)
    g = heads // kv_heads
    assert heads % hp == 0 and (hp % g == 0 or g % hp == 0)
    kvp = max(1, hp // g)
    kcol = (lambda hb: hb) if hp >= g else (lambda hb: hb * hp // g)
    in_specs = [pl.BlockSpec((None, tq, hp * dk), lambda bi, hb, i: (bi, i, hb))]
    args = [q]
    for k, v, layer in segs:
        sk = k.shape[-2]
        if layer is None:
            in_specs += [pl.BlockSpec((None, sk, kvp * dk), lambda bi, hb, i: (bi, 0, kcol(hb))),
                         pl.BlockSpec((None, sk, kvp * dv), lambda bi, hb, i: (bi, 0, kcol(hb)))]
        else:
            in_specs += [pl.BlockSpec((None, None, sk, kvp * dk),
                                      lambda bi, hb, i, layer=layer: (bi, layer, 0, kcol(hb))),
                         pl.BlockSpec((None, None, sk, kvp * dv),
                                      lambda bi, hb, i, layer=layer: (bi, layer, 0, kcol(hb)))]
        args += [k, v]
    return pl.pallas_call(
        functools.partial(_attn_kernel, nseg=len(segs), hp=hp, g=min(g, hp), dk=dk, dv=dv),
        out_shape=jax.ShapeDtypeStruct((b, sq, heads * dv), F32),
        grid=(b, heads // hp, sq // tq),
        in_specs=in_specs,
        out_specs=pl.BlockSpec((None, tq, hp * dv), lambda bi, hb, i: (bi, i, hb)),
        compiler_params=_cparams(("parallel", "parallel", "parallel")),
        name="attention",
    )(*args)


def _log_sigmoid(x):
    return jnp.minimum(x, 0.0) - jnp.log1p(jnp.exp(-jnp.abs(x)))


def _mlstm_chunk_operands(q_ref, k_ref, v_ref, gate_ref, bias_ref):
    nh, hd = MLSTM_HEADS, MLSTM_HEAD_DIM
    g = gate_ref[...] + bias_ref[...]
    lf = _log_sigmoid(g)
    lf_hi = lf.astype(BF16)
    rem = lf - lf_hi.astype(F32)
    lf_mid = rem.astype(BF16)
    lf_lo = (rem - lf_mid.astype(F32)).astype(BF16)
    heads = []
    for h in range(nh):
        q = q_ref[:, h * hd:(h + 1) * hd]
        kh = k_ref[:, h * hd:(h + 1) * hd] * (hd ** -0.5)
        kt = kh.T
        qh = q.astype(BF16)
        heads.append(dict(q=q, qh=qh, kh=kh, kt=kt, vh=v_ref[:, h * hd:(h + 1) * hd].astype(BF16),
                          s=jnp.dot(qh, kt.astype(BF16), preferred_element_type=F32)))
    return dict(g=g, gt=g.T, lf=(lf_hi, lf_mid, lf_lo), heads=heads)


def _mlstm_chain(ops, h_ref, c_s, n_s, m_s, *, direction):
    nh, hd = MLSTM_HEADS, MLSTM_HEAD_DIM
    rev = direction == 1
    g, gt = ops["g"], ops["gt"]
    cl = g.shape[0]
    ri = lax.broadcasted_iota(jnp.int32, (cl, cl), 0)
    ci = lax.broadcasted_iota(jnp.int32, (cl, cl), 1)
    mask = (ri <= ci) if rev else (ri >= ci)
    tri = jnp.where(mask, 1.0, 0.0).astype(BF16)
    cum = functools.reduce(jnp.add, [jnp.dot(tri, part, preferred_element_type=F32) for part in ops["lf"]])
    cumt = cum.T
    last = 0 if rev else cl - 1
    li0 = MG_LANE + direction * 2 * nh
    lf0 = li0 + nh

    for h in range(nh):
        hv = ops["heads"][h]
        cum_c = cum[:, lf0 + h:lf0 + h + 1]
        cum_r = cumt[lf0 + h:lf0 + h + 1, :]
        li_r = gt[li0 + h:li0 + h + 1, :]
        li_c = g[:, li0 + h:li0 + h + 1]
        m_prev = m_s[direction, h:h + 1, 0:1]
        dmat = jnp.where(mask, cum_c - cum_r + li_r, -jnp.inf)
        inter = cum_c + m_prev
        m_t = jnp.maximum(inter, jnp.max(dmat, axis=1, keepdims=True))
        sc = hv["s"] * jnp.exp(dmat - m_t)
        inter_w = jnp.exp(inter - m_t)
        cmat = c_s[direction, h]
        nrow = n_s[direction, h:h + 1, :]
        num = (jnp.dot(sc.astype(BF16), hv["vh"], preferred_element_type=F32)
               + inter_w * jnp.dot(hv["qh"], cmat.astype(BF16), preferred_element_type=F32))
        qn = jnp.sum(hv["q"] * nrow, axis=1, keepdims=True)
        den = jnp.sum(sc, axis=1, keepdims=True) + inter_w * qn
        h_ref[:, h * hd:(h + 1) * hd] = num / jnp.maximum(jnp.abs(den), jnp.exp(-m_t))

        tot = cum_c[last:last + 1, :]
        g_r = tot - cum_r + li_r
        g_c = tot - cum_c + li_c
        m_new = jnp.maximum(tot + m_prev, jnp.max(g_r, axis=1, keepdims=True))
        decay = jnp.exp(tot + m_prev - m_new)
        ws_r = jnp.exp(g_r - m_new)
        ws_c = jnp.exp(g_c - m_new)
        c_s[direction, h] = decay * cmat + jnp.dot((hv["kt"] * ws_r).astype(BF16), hv["vh"],
                                                   preferred_element_type=F32)
        n_s[direction, h:h + 1, :] = decay * nrow + jnp.sum(hv["kh"] * ws_c, axis=0, keepdims=True)
        m_s[direction, h:h + 1, :] = jnp.broadcast_to(m_new, (1, hd))


def _mlstm_kernel(*refs, shared, has_init, emit_state):
    nin = 4 if shared else 8
    fwd_in = refs[:4]
    bwd_in = fwd_in if shared else refs[4:8]
    bias_ref = refs[nin]
    pos = nin + 1
    if has_init:
        c0_ref, n0_ref, m0_ref = refs[pos:pos + 3]
        pos += 3
    hf_ref, hb_ref = refs[pos:pos + 2]
    pos += 2
    if emit_state:
        co_ref, no_ref, mo_ref = refs[pos:pos + 3]
        pos += 3
    c_s, n_s, m_s = refs[pos:pos + 3]
    c = pl.program_id(1)

    @pl.when(c == 0)
    def _():
        if has_init:
            c_s[...] = c0_ref[...]
            n_s[...] = n0_ref[...]
            m_s[...] = m0_ref[...]
        else:
            c_s[...] = jnp.zeros_like(c_s)
            n_s[...] = jnp.zeros_like(n_s)
            m_s[...] = jnp.zeros_like(m_s)

    ops_f = _mlstm_chunk_operands(*fwd_in, bias_ref)
    ops_b = ops_f if shared else _mlstm_chunk_operands(*bwd_in, bias_ref)
    _mlstm_chain(ops_f, hf_ref, c_s, n_s, m_s, direction=0)
    _mlstm_chain(ops_b, hb_ref, c_s, n_s, m_s, direction=1)

    if emit_state:
        @pl.when(c == pl.num_programs(1) - 1)
        def _():
            co_ref[...] = c_s[...]
            no_ref[...] = n_s[...]
            mo_ref[...] = m_s[...]


def _mlstm(proj, bias128, init, *, layer, batch, seq, emit_state, chunk_pref=256):
    n = proj.shape[0]
    cl = _pick(seq, chunk_pref)
    nc = seq // cl
    shared = nc == 1
    nh, hd = MLSTM_HEADS, MLSTM_HEAD_DIM
    rows = (lambda b, c: b * nc + c), (lambda b, c: b * nc + (nc - 1 - c))

    def chunk_specs(row):
        qkv = lambda name: pl.BlockSpec((cl, nh * hd), lambda b, c: (row(b, c), _COL[name] // (nh * hd)))
        return [qkv("mq"), qkv("mk"), qkv("mv"), pl.BlockSpec((cl, LANE), lambda b, c: (row(b, c), KRMG_BLOCK))]

    in_specs = chunk_specs(rows[0]) + ([] if shared else chunk_specs(rows[1]))
    in_specs.append(pl.BlockSpec((None, 1, LANE), lambda b, c: (layer, 0, 0)))
    args = [proj] * (len(in_specs) - 1) + [bias128]
    has_init = init is not None
    if has_init:
        c0, n0, m0 = init
        in_specs += [pl.BlockSpec((None, None, 2, nh, hd, hd), lambda b, c: (b, layer, 0, 0, 0, 0)),
                     pl.BlockSpec((None, None, 2, nh, hd), lambda b, c: (b, layer, 0, 0, 0)),
                     pl.BlockSpec((None, None, 2, nh, hd), lambda b, c: (b, layer, 0, 0, 0))]
        args += [c0, n0, m0]
    out_shape = [jax.ShapeDtypeStruct((n, nh * hd), F32)] * 2
    out_specs = [pl.BlockSpec((cl, nh * hd), lambda b, c, row=row: (row(b, c), 0)) for row in rows]
    if emit_state:
        out_shape += [jax.ShapeDtypeStruct((batch, 2, nh, hd, hd), F32), jax.ShapeDtypeStruct((batch, 2, nh, hd), F32),
                      jax.ShapeDtypeStruct((batch, 2, nh, hd), F32)]
        out_specs += [pl.BlockSpec((None, 2, nh, hd, hd), lambda b, c: (b, 0, 0, 0, 0)),
                      pl.BlockSpec((None, 2, nh, hd), lambda b, c: (b, 0, 0, 0)),
                      pl.BlockSpec((None, 2, nh, hd), lambda b, c: (b, 0, 0, 0))]
    return pl.pallas_call(
        functools.partial(_mlstm_kernel, shared=shared, has_init=has_init, emit_state=emit_state),
        out_shape=tuple(out_shape),
        grid=(batch, nc),
        in_specs=in_specs,
        out_specs=tuple(out_specs),
        scratch_shapes=[pltpu.VMEM((2, nh, hd, hd), F32), pltpu.VMEM((2, nh, hd), F32), pltpu.VMEM((2, nh, hd), F32)],
        compiler_params=_cparams(("parallel", "arbitrary")),
        name="mlstm",
    )(*args)


_S5_STATE_TILES = 2 * S5_NS // MXU_TILE
_S5_GROUPS_PER_TILE = MXU_TILE // S5_STATE


def _s5_channel_tile(state_tile):
    first_group = (state_tile % (S5_NS // MXU_TILE)) * _S5_GROUPS_PER_TILE
    return first_group * S5_GROUP // MXU_TILE


def _to_time_major_kernel(u_ref, p_ref, o_ref):
    sg, tsteps, ch = u_ref.shape
    u = u_ref[...].reshape(sg * tsteps, ch).astype(BF16)
    o_ref[...] = jnp.dot(p_ref[...], u, preferred_element_type=F32).astype(BF16)


def _from_time_major_kernel(yf_ref, yb_ref, pt_ref, o_ref):
    sg, tsteps, ch = o_ref.shape
    y = yf_ref[...] + yb_ref[...]
    hi = y.astype(BF16)
    lo = (y - hi.astype(F32)).astype(BF16)
    pt = pt_ref[...]
    out = jnp.dot(pt, hi, preferred_element_type=F32) + jnp.dot(pt, lo, preferred_element_type=F32)
    o_ref[...] = out.reshape(sg, tsteps, ch)


def _s5_kernel(*refs, has_init):
    if has_init:
        u_ref, bd_ref, a_ref, cd_ref, x0_ref, y_ref, xf_ref, bu_s, x_s = refs
    else:
        u_ref, bd_ref, a_ref, cd_ref, y_ref, xf_ref, bu_s, x_s = refs
    d = pl.program_id(0)
    c = pl.program_id(2)
    ns = S5_NS
    sg = x_s.shape[0]
    tsteps = u_ref.shape[0] // sg
    mt = MXU_TILE

    @pl.when(c == 0)
    def _():
        if has_init:
            x_s[...] = x0_ref[...]
        else:
            x_s[...] = jnp.zeros_like(x_s)

    u = u_ref[...]
    for st in range(_S5_STATE_TILES):
        ct = _s5_channel_tile(st)
        bu_s[:, st * mt:(st + 1) * mt] = jnp.dot(u[:, ct * mt:(ct + 1) * mt],
                                                 bd_ref[ct * mt:(ct + 1) * mt, st * mt:(st + 1) * mt],
                                                 preferred_element_type=F32)
    a_re = jnp.broadcast_to(a_ref[:, :ns], (sg, ns))
    a_im = jnp.broadcast_to(a_ref[:, ns:], (sg, ns))

    def body(t, carry):
        xr, xi = carry
        tt = t + d * (tsteps - 1 - 2 * t)
        r0 = pl.multiple_of(tt * sg, sg)
        br = bu_s[pl.ds(r0, sg), :ns]
        bi = bu_s[pl.ds(r0, sg), ns:]
        nr = a_re * xr - a_im * xi + br
        ni = a_re * xi + a_im * xr + bi
        bu_s[pl.ds(r0, sg), :ns] = nr
        bu_s[pl.ds(r0, sg), ns:] = ni
        return nr, ni

    xr, xi = lax.fori_loop(0, tsteps, body, (x_s[:, :ns], x_s[:, ns:]))
    x_s[:, :ns] = xr
    x_s[:, ns:] = xi

    for ct in range(S5_CH // mt):
        acc = None
        for st in range(_S5_STATE_TILES):
            if _s5_channel_tile(st) != ct:
                continue
            t = jnp.dot(bu_s[:, st * mt:(st + 1) * mt].astype(BF16),
                        cd_ref[st * mt:(st + 1) * mt, ct * mt:(ct + 1) * mt], preferred_element_type=F32)
            acc = t if acc is None else acc + t
        y_ref[:, ct * mt:(ct + 1) * mt] = acc

    @pl.when(c == pl.num_programs(2) - 1)
    def _():
        xf_ref[...] = x_s[...]


def _s5(proj3, pr, x0, *, layer, rows_pref=512, scan_rows_pref=1024):
    batch, seq, _ = proj3.shape
    sg = SUBLANE if batch % SUBLANE == 0 else batch
    ng = batch // sg
    tsteps = _pick(seq, max(SUBLANE, rows_pref // sg))
    rows = sg * tsteps
    nc = seq // tsteps
    r = jnp.arange(rows)
    perm = (jnp.arange(rows)[None, :] == ((r % sg) * tsteps + r // sg)[:, None]).astype(BF16)
    const2 = lambda a: pl.BlockSpec(a.shape, lambda g, c: (0,) * a.ndim, pipeline_mode=pl.Buffered(1))

    u_tm = pl.pallas_call(
        _to_time_major_kernel,
        out_shape=jax.ShapeDtypeStruct((ng, seq * sg, S5_CH), BF16),
        grid=(ng, nc),
        in_specs=[pl.BlockSpec((sg, tsteps, S5_CH), lambda g, c: (g, c, _COL["su"] // S5_CH)), const2(perm)],
        out_specs=pl.BlockSpec((None, rows, S5_CH), lambda g, c: (g, c, 0)),
        compiler_params=_cparams(("parallel", "parallel")),
        name="s5_to_time_major",
    )(proj3, perm)

    ts_scan = _pick(seq, max(SUBLANE, scan_rows_pref // sg))
    rows_scan = sg * ts_scan
    nc_scan = seq // ts_scan
    cpos = lambda d, c: c + d * (nc_scan - 1 - 2 * c)
    has_init = x0 is not None
    in_specs = [pl.BlockSpec((None, rows_scan, S5_CH), lambda d, g, c: (g, cpos(d, c), 0)),
                pl.BlockSpec((None, None, S5_CH, 2 * S5_NS), lambda d, g, c: (layer, d, 0, 0)),
                pl.BlockSpec((None, None, 1, 2 * S5_NS), lambda d, g, c: (layer, d, 0, 0)),
                _layer_spec(pr["s5_cd"], layer, 3)]
    args = [u_tm, pr["s5_bd"], pr["s5_abar"], pr["s5_cd"]]
    if has_init:
        in_specs.append(pl.BlockSpec((None, sg, 2 * S5_NS), lambda d, g, c: (d, g, 0)))
        args.append(x0)
    y_tm, xfin = pl.pallas_call(
        functools.partial(_s5_kernel, has_init=has_init),
        out_shape=(jax.ShapeDtypeStruct((2, ng, seq * sg, S5_CH), F32),
                   jax.ShapeDtypeStruct((2, batch, 2 * S5_NS), F32)),
        grid=(2, ng, nc_scan),
        in_specs=in_specs,
        out_specs=(pl.BlockSpec((None, None, rows_scan, S5_CH), lambda d, g, c: (d, g, cpos(d, c), 0)),
                   pl.BlockSpec((None, sg, 2 * S5_NS), lambda d, g, c: (d, g, 0))),
        scratch_shapes=[pltpu.VMEM((rows_scan, 2 * S5_NS), F32), pltpu.VMEM((sg, 2 * S5_NS), F32)],
        compiler_params=_cparams(("parallel", "parallel", "arbitrary")),
        name="s5_scan",
    )(*args)

    y = pl.pallas_call(
        _from_time_major_kernel,
        out_shape=jax.ShapeDtypeStruct((batch, seq, S5_CH), F32),
        grid=(ng, nc),
        in_specs=[pl.BlockSpec((None, None, rows, S5_CH), lambda g, c: (0, g, c, 0)),
                  pl.BlockSpec((None, None, rows, S5_CH), lambda g, c: (1, g, c, 0)),
                  const2(perm)],
        out_specs=pl.BlockSpec((sg, tsteps, S5_CH), lambda g, c: (g, c, 0)),
        compiler_params=_cparams(("parallel", "parallel")),
        name="s5_from_time_major",
    )(y_tm, y_tm, perm.T)
    return y, xfin


def _gelu_tanh(x):
    return 0.5 * x * (1.0 + jnp.tanh(math.sqrt(2.0 / math.pi) * (x + 0.044715 * (x * x * x))))


def _outproj_kernel(x_ref, mod_ref, oa_ref, ob_ref, hf_ref, hb_ref, mo_ref, y_ref, su_ref,
                    d_ref, wglu_ref, on_ref, wout_ref, o_ref):
    oc = jax.nn.sigmoid(mo_ref[...]) * (hf_ref[...] + hb_ref[...])
    y = _gelu_tanh(y_ref[...] + d_ref[...] * su_ref[...])
    od = y * jax.nn.sigmoid(jnp.dot(y.astype(BF16), wglu_ref[...], preferred_element_type=F32))
    acc = None
    for gi, part in enumerate((oa_ref[...], ob_ref[...], oc, od)):
        nrm = _rms(part, on_ref[gi:gi + 1, :]).astype(BF16)
        t = jnp.dot(nrm, wout_ref[gi * GROUP_WIDTH:(gi + 1) * GROUP_WIDTH, :], preferred_element_type=F32)
        acc = t if acc is None else acc + t
    o_ref[...] = x_ref[...] + mod_ref[5:6, :] * acc


def _outproj(x, mod, oa, ob, hf, hb, proj, y, pr, *, layer, seq, tm_pref=512):
    n, d = x.shape
    groups = mod.shape[0]
    tm = _pick(seq if groups > 1 else n, tm_pref)
    per = seq // tm if groups > 1 else 1
    mod_map = (lambda i: (i // per, 0, 0)) if groups > 1 else (lambda i: (0, 0, 0))
    gw = GROUP_WIDTH
    rowblk = pl.BlockSpec((tm, gw), lambda i: (i, 0))
    ws = [pr["s5_d"], pr["s5_w_glu"], pr["out_norm"], pr["w_out"]]
    return pl.pallas_call(
        _outproj_kernel,
        out_shape=jax.ShapeDtypeStruct((n, d), F32),
        grid=(n // tm,),
        in_specs=[pl.BlockSpec((tm, d), lambda i: (i, 0)),
                  pl.BlockSpec((None, N_MOD, d), mod_map),
                  rowblk, rowblk, rowblk, rowblk,
                  pl.BlockSpec((tm, gw), lambda i: (i, _COL["mo"] // gw)),
                  rowblk,
                  pl.BlockSpec((tm, gw), lambda i: (i, _COL["su"] // gw))] + [_layer_spec(w, layer, 1) for w in ws],
        out_specs=pl.BlockSpec((tm, d), lambda i: (i, 0)),
        compiler_params=_cparams(("parallel",)),
        name="merge_out_proj",
    )(x, mod, oa, ob, hf, hb, proj, y, proj, *ws)


def _final_norm_kernel(x_ref, g_ref, o_ref):
    o_ref[...] = _rms(x_ref[...], g_ref[...])


def _final_norm(x, g, tm_pref=1024):
    n, d = x.shape
    tm = _pick(n, tm_pref)
    return pl.pallas_call(
        _final_norm_kernel,
        out_shape=jax.ShapeDtypeStruct((n, d), F32),
        grid=(n // tm,),
        in_specs=[pl.BlockSpec((tm, d), lambda i: (i, 0)), pl.BlockSpec((1, d), lambda i: (0, 0))],
        out_specs=pl.BlockSpec((tm, d), lambda i: (i, 0)),
        compiler_params=_cparams(("parallel",)),
        name="final_norm",
    )(x, g)


def _trunk_layer(x, mod, pr, *, layer, batch, seq, tables, cache):
    latent = cache is not None
    x = _ffn(x, mod, pr, layer=layer, which=0, seq=seq)
    proj = _inproj(x, mod, pr, layer=layer, seq=seq)

    qa, ka, va, ckvn, qb, kb, vb, kbn = _attn_prep(proj, tables, pr, layer=layer, seq=seq)
    r3 = lambda a: a.reshape(batch, seq, a.shape[-1])
    segs_a = [(r3(ka), r3(va), None)]
    segs_b = [(r3(kb), r3(vb), None)]
    if latent:
        segs_a.insert(0, cache["mla_kv"] + (None,))
        segs_b.insert(0, cache["gqa_kv"] + (layer,))
    oa = _attention(r3(qa), segs_a, heads=MLA_HEADS, kv_heads=MLA_HEADS, dk=2 * LANE, dv=MLA_V)
    ob = _attention(r3(qb), segs_b, heads=GQA_HEADS, kv_heads=GQA_KV_HEADS, dk=GQA_HEAD_DIM, dv=GQA_HEAD_DIM)
    oa = oa.reshape(batch * seq, -1)
    ob = ob.reshape(batch * seq, -1)

    m_init = cache["mlstm"] if latent else None
    mres = _mlstm(proj, pr["mlstm_bias"], m_init, layer=layer, batch=batch, seq=seq, emit_state=not latent)

    y, xfin = _s5(proj.reshape(batch, seq, PROJ_COLS), pr, cache["s5"] if latent else None, layer=layer)

    x = _outproj(x, mod, oa, ob, mres[0], mres[1], proj, y.reshape(batch * seq, S5_CH), pr,
                 layer=layer, seq=seq)
    x = _ffn(x, mod, pr, layer=layer, which=1, seq=seq)

    new_ctx = None
    if not latent:
        kr = proj[:, _COL["kr"]:_COL["kr"] + MLA_ROPE]
        gv = proj[:, _COL["gv"]:_COL["gv"] + GQA_KV_HEADS * GQA_HEAD_DIM]
        xs = xfin.reshape(2, batch, 2, S5_GROUPS, S5_STATE).transpose(1, 0, 2, 3, 4)
        new_ctx = (ckvn.reshape(batch, seq, MLA_KV_LORA),
                   kr.reshape(batch, seq, MLA_ROPE),
                   kbn.reshape(batch, seq, GQA_KV_HEADS, GQA_HEAD_DIM),
                   gv.reshape(batch, seq, GQA_KV_HEADS, GQA_HEAD_DIM),
                   mres[2], mres[3], mres[4][..., 0],
                   xs[:, :, 0], xs[:, :, 1])
    return x, new_ctx


def _permute_w_in(w_in):
    parts = [w_in[..., _ORIG[n][0]:_ORIG[n][0] + _ORIG[n][1]] for n in _ORDER]
    pad = PROJ_COLS - sum(p.shape[-1] for p in parts)
    parts.append(jnp.zeros(w_in.shape[:-1] + (pad,), w_in.dtype))
    return jnp.concatenate(parts, axis=-1).astype(BF16)


def _interleave_w13(w13):
    depth, two, d, _ = w13.shape
    w = w13.astype(BF16).reshape(depth, two, d, 2, D_FF // FFN_TILE, FFN_TILE)
    return jnp.swapaxes(w, 3, 4).reshape(depth, two, d, 2 * D_FF)


def _permute_w_uq(w_uq):
    depth, k, _ = w_uq.shape
    w = w_uq.reshape(depth, k, MLA_HEADS, MLA_NOPE + MLA_ROPE)
    w = jnp.pad(w, ((0, 0), (0, 0), (0, 0), (0, 2 * LANE - MLA_NOPE - MLA_ROPE)))
    return w.reshape(depth, k, MLA_HEADS * 2 * LANE).astype(BF16)


def kernel(x_prompt, x_sample, cache_mla_ckv, cache_mla_krope, cache_gqa_k, cache_gqa_v, state_mlstm_c, state_mlstm_n, state_mlstm_m, state_s5_re, state_s5_im, c, c_ctx, ada_w, ada_b, norm_g, ffn_w13, ffn_w2, w_in, mla_q_norm, mla_kv_norm, mla_w_uq, mla_w_ukv, gqa_q_norm, gqa_k_norm, mlstm_gate_b, s5_a_re, s5_a_im, s5_log_dt, s5_b_re, s5_b_im, s5_c_re, s5_c_im, s5_d, s5_w_glu, out_norm, w_out, final_norm):
    bc, sc, d = x_prompt.shape
    bl, sl, _ = x_sample.shape
    depth = ada_w.shape[0]
    past = cache_mla_ckv.shape[2]

    rows = ((1 + bl + SUBLANE - 1) // SUBLANE) * SUBLANE
    cvecs = jnp.concatenate([c_ctx[None, :], c, jnp.zeros((rows - 1 - bl, d), F32)], axis=0)
    mod_all = _modulation(cvecs, ada_w, ada_b).reshape(depth, rows, N_MOD, d)

    tables = _rope_tables(sl)

    abar, bd, cd = _s5_params(s5_a_re, s5_a_im, s5_log_dt, s5_b_re, s5_b_im, s5_c_re, s5_c_im)
    nmg = MLSTM_HEADS * 4
    row = lambda a: a.reshape(depth, 1, a.shape[-1])
    pr = {"norm_g": norm_g.reshape(depth, 3, 1, d),
          "ffn_w13": _interleave_w13(ffn_w13), "ffn_w2": ffn_w2.astype(BF16),
          "w_in": _permute_w_in(w_in),
          "mla_q_norm": row(mla_q_norm), "mla_kv_norm": row(mla_kv_norm),
          "mla_w_uq": _permute_w_uq(mla_w_uq), "mla_w_ukv": mla_w_ukv.astype(BF16),
          "gqa_q_norm": row(gqa_q_norm), "gqa_k_norm": row(gqa_k_norm),
          "mlstm_bias": jnp.pad(mlstm_gate_b.reshape(depth, 1, nmg), ((0, 0), (0, 0), (MG_LANE, LANE - MG_LANE - nmg))),
          "s5_abar": abar, "s5_bd": bd, "s5_cd": cd,
          "s5_d": row(s5_d), "s5_w_glu": s5_w_glu.astype(BF16),
          "out_norm": out_norm.reshape(depth, 4, GROUP_WIDTH), "w_out": w_out.astype(BF16)}

    gkv = GQA_KV_HEADS * GQA_HEAD_DIM
    cache_k = cache_gqa_k.reshape(bl, depth, past, gkv)
    cache_v = cache_gqa_v.reshape(bl, depth, past, gkv)
    m0 = jnp.broadcast_to(state_mlstm_m[..., None], state_mlstm_n.shape)
    kr_pad = jnp.pad(cache_mla_krope, ((0, 0), (0, 0), (0, 0), (0, LANE - MLA_ROPE)))

    x_ctx = x_prompt.reshape(bc * sc, d)
    x_lat = x_sample.reshape(bl * sl, d)
    per_layer = []
    for l in range(depth):
        x_ctx, ctx_l = _trunk_layer(x_ctx, mod_all[l, 0:1], pr, layer=l, batch=bc, seq=sc, tables=None, cache=None)
        per_layer.append(ctx_l)

        x0 = jnp.concatenate([state_s5_re[:, l].reshape(bl, 2, S5_NS), state_s5_im[:, l].reshape(bl, 2, S5_NS)],
                             axis=-1).transpose(1, 0, 2)
        cache = {"mla_kv": tuple(_cache_kv(cache_mla_ckv, kr_pad, pr["mla_w_ukv"], layer=l)),
                 "gqa_kv": (cache_k, cache_v),
                 "mlstm": (state_mlstm_c, state_mlstm_n, m0),
                 "s5": x0}
        x_lat, _ = _trunk_layer(x_lat, mod_all[l, 1:1 + bl], pr, layer=l, batch=bl, seq=sl, tables=tables,
                                cache=cache)

    new_ctx = [jnp.stack([t[i] for t in per_layer], axis=1) for i in range(9)]
    y_prompt = _final_norm(x_ctx, final_norm[None, :]).reshape(bc, sc, d)
    y_sample = _final_norm(x_lat, final_norm[None, :]).reshape(bl, sl, d)
    return (y_prompt, y_sample, *new_ctx)
```

```python
import functools
import math

import jax
import jax.numpy as jnp
from jax import lax
from jax.experimental import pallas as pl
from jax.experimental.pallas import tpu as pltpu

F32 = jnp.float32
BF16 = jnp.bfloat16

EPS = 1e-6
ROPE_THETA = 10000.0
GRID_W = 64
N_MOD = 9
LOG2E = math.log2(math.e)

D_FF = 5632
MLA_HEADS, MLA_NOPE, MLA_ROPE, MLA_V = 4, 128, 64, 128
MLA_Q_LORA, MLA_KV_LORA = 384, 256
GQA_HEADS, GQA_KV_HEADS, GQA_HEAD_DIM = 4, 2, 128
MLSTM_HEADS, MLSTM_HEAD_DIM = 4, 128
S5_GROUPS, S5_GROUP, S5_STATE = 32, 16, 64
S5_CH = S5_GROUPS * S5_GROUP
S5_NS = S5_GROUPS * S5_STATE
GROUP_WIDTH = 512

LANE = 128
SUBLANE = 8
MXU_TILE = 256
NORM_ROWS = 128
FFN_TILE = 512
VMEM_BYTES = 64 * 1024 * 1024
VMEM_LIMIT_BYTES = VMEM_BYTES - 8 * 1024 * 1024
FFN_VMEM_LIMIT_BYTES = VMEM_BYTES - 4 * 1024 * 1024

_ORIG = dict(cq=(0, 384), ckv=(384, 256), kr=(640, 64), gq=(704, 512), gk=(1216, 256), gv=(1472, 256),
             mq=(1728, 512), mk=(2240, 512), mv=(2752, 512), mo=(3264, 512), mg=(3776, 16), su=(3792, 512))
_ORDER = ("mq", "mk", "mv", "mo", "gq", "su", "gk", "gv", "ckv", "cq", "kr", "mg")
_COL = {}
_off = 0
for _n in _ORDER:
    _COL[_n] = _off
    _off += _ORIG[_n][1]
PROJ_COLS = ((_off + LANE - 1) // LANE) * LANE
KRMG_BLOCK = _COL["kr"] // LANE
MG_LANE = _COL["mg"] - _COL["kr"]


def _cparams(sem, vmem_limit=VMEM_LIMIT_BYTES):
    return pltpu.CompilerParams(dimension_semantics=sem, vmem_limit_bytes=vmem_limit)


def _pick(n, pref):
    for t in range(min(n, pref), 0, -1):
        if n % t == 0 and (t % SUBLANE == 0 or t == n):
            return t
    return n


def _layer_spec(arr, layer, nargs):
    zeros = (0,) * (arr.ndim - 1)
    imap = {1: lambda i: (layer,) + zeros, 2: lambda i, j: (layer,) + zeros,
            3: lambda i, j, k: (layer,) + zeros}[nargs]
    return pl.BlockSpec((None,) + arr.shape[1:], imap, pipeline_mode=pl.Buffered(1))


def _rms(x, g):
    return x * lax.rsqrt(jnp.mean(x * x, axis=-1, keepdims=True) + EPS) * g


def _modulate(x, g, scale, shift):
    return x * lax.rsqrt(jnp.mean(x * x, axis=-1, keepdims=True) + EPS) * (g * (1.0 + scale)) + shift


def _swap_halves(x, half):
    w = x.shape[-1]
    lane = lax.broadcasted_iota(jnp.int32, x.shape, x.ndim - 1)
    first = (lane & (2 * half - 1)) < half
    return jnp.where(first, pltpu.roll(x, w - half, axis=x.ndim - 1), pltpu.roll(x, half, axis=x.ndim - 1))


def _silu(x):
    return x * jax.nn.sigmoid(x)


def _mod_kernel(c_ref, w_ref, b_ref, o_ref):
    a = _silu(c_ref[...]).astype(BF16)
    o_ref[...] = jnp.dot(a, w_ref[...].astype(BF16), preferred_element_type=F32) + b_ref[...]


def _modulation(cvecs, ada_w, ada_b):
    depth, d, nm = ada_w.shape
    r = cvecs.shape[0]
    tn = _pick(nm, 1024)
    return pl.pallas_call(
        _mod_kernel,
        out_shape=jax.ShapeDtypeStruct((depth, r, nm), F32),
        grid=(depth, nm // tn),
        in_specs=[pl.BlockSpec((r, d), lambda l, j: (0, 0)),
                  pl.BlockSpec((None, d, tn), lambda l, j: (l, 0, j)),
                  pl.BlockSpec((None, 1, tn), lambda l, j: (l, 0, j))],
        out_specs=pl.BlockSpec((None, r, tn), lambda l, j: (l, 0, j)),
        compiler_params=_cparams(("parallel", "parallel")),
        name="adaln_mod",
    )(cvecs, ada_w, ada_b.reshape(depth, 1, nm))


def _rope_kernel(ca_ref, sa_ref, cb_ref, sb_ref, *, log2w):
    s = ca_ref.shape[0]
    t = lax.broadcasted_iota(jnp.int32, (s, 1), 0)
    row = lax.shift_right_logical(t, log2w).astype(F32)
    col = (t & ((1 << log2w) - 1)).astype(F32)

    def tables(width, off, rd):
        lane = lax.broadcasted_iota(jnp.int32, (1, width), 1)
        r = lane - off
        inr = (r >= 0) & (r < rd)
        half, quarter = rd // 2, rd // 4
        is_col = r >= half
        rr = jnp.where(is_col, r - half, r)
        second = rr >= quarter
        j = jnp.where(second, rr - quarter, rr).astype(F32)
        inv = jnp.exp(j * (-2.0 / half * math.log(ROPE_THETA)))
        ang = jnp.where(is_col, col, row) * inv
        sign = jnp.where(second, 1.0, -1.0)
        return jnp.where(inr, jnp.cos(ang), 1.0), jnp.where(inr, sign * jnp.sin(ang), 0.0)

    ca, sa = tables(2 * LANE, MLA_NOPE, MLA_ROPE)
    cb, sb = tables(GQA_HEAD_DIM, 0, GQA_HEAD_DIM)
    ca_ref[...] = ca
    sa_ref[...] = sa
    cb_ref[...] = cb
    sb_ref[...] = sb


def _rope_tables(s):
    log2w = GRID_W.bit_length() - 1
    assert 1 << log2w == GRID_W
    shp = lambda w: jax.ShapeDtypeStruct((s, w), F32)
    return pl.pallas_call(
        functools.partial(_rope_kernel, log2w=log2w),
        out_shape=(shp(2 * LANE), shp(2 * LANE), shp(GQA_HEAD_DIM), shp(GQA_HEAD_DIM)),
        name="rope_tables",
    )()


def _s5_params(a_re, a_im, log_dt, b_re, b_im, c_re, c_im):
    depth, nd = a_re.shape[:2]
    r = depth * nd
    are = a_re.reshape(r, S5_NS)
    aim = a_im.reshape(r, S5_NS)
    ldt = jnp.broadcast_to(log_dt[..., None], (depth, nd, S5_GROUPS, S5_STATE)).reshape(r, S5_NS)
    rep = lambda b: jnp.repeat(jnp.transpose(b, (0, 3, 1, 2)).reshape(depth, 1, S5_GROUP, S5_NS), nd, axis=1)
    bre = rep(b_re).reshape(r, S5_GROUP, S5_NS)
    bim = rep(b_im).reshape(r, S5_GROUP, S5_NS)
    v = jax.ShapeDtypeStruct((r, S5_NS), F32)
    m = jax.ShapeDtypeStruct((r, S5_GROUP, S5_NS), F32)
    ar, ai, bbr, bbi = pl.pallas_call(_s5_param_rows_kernel, out_shape=(v, v, m, m), name="s5_discretise")(
        are, aim, ldt, bre, bim)
    eye = jnp.eye(S5_GROUPS, dtype=F32)

    def dense_b(bb):
        bb = bb.reshape(r, S5_GROUP, S5_GROUPS, S5_STATE)
        return jnp.einsum("dcgp,gh->dgchp", bb, eye).reshape(r, S5_CH, S5_NS)

    def dense_c(cc):
        return jnp.einsum("lgcp,gh->lhpgc", cc, eye).reshape(depth, S5_NS, S5_CH)

    bd = jnp.concatenate([dense_b(bbr), dense_b(bbi)], axis=-1).astype(BF16).reshape(depth, nd, S5_CH, 2 * S5_NS)
    cd = jnp.concatenate([dense_c(c_re), -dense_c(c_im)], axis=1).astype(BF16)
    abar = jnp.concatenate([ar, ai], axis=-1).reshape(depth, nd, 1, 2 * S5_NS)
    return abar, bd, cd


def _s5_param_rows_kernel(are_ref, aim_ref, ldt_ref, bre_ref, bim_ref, oar_ref, oai_ref, obr_ref, obi_ref):
    lr, li = are_ref[...], aim_ref[...]
    dt = jnp.exp(ldt_ref[...])
    mag = jnp.exp(lr * dt)
    ar, ai = mag * jnp.cos(li * dt), mag * jnp.sin(li * dt)
    oar_ref[...] = ar
    oai_ref[...] = ai
    nr, ni = ar - 1.0, ai
    den = lr * lr + li * li
    fr, fi = (nr * lr + ni * li) / den, (ni * lr - nr * li) / den
    for d in range(fr.shape[0]):
        br, bi = bre_ref[d], bim_ref[d]
        obr_ref[d] = fr[d:d + 1] * br - fi[d:d + 1] * bi
        obi_ref[d] = fr[d:d + 1] * bi + fi[d:d + 1] * br


def _ffn_kernel(x_ref, mod_ref, g_ref, w1_ref, w3_ref, w2_ref, o_ref, h_ref, *, row0, slab):
    j = pl.program_id(1)
    nslab = h_ref.shape[0] // slab
    nr = math.gcd(slab, NORM_ROWS)

    def partial_ffn(rows):
        h = h_ref[rows, :]
        gate = jnp.dot(h, w1_ref[...], preferred_element_type=F32)
        up = jnp.dot(h, w3_ref[...], preferred_element_type=F32)
        act = (_silu(gate) * up).astype(BF16)
        return jnp.dot(act, w2_ref[...], preferred_element_type=F32)

    @pl.when(j == 0)
    def _():
        for r in range(nslab):
            for p in range(slab // nr):
                rows = slice(r * slab + p * nr, r * slab + (p + 1) * nr)
                h = _modulate(x_ref[rows, :], g_ref[...], mod_ref[row0 + 1:row0 + 2, :], mod_ref[row0:row0 + 1, :])
                h_ref[rows, :] = h.astype(BF16)
            rows = slice(r * slab, (r + 1) * slab)
            o_ref[rows, :] = partial_ffn(rows)

    @pl.when(j > 0)
    def _():
        for r in range(nslab):
            rows = slice(r * slab, (r + 1) * slab)
            o_ref[rows, :] += partial_ffn(rows)

    @pl.when(j == pl.num_programs(1) - 1)
    def _():
        o_ref[...] = x_ref[...] + (0.5 * mod_ref[row0 + 2:row0 + 3, :]) * o_ref[...]


def _ffn(x, mod, pr, *, layer, which, seq, tm_pref=1024):
    n, d = x.shape
    groups = mod.shape[0]
    tm = _pick(seq if groups > 1 else n, tm_pref)
    tf = FFN_TILE
    nf = D_FF // tf
    per = seq // tm if groups > 1 else 1
    mod_map = (lambda i, j: (i // per, 0, 0)) if groups > 1 else (lambda i, j: (0, 0, 0))
    return pl.pallas_call(
        functools.partial(_ffn_kernel, row0=6 * which, slab=_pick(tm, 512)),
        out_shape=jax.ShapeDtypeStruct((n, d), F32),
        grid=(n // tm, nf),
        in_specs=[pl.BlockSpec((tm, d), lambda i, j: (i, 0)),
                  pl.BlockSpec((None, N_MOD, d), mod_map),
                  pl.BlockSpec((None, None, 1, d), lambda i, j: (layer, 2 * which, 0, 0)),
                  pl.BlockSpec((None, None, d, tf), lambda i, j: (layer, which, 0, j)),
                  pl.BlockSpec((None, None, d, tf), lambda i, j: (layer, which, 0, nf + j)),
                  pl.BlockSpec((None, None, tf, d), lambda i, j: (layer, which, j, 0))],
        out_specs=pl.BlockSpec((tm, d), lambda i, j: (i, 0)),
        scratch_shapes=[pltpu.VMEM((tm, d), BF16)],
        compiler_params=_cparams(("parallel", "arbitrary"), FFN_VMEM_LIMIT_BYTES),
        name="ffn",
    )(x, mod, pr["norm_g"], pr["ffn_w13"], pr["ffn_w13"], pr["ffn_w2"])


def _inproj_kernel(x_ref, mod_ref, g_ref, w_ref, o_ref):
    h = _modulate(x_ref[...], g_ref[...], mod_ref[4:5, :], mod_ref[3:4, :]).astype(BF16)
    o_ref[...] = jnp.dot(h, w_ref[...], preferred_element_type=F32)


def _inproj(x, mod, pr, *, layer, seq, tm_pref=512):
    n, d = x.shape
    groups = mod.shape[0]
    tm = _pick(seq if groups > 1 else n, tm_pref)
    per = seq // tm if groups > 1 else 1
    mod_map = (lambda i: (i // per, 0, 0)) if groups > 1 else (lambda i: (0, 0, 0))
    return pl.pallas_call(
        _inproj_kernel,
        out_shape=jax.ShapeDtypeStruct((n, PROJ_COLS), F32),
        grid=(n // tm,),
        in_specs=[pl.BlockSpec((tm, d), lambda i: (i, 0)),
                  pl.BlockSpec((None, N_MOD, d), mod_map),
                  pl.BlockSpec((None, None, 1, d), lambda i: (layer, 1, 0, 0)),
                  _layer_spec(pr["w_in"], layer, 1)],
        out_specs=pl.BlockSpec((tm, PROJ_COLS), lambda i: (i, 0)),
        compiler_params=_cparams(("parallel",)),
        name="in_proj",
    )(x, mod, pr["norm_g"], pr["w_in"])


def _mla_kv(ckv_n, kr128, wukv_ref, ka_ref, va_ref):
    kv = jnp.dot(ckv_n.astype(BF16), wukv_ref[...], preferred_element_type=F32)
    krb = kr128.astype(BF16)
    hw = MLA_NOPE + MLA_V
    for h in range(MLA_HEADS):
        ka_ref[:, h * 2 * LANE:h * 2 * LANE + MLA_NOPE] = kv[:, h * hw:h * hw + MLA_NOPE].astype(BF16)
        ka_ref[:, h * 2 * LANE + MLA_NOPE:(h + 1) * 2 * LANE] = krb
        va_ref[:, h * MLA_V:(h + 1) * MLA_V] = kv[:, h * hw + MLA_NOPE:(h + 1) * hw].astype(BF16)


def _attn_prep_kernel(*refs, rope):
    if rope:
        (cq_ref, ckv_ref, krmg_ref, gq_ref, gk_ref, gv_ref, ca_ref, sa_ref, cb_ref, sb_ref,
         qng_ref, kvng_ref, wuq_ref, wukv_ref, gqg_ref, gkg_ref,
         qa_ref, ka_ref, va_ref, ckvn_ref, qb_ref, kb_ref, vb_ref, kbn_ref) = refs
    else:
        (cq_ref, ckv_ref, krmg_ref, gq_ref, gk_ref, gv_ref,
         qng_ref, kvng_ref, wuq_ref, wukv_ref, gqg_ref, gkg_ref,
         qa_ref, ka_ref, va_ref, ckvn_ref, qb_ref, kb_ref, vb_ref, kbn_ref) = refs

    cqn = _rms(cq_ref[...], qng_ref[...]).astype(BF16)
    qa = jnp.dot(cqn, wuq_ref[...], preferred_element_type=F32)
    if rope:
        ca = jnp.concatenate([ca_ref[...]] * MLA_HEADS, axis=1)
        sa = jnp.concatenate([sa_ref[...]] * MLA_HEADS, axis=1)
        qa = qa * ca + _swap_halves(qa, MLA_ROPE // 4) * sa
    qa_ref[...] = (qa * ((MLA_NOPE + MLA_ROPE) ** -0.5 * LOG2E)).astype(BF16)

    ckv_n = _rms(ckv_ref[...], kvng_ref[...])
    ckvn_ref[...] = ckv_n
    krmg = krmg_ref[...]
    lane = lax.broadcasted_iota(jnp.int32, krmg.shape, 1)
    if rope:
        krmg = krmg * ca_ref[:, LANE:] + _swap_halves(krmg, MLA_ROPE // 4) * sa_ref[:, LANE:]
    kr128 = jnp.where(lane < MLA_ROPE, krmg, 0.0)
    _mla_kv(ckv_n, kr128, wukv_ref, ka_ref, va_ref)

    hd = GQA_HEAD_DIM
    gq, gk = gq_ref[...], gk_ref[...]
    for h in range(GQA_HEADS):
        q = _rms(gq[:, h * hd:(h + 1) * hd], gqg_ref[...])
        if rope:
            q = q * cb_ref[...] + _swap_halves(q, hd // 4) * sb_ref[...]
        qb_ref[:, h * hd:(h + 1) * hd] = (q * (hd ** -0.5 * LOG2E)).astype(BF16)
    for h in range(GQA_KV_HEADS):
        k = _rms(gk[:, h * hd:(h + 1) * hd], gkg_ref[...])
        kbn_ref[:, h * hd:(h + 1) * hd] = k
        if rope:
            k = k * cb_ref[...] + _swap_halves(k, hd // 4) * sb_ref[...]
        kb_ref[:, h * hd:(h + 1) * hd] = k.astype(BF16)
    vb_ref[...] = gv_ref[...].astype(BF16)


def _attn_prep(proj, tables, pr, *, layer, seq, tm_pref=512):
    n = proj.shape[0]
    tm = _pick(seq, tm_pref)
    per = seq // tm
    rope = tables is not None

    def pblock(name, width):
        idx = _COL[name] // width
        assert idx * width == _COL[name]
        return pl.BlockSpec((tm, width), lambda i: (i, idx))

    in_specs = [pblock("cq", MLA_Q_LORA), pblock("ckv", MLA_KV_LORA),
                pl.BlockSpec((tm, LANE), lambda i: (i, KRMG_BLOCK)),
                pblock("gq", 512), pblock("gk", 256), pblock("gv", 256)]
    args = [proj] * 6
    if rope:
        in_specs += [pl.BlockSpec((tm, t.shape[1]), lambda i: (i % per, 0)) for t in tables]
        args += list(tables)
    ws = [pr["mla_q_norm"], pr["mla_kv_norm"], pr["mla_w_uq"], pr["mla_w_ukv"], pr["gqa_q_norm"], pr["gqa_k_norm"]]
    in_specs += [_layer_spec(w, layer, 1) for w in ws]
    args += ws
    widths = [(4 * 2 * LANE, BF16), (4 * 2 * LANE, BF16), (4 * MLA_V, BF16), (MLA_KV_LORA, F32),
              (GQA_HEADS * GQA_HEAD_DIM, BF16), (GQA_KV_HEADS * GQA_HEAD_DIM, BF16),
              (GQA_KV_HEADS * GQA_HEAD_DIM, BF16), (GQA_KV_HEADS * GQA_HEAD_DIM, F32)]
    return pl.pallas_call(
        functools.partial(_attn_prep_kernel, rope=rope),
        out_shape=tuple(jax.ShapeDtypeStruct((n, w), dt) for w, dt in widths),
        grid=(n // tm,),
        in_specs=in_specs,
        out_specs=tuple(pl.BlockSpec((tm, w), lambda i: (i, 0)) for w, _ in widths),
        compiler_params=_cparams(("parallel",)),
        name="attn_prep",
    )(*args)


def _cache_kv_kernel(ckv_ref, kr_ref, wukv_ref, ka_ref, va_ref):
    _mla_kv(ckv_ref[...], kr_ref[...], wukv_ref, ka_ref, va_ref)


def _cache_kv(ckv, kr128, wukv, *, layer):
    b, _, p, _ = ckv.shape
    return pl.pallas_call(
        _cache_kv_kernel,
        out_shape=(jax.ShapeDtypeStruct((b, p, 4 * 2 * LANE), BF16), jax.ShapeDtypeStruct((b, p, 4 * MLA_V), BF16)),
        grid=(b,),
        in_specs=[pl.BlockSpec((None, None, p, MLA_KV_LORA), lambda i: (i, layer, 0, 0)),
                  pl.BlockSpec((None, None, p, LANE), lambda i: (i, layer, 0, 0)),
                  _layer_spec(wukv, layer, 1)],
        out_specs=(pl.BlockSpec((None, p, 4 * 2 * LANE), lambda i: (i, 0, 0)),
                   pl.BlockSpec((None, p, 4 * MLA_V), lambda i: (i, 0, 0))),
        compiler_params=_cparams(("parallel",)),
        name="mla_cache_kv",
    )(ckv, kr128, wukv)


def _attn_kernel(q_ref, *refs, nseg, hp, g, dk, dv):
    o_ref = refs[-1]
    for j in range(hp):
        kv = j // g
        q = q_ref[:, j * dk:(j + 1) * dk]
        ks = [refs[2 * i][:, kv * dk:(kv + 1) * dk].astype(BF16) for i in range(nseg)]
        vs = [refs[2 * i + 1][:, kv * dv:(kv + 1) * dv].astype(BF16) for i in range(nseg)]
        ss = [lax.dot_general(q, k, (((1,), (1,)), ((), ())), preferred_element_type=F32) for k in ks]
        m = functools.reduce(jnp.maximum, [jnp.max(s, axis=-1, keepdims=True) for s in ss])
        ps = [jnp.exp2(s - m) for s in ss]
        l = functools.reduce(jnp.add, [jnp.sum(p, axis=-1, keepdims=True) for p in ps])
        o = functools.reduce(jnp.add, [jnp.dot(p.astype(BF16), v, preferred_element_type=F32)
                                       for p, v in zip(ps, vs)])
        o_ref[:, j * dv:(j + 1) * dv] = o / l


def _attention(q, segs, *, heads, kv_heads, dk, dv, tq_pref=512, hp=4):
    b, sq, _ = q.shape
    tq = _pick(sq, tq_pref)
    g = heads // kv_heads
    assert heads % hp == 0 and (hp % g == 0 or g % hp == 0)
    kvp = max(1, hp // g)
    kcol = (lambda hb: hb) if hp >= g else (lambda hb: hb * hp // g)
    in_specs = [pl.BlockSpec((None, tq, hp * dk), lambda bi, hb, i: (bi, i, hb))]
    args = [q]
    for k, v, layer in segs:
        sk = k.shape[-2]
        if layer is None:
            in_specs += [pl.BlockSpec((None, sk, kvp * dk), lambda bi, hb, i: (bi, 0, kcol(hb))),
                         pl.BlockSpec((None, sk, kvp * dv), lambda bi, hb, i: (bi, 0, kcol(hb)))]
        else:
            in_specs += [pl.BlockSpec((None, None, sk, kvp * dk),
                                      lambda bi, hb, i, layer=layer: (bi, layer, 0, kcol(hb))),
                         pl.BlockSpec((None, None, sk, kvp * dv),
                                      lambda bi, hb, i, layer=layer: (bi, layer, 0, kcol(hb)))]
        args += [k, v]
    return pl.pallas_call(
        functools.partial(_attn_kernel, nseg=len(segs), hp=hp, g=min(g, hp), dk=dk, dv=dv),
        out_shape=jax.ShapeDtypeStruct((b, sq, heads * dv), F32),
        grid=(b, heads // hp, sq // tq),
        in_specs=in_specs,
        out_specs=pl.BlockSpec((None, tq, hp * dv), lambda bi, hb, i: (bi, i, hb)),
        compiler_params=_cparams(("parallel", "parallel", "parallel")),
        name="attention",
    )(*args)


def _log_sigmoid(x):
    return jnp.minimum(x, 0.0) - jnp.log1p(jnp.exp(-jnp.abs(x)))


def _mlstm_chunk_operands(q_ref, k_ref, v_ref, gate_ref, bias_ref):
    nh, hd = MLSTM_HEADS, MLSTM_HEAD_DIM
    g = gate_ref[...] + bias_ref[...]
    lf = _log_sigmoid(g)
    lf_hi = lf.astype(BF16)
    rem = lf - lf_hi.astype(F32)
    lf_mid = rem.astype(BF16)
    lf_lo = (rem - lf_mid.astype(F32)).astype(BF16)
    heads = []
    for h in range(nh):
        q = q_ref[:, h * hd:(h + 1) * hd]
        kh = k_ref[:, h * hd:(h + 1) * hd] * (hd ** -0.5)
        kt = kh.T
        qh = q.astype(BF16)
        heads.append(dict(q=q, qh=qh, kh=kh, kt=kt, vh=v_ref[:, h * hd:(h + 1) * hd].astype(BF16),
                          s=jnp.dot(qh, kt.astype(BF16), preferred_element_type=F32)))
    return dict(g=g, gt=g.T, lf=(lf_hi, lf_mid, lf_lo), heads=heads)


def _mlstm_chain(ops, h_ref, c_s, n_s, m_s, *, direction):
    nh, hd = MLSTM_HEADS, MLSTM_HEAD_DIM
    rev = direction == 1
    g, gt = ops["g"], ops["gt"]
    cl = g.shape[0]
    ri = lax.broadcasted_iota(jnp.int32, (cl, cl), 0)
    ci = lax.broadcasted_iota(jnp.int32, (cl, cl), 1)
    mask = (ri <= ci) if rev else (ri >= ci)
    tri = jnp.where(mask, 1.0, 0.0).astype(BF16)
    cum = functools.reduce(jnp.add, [jnp.dot(tri, part, preferred_element_type=F32) for part in ops["lf"]])
    cumt = cum.T
    last = 0 if rev else cl - 1
    li0 = MG_LANE + direction * 2 * nh
    lf0 = li0 + nh

    for h in range(nh):
        hv = ops["heads"][h]
        cum_c = cum[:, lf0 + h:lf0 + h + 1]
        cum_r = cumt[lf0 + h:lf0 + h + 1, :]
        li_r = gt[li0 + h:li0 + h + 1, :]
        li_c = g[:, li0 + h:li0 + h + 1]
        m_prev = m_s[direction, h:h + 1, 0:1]
        dmat = jnp.where(mask, cum_c - cum_r + li_r, -jnp.inf)
        inter = cum_c + m_prev
        m_t = jnp.maximum(inter, jnp.max(dmat, axis=1, keepdims=True))
        sc = hv["s"] * jnp.exp(dmat - m_t)
        inter_w = jnp.exp(inter - m_t)
        cmat = c_s[direction, h]
        nrow = n_s[direction, h:h + 1, :]
        num = (jnp.dot(sc.astype(BF16), hv["vh"], preferred_element_type=F32)
               + inter_w * jnp.dot(hv["qh"], cmat.astype(BF16), preferred_element_type=F32))
        qn = jnp.sum(hv["q"] * nrow, axis=1, keepdims=True)
        den = jnp.sum(sc, axis=1, keepdims=True) + inter_w * qn
        h_ref[:, h * hd:(h + 1) * hd] = num / jnp.maximum(jnp.abs(den), jnp.exp(-m_t))

        tot = cum_c[last:last + 1, :]
        g_r = tot - cum_r + li_r
        g_c = tot - cum_c + li_c
        m_new = jnp.maximum(tot + m_prev, jnp.max(g_r, axis=1, keepdims=True))
        decay = jnp.exp(tot + m_prev - m_new)
        ws_r = jnp.exp(g_r - m_new)
        ws_c = jnp.exp(g_c - m_new)
        c_s[direction, h] = decay * cmat + jnp.dot((hv["kt"] * ws_r).astype(BF16), hv["vh"],
                                                   preferred_element_type=F32)
        n_s[direction, h:h + 1, :] = decay * nrow + jnp.sum(hv["kh"] * ws_c, axis=0, keepdims=True)
        m_s[direction, h:h + 1, :] = jnp.broadcast_to(m_new, (1, hd))


def _mlstm_kernel(*refs, shared, has_init, emit_state):
    nin = 4 if shared else 8
    fwd_in = refs[:4]
    bwd_in = fwd_in if shared else refs[4:8]
    bias_ref = refs[nin]
    pos = nin + 1
    if has_init:
        c0_ref, n0_ref, m0_ref = refs[pos:pos + 3]
        pos += 3
    hf_ref, hb_ref = refs[pos:pos + 2]
    pos += 2
    if emit_state:
        co_ref, no_ref, mo_ref = refs[pos:pos + 3]
        pos += 3
    c_s, n_s, m_s = refs[pos:pos + 3]
    c = pl.program_id(1)

    @pl.when(c == 0)
    def _():
        if has_init:
            c_s[...] = c0_ref[...]
            n_s[...] = n0_ref[...]
            m_s[...] = m0_ref[...]
        else:
            c_s[...] = jnp.zeros_like(c_s)
            n_s[...] = jnp.zeros_like(n_s)
            m_s[...] = jnp.zeros_like(m_s)

    ops_f = _mlstm_chunk_operands(*fwd_in, bias_ref)
    ops_b = ops_f if shared else _mlstm_chunk_operands(*bwd_in, bias_ref)
    _mlstm_chain(ops_f, hf_ref, c_s, n_s, m_s, direction=0)
    _mlstm_chain(ops_b, hb_ref, c_s, n_s, m_s, direction=1)

    if emit_state:
        @pl.when(c == pl.num_programs(1) - 1)
        def _():
            co_ref[...] = c_s[...]
            no_ref[...] = n_s[...]
            mo_ref[...] = m_s[...]


def _mlstm(proj, bias128, init, *, layer, batch, seq, emit_state, chunk_pref=256):
    n = proj.shape[0]
    cl = _pick(seq, chunk_pref)
    nc = seq // cl
    shared = nc == 1
    nh, hd = MLSTM_HEADS, MLSTM_HEAD_DIM
    rows = (lambda b, c: b * nc + c), (lambda b, c: b * nc + (nc - 1 - c))

    def chunk_specs(row):
        qkv = lambda name: pl.BlockSpec((cl, nh * hd), lambda b, c: (row(b, c), _COL[name] // (nh * hd)))
        return [qkv("mq"), qkv("mk"), qkv("mv"), pl.BlockSpec((cl, LANE), lambda b, c: (row(b, c), KRMG_BLOCK))]

    in_specs = chunk_specs(rows[0]) + ([] if shared else chunk_specs(rows[1]))
    in_specs.append(pl.BlockSpec((None, 1, LANE), lambda b, c: (layer, 0, 0)))
    args = [proj] * (len(in_specs) - 1) + [bias128]
    has_init = init is not None
    if has_init:
        c0, n0, m0 = init
        in_specs += [pl.BlockSpec((None, None, 2, nh, hd, hd), lambda b, c: (b, layer, 0, 0, 0, 0)),
                     pl.BlockSpec((None, None, 2, nh, hd), lambda b, c: (b, layer, 0, 0, 0)),
                     pl.BlockSpec((None, None, 2, nh, hd), lambda b, c: (b, layer, 0, 0, 0))]
        args += [c0, n0, m0]
    out_shape = [jax.ShapeDtypeStruct((n, nh * hd), F32)] * 2
    out_specs = [pl.BlockSpec((cl, nh * hd), lambda b, c, row=row: (row(b, c), 0)) for row in rows]
    if emit_state:
        out_shape += [jax.ShapeDtypeStruct((batch, 2, nh, hd, hd), F32), jax.ShapeDtypeStruct((batch, 2, nh, hd), F32),
                      jax.ShapeDtypeStruct((batch, 2, nh, hd), F32)]
        out_specs += [pl.BlockSpec((None, 2, nh, hd, hd), lambda b, c: (b, 0, 0, 0, 0)),
                      pl.BlockSpec((None, 2, nh, hd), lambda b, c: (b, 0, 0, 0)),
                      pl.BlockSpec((None, 2, nh, hd), lambda b, c: (b, 0, 0, 0))]
    return pl.pallas_call(
        functools.partial(_mlstm_kernel, shared=shared, has_init=has_init, emit_state=emit_state),
        out_shape=tuple(out_shape),
        grid=(batch, nc),
        in_specs=in_specs,
        out_specs=tuple(out_specs),
        scratch_shapes=[pltpu.VMEM((2, nh, hd, hd), F32), pltpu.VMEM((2, nh, hd), F32), pltpu.VMEM((2, nh, hd), F32)],
        compiler_params=_cparams(("parallel", "arbitrary")),
        name="mlstm",
    )(*args)


_S5_STATE_TILES = 2 * S5_NS // MXU_TILE
_S5_GROUPS_PER_TILE = MXU_TILE // S5_STATE


def _s5_channel_tile(state_tile):
    first_group = (state_tile % (S5_NS // MXU_TILE)) * _S5_GROUPS_PER_TILE
    return first_group * S5_GROUP // MXU_TILE


def _to_time_major_kernel(u_ref, p_ref, o_ref):
    sg, tsteps, ch = u_ref.shape
    u = u_ref[...].reshape(sg * tsteps, ch).astype(BF16)
    o_ref[...] = jnp.dot(p_ref[...], u, preferred_element_type=F32).astype(BF16)


def _from_time_major_kernel(yf_ref, yb_ref, pt_ref, o_ref):
    sg, tsteps, ch = o_ref.shape
    y = yf_ref[...] + yb_ref[...]
    hi = y.astype(BF16)
    lo = (y - hi.astype(F32)).astype(BF16)
    pt = pt_ref[...]
    out = jnp.dot(pt, hi, preferred_element_type=F32) + jnp.dot(pt, lo, preferred_element_type=F32)
    o_ref[...] = out.reshape(sg, tsteps, ch)


def _s5_kernel(*refs, has_init):
    if has_init:
        u_ref, bd_ref, a_ref, cd_ref, x0_ref, y_ref, xf_ref, bu_s, x_s = refs
    else:
        u_ref, bd_ref, a_ref, cd_ref, y_ref, xf_ref, bu_s, x_s = refs
    d = pl.program_id(0)
    c = pl.program_id(2)
    ns = S5_NS
    sg = x_s.shape[0]
    tsteps = u_ref.shape[0] // sg
    mt = MXU_TILE

    @pl.when(c == 0)
    def _():
        if has_init:
            x_s[...] = x0_ref[...]
        else:
            x_s[...] = jnp.zeros_like(x_s)

    u = u_ref[...]
    for st in range(_S5_STATE_TILES):
        ct = _s5_channel_tile(st)
        bu_s[:, st * mt:(st + 1) * mt] = jnp.dot(u[:, ct * mt:(ct + 1) * mt],
                                                 bd_ref[ct * mt:(ct + 1) * mt, st * mt:(st + 1) * mt],
                                                 preferred_element_type=F32)
    a_re = jnp.broadcast_to(a_ref[:, :ns], (sg, ns))
    a_im = jnp.broadcast_to(a_ref[:, ns:], (sg, ns))

    def body(t, carry):
        xr, xi = carry
        tt = t + d * (tsteps - 1 - 2 * t)
        r0 = pl.multiple_of(tt * sg, sg)
        br = bu_s[pl.ds(r0, sg), :ns]
        bi = bu_s[pl.ds(r0, sg), ns:]
        nr = a_re * xr - a_im * xi + br
        ni = a_re * xi + a_im * xr + bi
        bu_s[pl.ds(r0, sg), :ns] = nr
        bu_s[pl.ds(r0, sg), ns:] = ni
        return nr, ni

    xr, xi = lax.fori_loop(0, tsteps, body, (x_s[:, :ns], x_s[:, ns:]))
    x_s[:, :ns] = xr
    x_s[:, ns:] = xi

    for ct in range(S5_CH // mt):
        acc = None
        for st in range(_S5_STATE_TILES):
            if _s5_channel_tile(st) != ct:
                continue
            t = jnp.dot(bu_s[:, st * mt:(st + 1) * mt].astype(BF16),
                        cd_ref[st * mt:(st + 1) * mt, ct * mt:(ct + 1) * mt], preferred_element_type=F32)
            acc = t if acc is None else acc + t
        y_ref[:, ct * mt:(ct + 1) * mt] = acc

    @pl.when(c == pl.num_programs(2) - 1)
    def _():
        xf_ref[...] = x_s[...]


def _s5(proj3, pr, x0, *, layer, rows_pref=512, scan_rows_pref=1024):
    batch, seq, _ = proj3.shape
    sg = SUBLANE if batch % SUBLANE == 0 else batch
    ng = batch // sg
    tsteps = _pick(seq, max(SUBLANE, rows_pref // sg))
    rows = sg * tsteps
    nc = seq // tsteps
    r = jnp.arange(rows)
    perm = (jnp.arange(rows)[None, :] == ((r % sg) * tsteps + r // sg)[:, None]).astype(BF16)
    const2 = lambda a: pl.BlockSpec(a.shape, lambda g, c: (0,) * a.ndim, pipeline_mode=pl.Buffered(1))

    u_tm = pl.pallas_call(
        _to_time_major_kernel,
        out_shape=jax.ShapeDtypeStruct((ng, seq * sg, S5_CH), BF16),
        grid=(ng, nc),
        in_specs=[pl.BlockSpec((sg, tsteps, S5_CH), lambda g, c: (g, c, _COL["su"] // S5_CH)), const2(perm)],
        out_specs=pl.BlockSpec((None, rows, S5_CH), lambda g, c: (g, c, 0)),
        compiler_params=_cparams(("parallel", "parallel")),
        name="s5_to_time_major",
    )(proj3, perm)

    ts_scan = _pick(seq, max(SUBLANE, scan_rows_pref // sg))
    rows_scan = sg * ts_scan
    nc_scan = seq // ts_scan
    cpos = lambda d, c: c + d * (nc_scan - 1 - 2 * c)
    has_init = x0 is not None
    in_specs = [pl.BlockSpec((None, rows_scan, S5_CH), lambda d, g, c: (g, cpos(d, c), 0)),
                pl.BlockSpec((None, None, S5_CH, 2 * S5_NS), lambda d, g, c: (layer, d, 0, 0)),
                pl.BlockSpec((None, None, 1, 2 * S5_NS), lambda d, g, c: (layer, d, 0, 0)),
                _layer_spec(pr["s5_cd"], layer, 3)]
    args = [u_tm, pr["s5_bd"], pr["s5_abar"], pr["s5_cd"]]
    if has_init:
        in_specs.append(pl.BlockSpec((None, sg, 2 * S5_NS), lambda d, g, c: (d, g, 0)))
        args.append(x0)
    y_tm, xfin = pl.pallas_call(
        functools.partial(_s5_kernel, has_init=has_init),
        out_shape=(jax.ShapeDtypeStruct((2, ng, seq * sg, S5_CH), F32),
                   jax.ShapeDtypeStruct((2, batch, 2 * S5_NS), F32)),
        grid=(2, ng, nc_scan),
        in_specs=in_specs,
        out_specs=(pl.BlockSpec((None, None, rows_scan, S5_CH), lambda d, g, c: (d, g, cpos(d, c), 0)),
                   pl.BlockSpec((None, sg, 2 * S5_NS), lambda d, g, c: (d, g, 0))),
        scratch_shapes=[pltpu.VMEM((rows_scan, 2 * S5_NS), F32), pltpu.VMEM((sg, 2 * S5_NS), F32)],
        compiler_params=_cparams(("parallel", "parallel", "arbitrary")),
        name="s5_scan",
    )(*args)

    y = pl.pallas_call(
        _from_time_major_kernel,
        out_shape=jax.ShapeDtypeStruct((batch, seq, S5_CH), F32),
        grid=(ng, nc),
        in_specs=[pl.BlockSpec((None, None, rows, S5_CH), lambda g, c: (0, g, c, 0)),
                  pl.BlockSpec((None, None, rows, S5_CH), lambda g, c: (1, g, c, 0)),
                  const2(perm)],
        out_specs=pl.BlockSpec((sg, tsteps, S5_CH), lambda g, c: (g, c, 0)),
        compiler_params=_cparams(("parallel", "parallel")),
        name="s5_from_time_major",
    )(y_tm, y_tm, perm.T)
    return y, xfin


def _gelu_tanh(x):
    return 0.5 * x * (1.0 + jnp.tanh(math.sqrt(2.0 / math.pi) * (x + 0.044715 * (x * x * x))))


def _outproj_kernel(x_ref, mod_ref, oa_ref, ob_ref, hf_ref, hb_ref, mo_ref, y_ref, su_ref,
                    d_ref, wglu_ref, on_ref, wout_ref, o_ref):
    oc = jax.nn.sigmoid(mo_ref[...]) * (hf_ref[...] + hb_ref[...])
    y = _gelu_tanh(y_ref[...] + d_ref[...] * su_ref[...])
    od = y * jax.nn.sigmoid(jnp.dot(y.astype(BF16), wglu_ref[...], preferred_element_type=F32))
    acc = None
    for gi, part in enumerate((oa_ref[...], ob_ref[...], oc, od)):
        nrm = _rms(part, on_ref[gi:gi + 1, :]).astype(BF16)
        t = jnp.dot(nrm, wout_ref[gi * GROUP_WIDTH:(gi + 1) * GROUP_WIDTH, :], preferred_element_type=F32)
        acc = t if acc is None else acc + t
    o_ref[...] = x_ref[...] + mod_ref[5:6, :] * acc


def _outproj(x, mod, oa, ob, hf, hb, proj, y, pr, *, layer, seq, tm_pref=512):
    n, d = x.shape
    groups = mod.shape[0]
    tm = _pick(seq if groups > 1 else n, tm_pref)
    per = seq // tm if groups > 1 else 1
    mod_map = (lambda i: (i // per, 0, 0)) if groups > 1 else (lambda i: (0, 0, 0))
    gw = GROUP_WIDTH
    rowblk = pl.BlockSpec((tm, gw), lambda i: (i, 0))
    ws = [pr["s5_d"], pr["s5_w_glu"], pr["out_norm"], pr["w_out"]]
    return pl.pallas_call(
        _outproj_kernel,
        out_shape=jax.ShapeDtypeStruct((n, d), F32),
        grid=(n // tm,),
        in_specs=[pl.BlockSpec((tm, d), lambda i: (i, 0)),
                  pl.BlockSpec((None, N_MOD, d), mod_map),
                  rowblk, rowblk, rowblk, rowblk,
                  pl.BlockSpec((tm, gw), lambda i: (i, _COL["mo"] // gw)),
                  rowblk,
                  pl.BlockSpec((tm, gw), lambda i: (i, _COL["su"] // gw))] + [_layer_spec(w, layer, 1) for w in ws],
        out_specs=pl.BlockSpec((tm, d), lambda i: (i, 0)),
        compiler_params=_cparams(("parallel",)),
        name="merge_out_proj",
    )(x, mod, oa, ob, hf, hb, proj, y, proj, *ws)


def _final_norm_kernel(x_ref, g_ref, o_ref):
    o_ref[...] = _rms(x_ref[...], g_ref[...])


def _final_norm(x, g, tm_pref=1024):
    n, d = x.shape
    tm = _pick(n, tm_pref)
    return pl.pallas_call(
        _final_norm_kernel,
        out_shape=jax.ShapeDtypeStruct((n, d), F32),
        grid=(n // tm,),
        in_specs=[pl.BlockSpec((tm, d), lambda i: (i, 0)), pl.BlockSpec((1, d), lambda i: (0, 0))],
        out_specs=pl.BlockSpec((tm, d), lambda i: (i, 0)),
        compiler_params=_cparams(("parallel",)),
        name="final_norm",
    )(x, g)


def _trunk_layer(x, mod, pr, *, layer, batch, seq, tables, cache):
    latent = cache is not None
    x = _ffn(x, mod, pr, layer=layer, which=0, seq=seq)
    proj = _inproj(x, mod, pr, layer=layer, seq=seq)

    qa, ka, va, ckvn, qb, kb, vb, kbn = _attn_prep(proj, tables, pr, layer=layer, seq=seq)
    r3 = lambda a: a.reshape(batch, seq, a.shape[-1])
    segs_a = [(r3(ka), r3(va), None)]
    segs_b = [(r3(kb), r3(vb), None)]
    if latent:
        segs_a.insert(0, cache["mla_kv"] + (None,))
        segs_b.insert(0, cache["gqa_kv"] + (layer,))
    oa = _attention(r3(qa), segs_a, heads=MLA_HEADS, kv_heads=MLA_HEADS, dk=2 * LANE, dv=MLA_V)
    ob = _attention(r3(qb), segs_b, heads=GQA_HEADS, kv_heads=GQA_KV_HEADS, dk=GQA_HEAD_DIM, dv=GQA_HEAD_DIM)
    oa = oa.reshape(batch * seq, -1)
    ob = ob.reshape(batch * seq, -1)

    m_init = cache["mlstm"] if latent else None
    mres = _mlstm(proj, pr["mlstm_bias"], m_init, layer=layer, batch=batch, seq=seq, emit_state=not latent)

    y, xfin = _s5(proj.reshape(batch, seq, PROJ_COLS), pr, cache["s5"] if latent else None, layer=layer)

    x = _outproj(x, mod, oa, ob, mres[0], mres[1], proj, y.reshape(batch * seq, S5_CH), pr,
                 layer=layer, seq=seq)
    x = _ffn(x, mod, pr, layer=layer, which=1, seq=seq)

    new_ctx = None
    if not latent:
        kr = proj[:, _COL["kr"]:_COL["kr"] + MLA_ROPE]
        gv = proj[:, _COL["gv"]:_COL["gv"] + GQA_KV_HEADS * GQA_HEAD_DIM]
        xs = xfin.reshape(2, batch, 2, S5_GROUPS, S5_STATE).transpose(1, 0, 2, 3, 4)
        new_ctx = (ckvn.reshape(batch, seq, MLA_KV_LORA),
                   kr.reshape(batch, seq, MLA_ROPE),
                   kbn.reshape(batch, seq, GQA_KV_HEADS, GQA_HEAD_DIM),
                   gv.reshape(batch, seq, GQA_KV_HEADS, GQA_HEAD_DIM),
                   mres[2], mres[3], mres[4][..., 0],
                   xs[:, :, 0], xs[:, :, 1])
    return x, new_ctx


def _permute_w_in(w_in):
    parts = [w_in[..., _ORIG[n][0]:_ORIG[n][0] + _ORIG[n][1]] for n in _ORDER]
    pad = PROJ_COLS - sum(p.shape[-1] for p in parts)
    parts.append(jnp.zeros(w_in.shape[:-1] + (pad,), w_in.dtype))
    return jnp.concatenate(parts, axis=-1).astype(BF16)


def _permute_w_uq(w_uq):
    depth, k, _ = w_uq.shape
    w = w_uq.reshape(depth, k, MLA_HEADS, MLA_NOPE + MLA_ROPE)
    w = jnp.pad(w, ((0, 0), (0, 0), (0, 0), (0, 2 * LANE - MLA_NOPE - MLA_ROPE)))
    return w.reshape(depth, k, MLA_HEADS * 2 * LANE).astype(BF16)


def kernel(x_prompt, x_sample, cache_mla_ckv, cache_mla_krope, cache_gqa_k, cache_gqa_v, state_mlstm_c, state_mlstm_n, state_mlstm_m, state_s5_re, state_s5_im, c, c_ctx, ada_w, ada_b, norm_g, ffn_w13, ffn_w2, w_in, mla_q_norm, mla_kv_norm, mla_w_uq, mla_w_ukv, gqa_q_norm, gqa_k_norm, mlstm_gate_b, s5_a_re, s5_a_im, s5_log_dt, s5_b_re, s5_b_im, s5_c_re, s5_c_im, s5_d, s5_w_glu, out_norm, w_out, final_norm):
    bc, sc, d = x_prompt.shape
    bl, sl, _ = x_sample.shape
    depth = ada_w.shape[0]
    past = cache_mla_ckv.shape[2]

    rows = ((1 + bl + SUBLANE - 1) // SUBLANE) * SUBLANE
    cvecs = jnp.concatenate([c_ctx[None, :], c, jnp.zeros((rows - 1 - bl, d), F32)], axis=0)
    mod_all = _modulation(cvecs, ada_w, ada_b).reshape(depth, rows, N_MOD, d)

    tables = _rope_tables(sl)

    abar, bd, cd = _s5_params(s5_a_re, s5_a_im, s5_log_dt, s5_b_re, s5_b_im, s5_c_re, s5_c_im)
    nmg = MLSTM_HEADS * 4
    row = lambda a: a.reshape(depth, 1, a.shape[-1])
    pr = {"norm_g": norm_g.reshape(depth, 3, 1, d),
          "ffn_w13": ffn_w13.astype(BF16), "ffn_w2": ffn_w2.astype(BF16),
          "w_in": _permute_w_in(w_in),
          "mla_q_norm": row(mla_q_norm), "mla_kv_norm": row(mla_kv_norm),
          "mla_w_uq": _permute_w_uq(mla_w_uq), "mla_w_ukv": mla_w_ukv.astype(BF16),
          "gqa_q_norm": row(gqa_q_norm), "gqa_k_norm": row(gqa_k_norm),
          "mlstm_bias": jnp.pad(mlstm_gate_b.reshape(depth, 1, nmg), ((0, 0), (0, 0), (MG_LANE, LANE - MG_LANE - nmg))),
          "s5_abar": abar, "s5_bd": bd, "s5_cd": cd,
          "s5_d": row(s5_d), "s5_w_glu": s5_w_glu.astype(BF16),
          "out_norm": out_norm.reshape(depth, 4, GROUP_WIDTH), "w_out": w_out.astype(BF16)}

    gkv = GQA_KV_HEADS * GQA_HEAD_DIM
    cache_k = cache_gqa_k.reshape(bl, depth, past, gkv)
    cache_v = cache_gqa_v.reshape(bl, depth, past, gkv)
    m0 = jnp.broadcast_to(state_mlstm_m[..., None], state_mlstm_n.shape)
    kr_pad = jnp.pad(cache_mla_krope, ((0, 0), (0, 0), (0, 0), (0, LANE - MLA_ROPE)))

    x_ctx = x_prompt.reshape(bc * sc, d)
    x_lat = x_sample.reshape(bl * sl, d)
    per_layer = []
    for l in range(depth):
        x_ctx, ctx_l = _trunk_layer(x_ctx, mod_all[l, 0:1], pr, layer=l, batch=bc, seq=sc, tables=None, cache=None)
        per_layer.append(ctx_l)

        x0 = jnp.concatenate([state_s5_re[:, l].reshape(bl, 2, S5_NS), state_s5_im[:, l].reshape(bl, 2, S5_NS)],
                             axis=-1).transpose(1, 0, 2)
        cache = {"mla_kv": tuple(_cache_kv(cache_mla_ckv, kr_pad, pr["mla_w_ukv"], layer=l)),
                 "gqa_kv": (cache_k, cache_v),
                 "mlstm": (state_mlstm_c, state_mlstm_n, m0),
                 "s5": x0}
        x_lat, _ = _trunk_layer(x_lat, mod_all[l, 1:1 + bl], pr, layer=l, batch=bl, seq=sl, tables=tables,
                                cache=cache)

    new_ctx = [jnp.stack([t[i] for t in per_layer], axis=1) for i in range(9)]
    y_prompt = _final_norm(x_ctx, final_norm[None, :]).reshape(bc, sc, d)
    y_sample = _final_norm(x_lat, final_norm[None, :]).reshape(bl, sl, d)
    return (y_prompt, y_sample, *new_ctx)
```

```python
import functools
import math

import jax
import jax.numpy as jnp
from jax import lax
from jax.experimental import pallas as pl
from jax.experimental.pallas import tpu as pltpu

F32 = jnp.float32
BF16 = jnp.bfloat16

EPS = 1e-6
ROPE_THETA = 10000.0
GRID_W = 64
N_MOD = 9
LOG2E = math.log2(math.e)

D_FF = 5632
MLA_HEADS, MLA_NOPE, MLA_ROPE, MLA_V = 4, 128, 64, 128
MLA_Q_LORA, MLA_KV_LORA = 384, 256
GQA_HEADS, GQA_KV_HEADS, GQA_HEAD_DIM = 4, 2, 128
MLSTM_HEADS, MLSTM_HEAD_DIM = 4, 128
S5_GROUPS, S5_GROUP, S5_STATE = 32, 16, 64
S5_CH = S5_GROUPS * S5_GROUP
S5_NS = S5_GROUPS * S5_STATE
GROUP_WIDTH = 512

LANE = 128
SUBLANE = 8
MXU_TILE = 256
NORM_ROWS = 128
FFN_TILE = 512
RELAYOUT_SUB = 4
OUTPROJ_SLAB = 256
SCAN_UNROLL = 4
VMEM_BYTES = 64 * 1024 * 1024
VMEM_LIMIT_BYTES = VMEM_BYTES - 8 * 1024 * 1024
FFN_VMEM_LIMIT_BYTES = VMEM_BYTES - 4 * 1024 * 1024

_ORIG = dict(cq=(0, 384), ckv=(384, 256), kr=(640, 64), gq=(704, 512), gk=(1216, 256), gv=(1472, 256),
             mq=(1728, 512), mk=(2240, 512), mv=(2752, 512), mo=(3264, 512), mg=(3776, 16), su=(3792, 512))
_ORDER = ("mq", "mk", "mv", "mo", "gq", "su", "gk", "gv", "ckv", "cq", "kr", "mg")
_COL = {}
_off = 0
for _n in _ORDER:
    _COL[_n] = _off
    _off += _ORIG[_n][1]
PROJ_COLS = ((_off + LANE - 1) // LANE) * LANE
KRMG_BLOCK = _COL["kr"] // LANE
MG_LANE = _COL["mg"] - _COL["kr"]


def _cparams(sem, vmem_limit=VMEM_LIMIT_BYTES):
    return pltpu.CompilerParams(dimension_semantics=sem, vmem_limit_bytes=vmem_limit)


def _pick(n, pref):
    for t in range(min(n, pref), 0, -1):
        if n % t == 0 and (t % SUBLANE == 0 or t == n):
            return t
    return n


def _layer_spec(arr, layer, nargs):
    zeros = (0,) * (arr.ndim - 1)
    imap = {1: lambda i: (layer,) + zeros, 2: lambda i, j: (layer,) + zeros,
            3: lambda i, j, k: (layer,) + zeros}[nargs]
    return pl.BlockSpec((None,) + arr.shape[1:], imap, pipeline_mode=pl.Buffered(1))


def _rms(x, g):
    return x * lax.rsqrt(jnp.mean(x * x, axis=-1, keepdims=True) + EPS) * g


def _modulate(x, g, scale, shift):
    return x * lax.rsqrt(jnp.mean(x * x, axis=-1, keepdims=True) + EPS) * (g * (1.0 + scale)) + shift


def _swap_halves(x, half):
    w = x.shape[-1]
    lane = lax.broadcasted_iota(jnp.int32, x.shape, x.ndim - 1)
    first = (lane & (2 * half - 1)) < half
    return jnp.where(first, pltpu.roll(x, w - half, axis=x.ndim - 1), pltpu.roll(x, half, axis=x.ndim - 1))


def _silu(x):
    return x * jax.nn.sigmoid(x)


def _mod_kernel(c_ref, w_ref, b_ref, o_ref):
    a = _silu(c_ref[...]).astype(BF16)
    o_ref[...] = jnp.dot(a, w_ref[...].astype(BF16), preferred_element_type=F32) + b_ref[...]


def _modulation(cvecs, ada_w, ada_b):
    depth, d, nm = ada_w.shape
    r = cvecs.shape[0]
    tn = _pick(nm, 1024)
    return pl.pallas_call(
        _mod_kernel,
        out_shape=jax.ShapeDtypeStruct((depth, r, nm), F32),
        grid=(depth, nm // tn),
        in_specs=[pl.BlockSpec((r, d), lambda l, j: (0, 0)),
                  pl.BlockSpec((None, d, tn), lambda l, j: (l, 0, j)),
                  pl.BlockSpec((None, 1, tn), lambda l, j: (l, 0, j))],
        out_specs=pl.BlockSpec((None, r, tn), lambda l, j: (l, 0, j)),
        compiler_params=_cparams(("parallel", "parallel")),
        name="adaln_mod",
    )(cvecs, ada_w, ada_b.reshape(depth, 1, nm))


def _rope_kernel(ca_ref, sa_ref, cb_ref, sb_ref, *, log2w):
    s = ca_ref.shape[0]
    t = lax.broadcasted_iota(jnp.int32, (s, 1), 0)
    row = lax.shift_right_logical(t, log2w).astype(F32)
    col = (t & ((1 << log2w) - 1)).astype(F32)

    def tables(width, off, rd):
        lane = lax.broadcasted_iota(jnp.int32, (1, width), 1)
        r = lane - off
        inr = (r >= 0) & (r < rd)
        half, quarter = rd // 2, rd // 4
        is_col = r >= half
        rr = jnp.where(is_col, r - half, r)
        second = rr >= quarter
        j = jnp.where(second, rr - quarter, rr).astype(F32)
        inv = jnp.exp(j * (-2.0 / half * math.log(ROPE_THETA)))
        ang = jnp.where(is_col, col, row) * inv
        sign = jnp.where(second, 1.0, -1.0)
        return jnp.where(inr, jnp.cos(ang), 1.0), jnp.where(inr, sign * jnp.sin(ang), 0.0)

    ca, sa = tables(2 * LANE, MLA_NOPE, MLA_ROPE)
    cb, sb = tables(GQA_HEAD_DIM, 0, GQA_HEAD_DIM)
    ca_ref[...] = ca
    sa_ref[...] = sa
    cb_ref[...] = cb
    sb_ref[...] = sb


def _rope_tables(s):
    log2w = GRID_W.bit_length() - 1
    assert 1 << log2w == GRID_W
    shp = lambda w: jax.ShapeDtypeStruct((s, w), F32)
    return pl.pallas_call(
        functools.partial(_rope_kernel, log2w=log2w),
        out_shape=(shp(2 * LANE), shp(2 * LANE), shp(GQA_HEAD_DIM), shp(GQA_HEAD_DIM)),
        name="rope_tables",
    )()


def _s5_params(a_re, a_im, log_dt, b_re, b_im, c_re, c_im):
    depth, nd = a_re.shape[:2]
    r = depth * nd
    are = a_re.reshape(r, S5_NS)
    aim = a_im.reshape(r, S5_NS)
    ldt = jnp.broadcast_to(log_dt[..., None], (depth, nd, S5_GROUPS, S5_STATE)).reshape(r, S5_NS)
    rep = lambda b: jnp.repeat(jnp.transpose(b, (0, 3, 1, 2)).reshape(depth, 1, S5_GROUP, S5_NS), nd, axis=1)
    bre = rep(b_re).reshape(r, S5_GROUP, S5_NS)
    bim = rep(b_im).reshape(r, S5_GROUP, S5_NS)
    v = jax.ShapeDtypeStruct((r, S5_NS), F32)
    m = jax.ShapeDtypeStruct((r, S5_GROUP, S5_NS), F32)
    ar, ai, bbr, bbi = pl.pallas_call(_s5_param_rows_kernel, out_shape=(v, v, m, m), name="s5_discretise")(
        are, aim, ldt, bre, bim)
    eye = jnp.eye(S5_GROUPS, dtype=F32)

    def dense_b(bb):
        bb = bb.reshape(r, S5_GROUP, S5_GROUPS, S5_STATE)
        return jnp.einsum("dcgp,gh->dgchp", bb, eye).reshape(r, S5_CH, S5_NS)

    def dense_c(cc):
        return jnp.einsum("lgcp,gh->lhpgc", cc, eye).reshape(depth, S5_NS, S5_CH)

    bd = jnp.concatenate([dense_b(bbr), dense_b(bbi)], axis=-1).astype(BF16).reshape(depth, nd, S5_CH, 2 * S5_NS)
    cd = jnp.concatenate([dense_c(c_re), -dense_c(c_im)], axis=1).astype(BF16)
    abar = jnp.concatenate([ar, ai], axis=-1).reshape(depth, nd, 1, 2 * S5_NS)
    return abar, bd, cd


def _s5_param_rows_kernel(are_ref, aim_ref, ldt_ref, bre_ref, bim_ref, oar_ref, oai_ref, obr_ref, obi_ref):
    lr, li = are_ref[...], aim_ref[...]
    dt = jnp.exp(ldt_ref[...])
    mag = jnp.exp(lr * dt)
    ar, ai = mag * jnp.cos(li * dt), mag * jnp.sin(li * dt)
    oar_ref[...] = ar
    oai_ref[...] = ai
    nr, ni = ar - 1.0, ai
    den = lr * lr + li * li
    fr, fi = (nr * lr + ni * li) / den, (ni * lr - nr * li) / den
    for d in range(fr.shape[0]):
        br, bi = bre_ref[d], bim_ref[d]
        obr_ref[d] = fr[d:d + 1] * br - fi[d:d + 1] * bi
        obi_ref[d] = fr[d:d + 1] * bi + fi[d:d + 1] * br


def _ffn_kernel(x_ref, mod_ref, g_ref, w1_ref, w3_ref, w2_ref, o_ref, h_ref, *, row0, slab):
    j = pl.program_id(1)
    nslab = h_ref.shape[0] // slab
    nr = math.gcd(slab, NORM_ROWS)

    def partial_ffn(rows):
        h = h_ref[rows, :]
        gate = jnp.dot(h, w1_ref[...], preferred_element_type=F32)
        up = jnp.dot(h, w3_ref[...], preferred_element_type=F32)
        act = (_silu(gate) * up).astype(BF16)
        return jnp.dot(act, w2_ref[...], preferred_element_type=F32)

    @pl.when(j == 0)
    def _():
        for r in range(nslab):
            for p in range(slab // nr):
                rows = slice(r * slab + p * nr, r * slab + (p + 1) * nr)
                h = _modulate(x_ref[rows, :], g_ref[...], mod_ref[row0 + 1:row0 + 2, :], mod_ref[row0:row0 + 1, :])
                h_ref[rows, :] = h.astype(BF16)
            rows = slice(r * slab, (r + 1) * slab)
            o_ref[rows, :] = partial_ffn(rows)

    last = pl.num_programs(1) - 1

    @pl.when((j > 0) & (j < last))
    def _():
        for r in range(nslab):
            rows = slice(r * slab, (r + 1) * slab)
            o_ref[rows, :] += partial_ffn(rows)

    @pl.when(j == last)
    def _():
        gate = 0.5 * mod_ref[row0 + 2:row0 + 3, :]
        for r in range(nslab):
            rows = slice(r * slab, (r + 1) * slab)
            o_ref[rows, :] = x_ref[rows, :] + gate * (o_ref[rows, :] + partial_ffn(rows))


def _ffn(x, mod, pr, *, layer, which, seq, tm_pref=1024):
    n, d = x.shape
    groups = mod.shape[0]
    tm = _pick(seq if groups > 1 else n, tm_pref)
    tf = FFN_TILE
    nf = D_FF // tf
    assert nf >= 2
    per = seq // tm if groups > 1 else 1
    mod_map = (lambda i, j: (i // per, 0, 0)) if groups > 1 else (lambda i, j: (0, 0, 0))
    return pl.pallas_call(
        functools.partial(_ffn_kernel, row0=6 * which, slab=_pick(tm, 512)),
        out_shape=jax.ShapeDtypeStruct((n, d), F32),
        grid=(n // tm, nf),
        in_specs=[pl.BlockSpec((tm, d), lambda i, j: (i, 0)),
                  pl.BlockSpec((None, N_MOD, d), mod_map),
                  pl.BlockSpec((None, None, 1, d), lambda i, j: (layer, 2 * which, 0, 0)),
                  pl.BlockSpec((None, None, d, tf), lambda i, j: (layer, which, 0, j)),
                  pl.BlockSpec((None, None, d, tf), lambda i, j: (layer, which, 0, nf + j)),
                  pl.BlockSpec((None, None, tf, d), lambda i, j: (layer, which, j, 0))],
        out_specs=pl.BlockSpec((tm, d), lambda i, j: (i, 0)),
        scratch_shapes=[pltpu.VMEM((tm, d), BF16)],
        compiler_params=_cparams(("parallel", "arbitrary"), FFN_VMEM_LIMIT_BYTES),
        name="ffn",
    )(x, mod, pr["norm_g"], pr["ffn_w13"], pr["ffn_w13"], pr["ffn_w2"])


def _inproj_kernel(x_ref, mod_ref, g_ref, w_ref, o_ref):
    h = _modulate(x_ref[...], g_ref[...], mod_ref[4:5, :], mod_ref[3:4, :]).astype(BF16)
    o_ref[...] = jnp.dot(h, w_ref[...], preferred_element_type=F32)


def _inproj(x, mod, pr, *, layer, seq, tm_pref=512):
    n, d = x.shape
    groups = mod.shape[0]
    tm = _pick(seq if groups > 1 else n, tm_pref)
    per = seq // tm if groups > 1 else 1
    mod_map = (lambda i: (i // per, 0, 0)) if groups > 1 else (lambda i: (0, 0, 0))
    return pl.pallas_call(
        _inproj_kernel,
        out_shape=jax.ShapeDtypeStruct((n, PROJ_COLS), F32),
        grid=(n // tm,),
        in_specs=[pl.BlockSpec((tm, d), lambda i: (i, 0)),
                  pl.BlockSpec((None, N_MOD, d), mod_map),
                  pl.BlockSpec((None, None, 1, d), lambda i: (layer, 1, 0, 0)),
                  _layer_spec(pr["w_in"], layer, 1)],
        out_specs=pl.BlockSpec((tm, PROJ_COLS), lambda i: (i, 0)),
        compiler_params=_cparams(("parallel",)),
        name="in_proj",
    )(x, mod, pr["norm_g"], pr["w_in"])


def _mla_kv(ckv_n, kr128, wukv_ref, ka_ref, va_ref):
    kv = jnp.dot(ckv_n.astype(BF16), wukv_ref[...], preferred_element_type=F32)
    krb = kr128.astype(BF16)
    hw = MLA_NOPE + MLA_V
    for h in range(MLA_HEADS):
        ka_ref[:, h * 2 * LANE:h * 2 * LANE + MLA_NOPE] = kv[:, h * hw:h * hw + MLA_NOPE].astype(BF16)
        ka_ref[:, h * 2 * LANE + MLA_NOPE:(h + 1) * 2 * LANE] = krb
        va_ref[:, h * MLA_V:(h + 1) * MLA_V] = kv[:, h * hw + MLA_NOPE:(h + 1) * hw].astype(BF16)


def _attn_prep_kernel(*refs, rope):
    if rope:
        (cq_ref, ckv_ref, krmg_ref, gq_ref, gk_ref, gv_ref, ca_ref, sa_ref, cb_ref, sb_ref,
         qng_ref, kvng_ref, wuq_ref, wukv_ref, gqg_ref, gkg_ref,
         qa_ref, ka_ref, va_ref, ckvn_ref, qb_ref, kb_ref, vb_ref, kbn_ref) = refs
    else:
        (cq_ref, ckv_ref, krmg_ref, gq_ref, gk_ref, gv_ref,
         qng_ref, kvng_ref, wuq_ref, wukv_ref, gqg_ref, gkg_ref,
         qa_ref, ka_ref, va_ref, ckvn_ref, qb_ref, kb_ref, vb_ref, kbn_ref) = refs

    cqn = _rms(cq_ref[...], qng_ref[...]).astype(BF16)
    qa = jnp.dot(cqn, wuq_ref[...], preferred_element_type=F32)
    if rope:
        ca = jnp.concatenate([ca_ref[...]] * MLA_HEADS, axis=1)
        sa = jnp.concatenate([sa_ref[...]] * MLA_HEADS, axis=1)
        qa = qa * ca + _swap_halves(qa, MLA_ROPE // 4) * sa
    qa_ref[...] = (qa * ((MLA_NOPE + MLA_ROPE) ** -0.5 * LOG2E)).astype(BF16)

    ckv_n = _rms(ckv_ref[...], kvng_ref[...])
    ckvn_ref[...] = ckv_n
    krmg = krmg_ref[...]
    lane = lax.broadcasted_iota(jnp.int32, krmg.shape, 1)
    if rope:
        krmg = krmg * ca_ref[:, LANE:] + _swap_halves(krmg, MLA_ROPE // 4) * sa_ref[:, LANE:]
    kr128 = jnp.where(lane < MLA_ROPE, krmg, 0.0)
    _mla_kv(ckv_n, kr128, wukv_ref, ka_ref, va_ref)

    hd = GQA_HEAD_DIM
    gq, gk = gq_ref[...], gk_ref[...]
    for h in range(GQA_HEADS):
        q = _rms(gq[:, h * hd:(h + 1) * hd], gqg_ref[...])
        if rope:
            q = q * cb_ref[...] + _swap_halves(q, hd // 4) * sb_ref[...]
        qb_ref[:, h * hd:(h + 1) * hd] = (q * (hd ** -0.5 * LOG2E)).astype(BF16)
    for h in range(GQA_KV_HEADS):
        k = _rms(gk[:, h * hd:(h + 1) * hd], gkg_ref[...])
        kbn_ref[:, h * hd:(h + 1) * hd] = k
        if rope:
            k = k * cb_ref[...] + _swap_halves(k, hd // 4) * sb_ref[...]
        kb_ref[:, h * hd:(h + 1) * hd] = k.astype(BF16)
    vb_ref[...] = gv_ref[...].astype(BF16)


def _attn_prep(proj, tables, pr, *, layer, seq, tm_pref=1024):
    n = proj.shape[0]
    rope = tables is not None
    tm = _pick(seq if rope else n, tm_pref)
    per = max(1, seq // tm)

    def pblock(name, width):
        idx = _COL[name] // width
        assert idx * width == _COL[name]
        return pl.BlockSpec((tm, width), lambda i: (i, idx))

    in_specs = [pblock("cq", MLA_Q_LORA), pblock("ckv", MLA_KV_LORA),
                pl.BlockSpec((tm, LANE), lambda i: (i, KRMG_BLOCK)),
                pblock("gq", 512), pblock("gk", 256), pblock("gv", 256)]
    args = [proj] * 6
    if rope:
        in_specs += [pl.BlockSpec((tm, t.shape[1]), lambda i: (i % per, 0)) for t in tables]
        args += list(tables)
    ws = [pr["mla_q_norm"], pr["mla_kv_norm"], pr["mla_w_uq"], pr["mla_w_ukv"], pr["gqa_q_norm"], pr["gqa_k_norm"]]
    in_specs += [_layer_spec(w, layer, 1) for w in ws]
    args += ws
    widths = [(4 * 2 * LANE, BF16), (4 * 2 * LANE, BF16), (4 * MLA_V, BF16), (MLA_KV_LORA, F32),
              (GQA_HEADS * GQA_HEAD_DIM, BF16), (GQA_KV_HEADS * GQA_HEAD_DIM, BF16),
              (GQA_KV_HEADS * GQA_HEAD_DIM, BF16), (GQA_KV_HEADS * GQA_HEAD_DIM, F32)]
    return pl.pallas_call(
        functools.partial(_attn_prep_kernel, rope=rope),
        out_shape=tuple(jax.ShapeDtypeStruct((n, w), dt) for w, dt in widths),
        grid=(n // tm,),
        in_specs=in_specs,
        out_specs=tuple(pl.BlockSpec((tm, w), lambda i: (i, 0)) for w, _ in widths),
        compiler_params=_cparams(("parallel",)),
        name="attn_prep",
    )(*args)


def _cache_kv_kernel(ckv_ref, kr_ref, wukv_ref, ka_ref, va_ref):
    _mla_kv(ckv_ref[...], kr_ref[...], wukv_ref, ka_ref, va_ref)


def _cache_kv(ckv, kr128, wukv, *, layer):
    b, _, p, _ = ckv.shape
    return pl.pallas_call(
        _cache_kv_kernel,
        out_shape=(jax.ShapeDtypeStruct((b, p, 4 * 2 * LANE), BF16), jax.ShapeDtypeStruct((b, p, 4 * MLA_V), BF16)),
        grid=(b,),
        in_specs=[pl.BlockSpec((None, None, p, MLA_KV_LORA), lambda i: (i, layer, 0, 0)),
                  pl.BlockSpec((None, None, p, LANE), lambda i: (i, layer, 0, 0)),
                  _layer_spec(wukv, layer, 1)],
        out_specs=(pl.BlockSpec((None, p, 4 * 2 * LANE), lambda i: (i, 0, 0)),
                   pl.BlockSpec((None, p, 4 * MLA_V), lambda i: (i, 0, 0))),
        compiler_params=_cparams(("parallel",)),
        name="mla_cache_kv",
    )(ckv, kr128, wukv)


def _attn_kernel(q_ref, *refs, nseg, hp, g, dk, dv):
    o_ref = refs[-1]
    for j in range(hp):
        kv = j // g
        q = q_ref[:, j * dk:(j + 1) * dk]
        ks = [refs[2 * i][:, kv * dk:(kv + 1) * dk].astype(BF16) for i in range(nseg)]
        vs = [refs[2 * i + 1][:, kv * dv:(kv + 1) * dv].astype(BF16) for i in range(nseg)]
        ss = [lax.dot_general(q, k, (((1,), (1,)), ((), ())), preferred_element_type=F32) for k in ks]
        m = functools.reduce(jnp.maximum, [jnp.max(s, axis=-1, keepdims=True) for s in ss])
        ps = [jnp.exp2(s - m) for s in ss]
        l = functools.reduce(jnp.add, [jnp.sum(p, axis=-1, keepdims=True) for p in ps])
        o = functools.reduce(jnp.add, [jnp.dot(p.astype(BF16), v, preferred_element_type=F32)
                                       for p, v in zip(ps, vs)])
        o_ref[:, j * dv:(j + 1) * dv] = o / l


def _attention(q, segs, *, heads, kv_heads, dk, dv, tq_pref=512, hp=4):
    b, sq, _ = q.shape
    tq = _pick(sq, tq_pref)
    g = heads // kv_heads
    assert heads % hp == 0 and (hp % g == 0 or g % hp == 0)
    kvp = max(1, hp // g)
    kcol = (lambda hb: hb) if hp >= g else (lambda hb: hb * hp // g)
    in_specs = [pl.BlockSpec((None, tq, hp * dk), lambda bi, hb, i: (bi, i, hb))]
    args = [q]
    for k, v, layer in segs:
        sk = k.shape[-2]
        if layer is None:
            in_specs += [pl.BlockSpec((None, sk, kvp * dk), lambda bi, hb, i: (bi, 0, kcol(hb))),
                         pl.BlockSpec((None, sk, kvp * dv), lambda bi, hb, i: (bi, 0, kcol(hb)))]
        else:
            in_specs += [pl.BlockSpec((None, None, sk, kvp * dk),
                                      lambda bi, hb, i, layer=layer: (bi, layer, 0, kcol(hb))),
                         pl.BlockSpec((None, None, sk, kvp * dv),
                                      lambda bi, hb, i, layer=layer: (bi, layer, 0, kcol(hb)))]
        args += [k, v]
    return pl.pallas_call(
        functools.partial(_attn_kernel, nseg=len(segs), hp=hp, g=min(g, hp), dk=dk, dv=dv),
        out_shape=jax.ShapeDtypeStruct((b, sq, heads * dv), F32),
        grid=(b, heads // hp, sq // tq),
        in_specs=in_specs,
        out_specs=pl.BlockSpec((None, tq, hp * dv), lambda bi, hb, i: (bi, i, hb)),
        compiler_params=_cparams(("parallel", "parallel", "parallel")),
        name="attention",
    )(*args)


def _log_sigmoid(x):
    return jnp.minimum(x, 0.0) - jnp.log1p(jnp.exp(-jnp.abs(x)))


def _mlstm_chunk_operands(q_ref, k_ref, v_ref, gate_ref, bias_ref):
    nh, hd = MLSTM_HEADS, MLSTM_HEAD_DIM
    g = gate_ref[...] + bias_ref[...]
    lf = _log_sigmoid(g)
    lf_hi = lf.astype(BF16)
    rem = lf - lf_hi.astype(F32)
    lf_mid = rem.astype(BF16)
    lf_lo = (rem - lf_mid.astype(F32)).astype(BF16)
    heads = []
    for h in range(nh):
        q = q_ref[:, h * hd:(h + 1) * hd]
        kh = k_ref[:, h * hd:(h + 1) * hd] * (hd ** -0.5)
        kt = kh.T
        qh = q.astype(BF16)
        heads.append(dict(q=q, qh=qh, kh=kh, kt=kt, vh=v_ref[:, h * hd:(h + 1) * hd].astype(BF16),
                          s=jnp.dot(qh, kt.astype(BF16), preferred_element_type=F32)))
    return dict(g=g, gt=g.T, lf=(lf_hi, lf_mid, lf_lo), heads=heads)


def _mlstm_chain(ops, h_ref, c_s, n_s, m_s, *, direction):
    nh, hd = MLSTM_HEADS, MLSTM_HEAD_DIM
    rev = direction == 1
    g, gt = ops["g"], ops["gt"]
    cl = g.shape[0]
    ri = lax.broadcasted_iota(jnp.int32, (cl, cl), 0)
    ci = lax.broadcasted_iota(jnp.int32, (cl, cl), 1)
    mask = (ri <= ci) if rev else (ri >= ci)
    tri = jnp.where(mask, 1.0, 0.0).astype(BF16)
    cum = functools.reduce(jnp.add, [jnp.dot(tri, part, preferred_element_type=F32) for part in ops["lf"]])
    cumt = cum.T
    last = 0 if rev else cl - 1
    li0 = MG_LANE + direction * 2 * nh
    lf0 = li0 + nh

    for h in range(nh):
        hv = ops["heads"][h]
        cum_c = cum[:, lf0 + h:lf0 + h + 1]
        cum_r = cumt[lf0 + h:lf0 + h + 1, :]
        li_r = gt[li0 + h:li0 + h + 1, :]
        li_c = g[:, li0 + h:li0 + h + 1]
        m_prev = m_s[direction, h:h + 1, 0:1]
        dmat = jnp.where(mask, cum_c - cum_r + li_r, -jnp.inf)
        inter = cum_c + m_prev
        m_t = jnp.maximum(inter, jnp.max(dmat, axis=1, keepdims=True))
        sc = hv["s"] * jnp.exp(dmat - m_t)
        inter_w = jnp.exp(inter - m_t)
        cmat = c_s[direction, h]
        nrow = n_s[direction, h:h + 1, :]
        num = (jnp.dot(sc.astype(BF16), hv["vh"], preferred_element_type=F32)
               + inter_w * jnp.dot(hv["qh"], cmat.astype(BF16), preferred_element_type=F32))
        qn = jnp.sum(hv["q"] * nrow, axis=1, keepdims=True)
        den = jnp.sum(sc, axis=1, keepdims=True) + inter_w * qn
        h_ref[:, h * hd:(h + 1) * hd] = num / jnp.maximum(jnp.abs(den), jnp.exp(-m_t))

        tot = cum_c[last:last + 1, :]
        g_r = tot - cum_r + li_r
        g_c = tot - cum_c + li_c
        m_new = jnp.maximum(tot + m_prev, jnp.max(g_r, axis=1, keepdims=True))
        decay = jnp.exp(tot + m_prev - m_new)
        ws_r = jnp.exp(g_r - m_new)
        ws_c = jnp.exp(g_c - m_new)
        c_s[direction, h] = decay * cmat + jnp.dot((hv["kt"] * ws_r).astype(BF16), hv["vh"],
                                                   preferred_element_type=F32)
        n_s[direction, h:h + 1, :] = decay * nrow + jnp.sum(hv["kh"] * ws_c, axis=0, keepdims=True)
        m_s[direction, h:h + 1, :] = jnp.broadcast_to(m_new, (1, hd))


def _mlstm_kernel(*refs, shared, has_init, emit_state):
    nin = 4 if shared else 8
    fwd_in = refs[:4]
    bwd_in = fwd_in if shared else refs[4:8]
    bias_ref = refs[nin]
    pos = nin + 1
    if has_init:
        c0_ref, n0_ref, m0_ref = refs[pos:pos + 3]
        pos += 3
    hf_ref, hb_ref = refs[pos:pos + 2]
    pos += 2
    if emit_state:
        co_ref, no_ref, mo_ref = refs[pos:pos + 3]
        pos += 3
    c_s, n_s, m_s = refs[pos:pos + 3]
    c = pl.program_id(1)

    @pl.when(c == 0)
    def _():
        if has_init:
            c_s[...] = c0_ref[...]
            n_s[...] = n0_ref[...]
            m_s[...] = m0_ref[...]
        else:
            c_s[...] = jnp.zeros_like(c_s)
            n_s[...] = jnp.zeros_like(n_s)
            m_s[...] = jnp.zeros_like(m_s)

    ops_f = _mlstm_chunk_operands(*fwd_in, bias_ref)
    ops_b = ops_f if shared else _mlstm_chunk_operands(*bwd_in, bias_ref)
    _mlstm_chain(ops_f, hf_ref, c_s, n_s, m_s, direction=0)
    _mlstm_chain(ops_b, hb_ref, c_s, n_s, m_s, direction=1)

    if emit_state:
        @pl.when(c == pl.num_programs(1) - 1)
        def _():
            co_ref[...] = c_s[...]
            no_ref[...] = n_s[...]
            mo_ref[...] = m_s[...]


def _mlstm(proj, bias128, init, *, layer, batch, seq, emit_state, chunk_pref=256):
    n = proj.shape[0]
    cl = _pick(seq, chunk_pref)
    nc = seq // cl
    shared = nc == 1
    nh, hd = MLSTM_HEADS, MLSTM_HEAD_DIM
    rows = (lambda b, c: b * nc + c), (lambda b, c: b * nc + (nc - 1 - c))

    def chunk_specs(row):
        qkv = lambda name: pl.BlockSpec((cl, nh * hd), lambda b, c: (row(b, c), _COL[name] // (nh * hd)))
        return [qkv("mq"), qkv("mk"), qkv("mv"), pl.BlockSpec((cl, LANE), lambda b, c: (row(b, c), KRMG_BLOCK))]

    in_specs = chunk_specs(rows[0]) + ([] if shared else chunk_specs(rows[1]))
    in_specs.append(pl.BlockSpec((None, 1, LANE), lambda b, c: (layer, 0, 0)))
    args = [proj] * (len(in_specs) - 1) + [bias128]
    has_init = init is not None
    if has_init:
        c0, n0, m0 = init
        in_specs += [pl.BlockSpec((None, None, 2, nh, hd, hd), lambda b, c: (b, layer, 0, 0, 0, 0)),
                     pl.BlockSpec((None, None, 2, nh, hd), lambda b, c: (b, layer, 0, 0, 0)),
                     pl.BlockSpec((None, None, 2, nh, hd), lambda b, c: (b, layer, 0, 0, 0))]
        args += [c0, n0, m0]
    out_shape = [jax.ShapeDtypeStruct((n, nh * hd), F32)] * 2
    out_specs = [pl.BlockSpec((cl, nh * hd), lambda b, c, row=row: (row(b, c), 0)) for row in rows]
    if emit_state:
        out_shape += [jax.ShapeDtypeStruct((batch, 2, nh, hd, hd), F32), jax.ShapeDtypeStruct((batch, 2, nh, hd), F32),
                      jax.ShapeDtypeStruct((batch, 2, nh, hd), F32)]
        out_specs += [pl.BlockSpec((None, 2, nh, hd, hd), lambda b, c: (b, 0, 0, 0, 0)),
                      pl.BlockSpec((None, 2, nh, hd), lambda b, c: (b, 0, 0, 0)),
                      pl.BlockSpec((None, 2, nh, hd), lambda b, c: (b, 0, 0, 0))]
    return pl.pallas_call(
        functools.partial(_mlstm_kernel, shared=shared, has_init=has_init, emit_state=emit_state),
        out_shape=tuple(out_shape),
        grid=(batch, nc),
        in_specs=in_specs,
        out_specs=tuple(out_specs),
        scratch_shapes=[pltpu.VMEM((2, nh, hd, hd), F32), pltpu.VMEM((2, nh, hd), F32), pltpu.VMEM((2, nh, hd), F32)],
        compiler_params=_cparams(("parallel", "arbitrary")),
        name="mlstm",
    )(*args)


_S5_STATE_TILES = 2 * S5_NS // MXU_TILE
_S5_GROUPS_PER_TILE = MXU_TILE // S5_STATE


def _s5_channel_tile(state_tile):
    first_group = (state_tile % (S5_NS // MXU_TILE)) * _S5_GROUPS_PER_TILE
    return first_group * S5_GROUP // MXU_TILE


def _to_time_major_kernel(u_ref, p_ref, o_ref):
    rows = p_ref.shape[0]
    sg, ch = u_ref.shape[0], u_ref.shape[2]
    tsteps = rows // sg
    for k in range(u_ref.shape[1] // tsteps):
        u = u_ref[:, k * tsteps:(k + 1) * tsteps, :].reshape(rows, ch).astype(BF16)
        o_ref[k * rows:(k + 1) * rows, :] = jnp.dot(p_ref[...], u, preferred_element_type=F32).astype(BF16)


def _from_time_major_kernel(yf_ref, yb_ref, pt_ref, o_ref):
    rows = pt_ref.shape[0]
    sg, ch = o_ref.shape[0], o_ref.shape[2]
    tsteps = rows // sg
    pt = pt_ref[...]
    for k in range(o_ref.shape[1] // tsteps):
        y = yf_ref[k * rows:(k + 1) * rows, :] + yb_ref[k * rows:(k + 1) * rows, :]
        hi = y.astype(BF16)
        lo = (y - hi.astype(F32)).astype(BF16)
        out = jnp.dot(pt, hi, preferred_element_type=F32) + jnp.dot(pt, lo, preferred_element_type=F32)
        o_ref[:, k * tsteps:(k + 1) * tsteps, :] = out.reshape(sg, tsteps, ch)


def _s5_kernel(*refs, has_init):
    if has_init:
        u_ref, bd_ref, a_ref, cd_ref, x0_ref, y_ref, xf_ref, bu_s, x_s = refs
    else:
        u_ref, bd_ref, a_ref, cd_ref, y_ref, xf_ref, bu_s, x_s = refs
    d = pl.program_id(0)
    c = pl.program_id(2)
    ns = S5_NS
    sg = x_s.shape[0]
    tsteps = u_ref.shape[0] // sg
    mt = MXU_TILE

    @pl.when(c == 0)
    def _():
        if has_init:
            x_s[...] = x0_ref[...]
        else:
            x_s[...] = jnp.zeros_like(x_s)

    u = u_ref[...]
    for st in range(_S5_STATE_TILES):
        ct = _s5_channel_tile(st)
        bu_s[:, st * mt:(st + 1) * mt] = jnp.dot(u[:, ct * mt:(ct + 1) * mt],
                                                 bd_ref[ct * mt:(ct + 1) * mt, st * mt:(st + 1) * mt],
                                                 preferred_element_type=F32)
    a_re = jnp.broadcast_to(a_ref[:, :ns], (sg, ns))
    a_im = jnp.broadcast_to(a_ref[:, ns:], (sg, ns))

    def body(t, carry):
        xr, xi = carry
        tt = t + d * (tsteps - 1 - 2 * t)
        r0 = pl.multiple_of(tt * sg, sg)
        br = bu_s[pl.ds(r0, sg), :ns]
        bi = bu_s[pl.ds(r0, sg), ns:]
        nr = a_re * xr - a_im * xi + br
        ni = a_re * xi + a_im * xr + bi
        bu_s[pl.ds(r0, sg), :ns] = nr
        bu_s[pl.ds(r0, sg), ns:] = ni
        return nr, ni

    xr, xi = lax.fori_loop(0, tsteps, body, (x_s[:, :ns], x_s[:, ns:]), unroll=SCAN_UNROLL)
    x_s[:, :ns] = xr
    x_s[:, ns:] = xi

    for ct in range(S5_CH // mt):
        acc = None
        for st in range(_S5_STATE_TILES):
            if _s5_channel_tile(st) != ct:
                continue
            t = jnp.dot(bu_s[:, st * mt:(st + 1) * mt].astype(BF16),
                        cd_ref[st * mt:(st + 1) * mt, ct * mt:(ct + 1) * mt], preferred_element_type=F32)
            acc = t if acc is None else acc + t
        y_ref[:, ct * mt:(ct + 1) * mt] = acc

    @pl.when(c == pl.num_programs(2) - 1)
    def _():
        xf_ref[...] = x_s[...]


def _s5(proj3, pr, x0, *, layer, rows_pref=512, scan_rows_pref=1024):
    batch, seq, _ = proj3.shape
    sg = SUBLANE if batch % SUBLANE == 0 else batch
    ng = batch // sg
    tsteps = _pick(seq, max(SUBLANE, rows_pref // sg))
    rows = sg * tsteps
    nc = seq // tsteps
    r = jnp.arange(rows)
    perm = (jnp.arange(rows)[None, :] == ((r % sg) * tsteps + r // sg)[:, None]).astype(BF16)
    const2 = lambda a: pl.BlockSpec(a.shape, lambda g, c: (0,) * a.ndim, pipeline_mode=pl.Buffered(1))
    nsub = math.gcd(nc, RELAYOUT_SUB)
    nc_r = nc // nsub

    u_tm = pl.pallas_call(
        _to_time_major_kernel,
        out_shape=jax.ShapeDtypeStruct((ng, seq * sg, S5_CH), BF16),
        grid=(ng, nc_r),
        in_specs=[pl.BlockSpec((sg, nsub * tsteps, S5_CH), lambda g, c: (g, c, _COL["su"] // S5_CH)), const2(perm)],
        out_specs=pl.BlockSpec((None, nsub * rows, S5_CH), lambda g, c: (g, c, 0)),
        compiler_params=_cparams(("parallel", "parallel")),
        name="s5_to_time_major",
    )(proj3, perm)

    ts_scan = _pick(seq, max(SUBLANE, scan_rows_pref // sg))
    rows_scan = sg * ts_scan
    nc_scan = seq // ts_scan
    cpos = lambda d, c: c + d * (nc_scan - 1 - 2 * c)
    has_init = x0 is not None
    in_specs = [pl.BlockSpec((None, rows_scan, S5_CH), lambda d, g, c: (g, cpos(d, c), 0)),
                pl.BlockSpec((None, None, S5_CH, 2 * S5_NS), lambda d, g, c: (layer, d, 0, 0)),
                pl.BlockSpec((None, None, 1, 2 * S5_NS), lambda d, g, c: (layer, d, 0, 0)),
                _layer_spec(pr["s5_cd"], layer, 3)]
    args = [u_tm, pr["s5_bd"], pr["s5_abar"], pr["s5_cd"]]
    if has_init:
        in_specs.append(pl.BlockSpec((None, sg, 2 * S5_NS), lambda d, g, c: (d, g, 0)))
        args.append(x0)
    y_tm, xfin = pl.pallas_call(
        functools.partial(_s5_kernel, has_init=has_init),
        out_shape=(jax.ShapeDtypeStruct((2, ng, seq * sg, S5_CH), F32),
                   jax.ShapeDtypeStruct((2, batch, 2 * S5_NS), F32)),
        grid=(2, ng, nc_scan),
        in_specs=in_specs,
        out_specs=(pl.BlockSpec((None, None, rows_scan, S5_CH), lambda d, g, c: (d, g, cpos(d, c), 0)),
                   pl.BlockSpec((None, sg, 2 * S5_NS), lambda d, g, c: (d, g, 0))),
        scratch_shapes=[pltpu.VMEM((rows_scan, 2 * S5_NS), F32), pltpu.VMEM((sg, 2 * S5_NS), F32)],
        compiler_params=_cparams(("parallel", "parallel", "arbitrary")),
        name="s5_scan",
    )(*args)

    y = pl.pallas_call(
        _from_time_major_kernel,
        out_shape=jax.ShapeDtypeStruct((batch, seq, S5_CH), F32),
        grid=(ng, nc_r),
        in_specs=[pl.BlockSpec((None, None, nsub * rows, S5_CH), lambda g, c: (0, g, c, 0)),
                  pl.BlockSpec((None, None, nsub * rows, S5_CH), lambda g, c: (1, g, c, 0)),
                  const2(perm)],
        out_specs=pl.BlockSpec((sg, nsub * tsteps, S5_CH), lambda g, c: (g, c, 0)),
        compiler_params=_cparams(("parallel", "parallel")),
        name="s5_from_time_major",
    )(y_tm, y_tm, perm.T)
    return y, xfin


def _gelu_tanh(x):
    return 0.5 * x * (1.0 + jnp.tanh(math.sqrt(2.0 / math.pi) * (x + 0.044715 * (x * x * x))))


def _outproj_kernel(x_ref, mod_ref, oa_ref, ob_ref, hf_ref, hb_ref, mo_ref, y_ref, su_ref,
                    d_ref, wglu_ref, on_ref, wout_ref, o_ref):
    slab = math.gcd(x_ref.shape[0], OUTPROJ_SLAB)
    for r in range(x_ref.shape[0] // slab):
        rows = slice(r * slab, (r + 1) * slab)
        oc = jax.nn.sigmoid(mo_ref[rows, :]) * (hf_ref[rows, :] + hb_ref[rows, :])
        y = _gelu_tanh(y_ref[rows, :] + d_ref[...] * su_ref[rows, :])
        od = y * jax.nn.sigmoid(jnp.dot(y.astype(BF16), wglu_ref[...], preferred_element_type=F32))
        acc = None
        for gi, part in enumerate((oa_ref[rows, :], ob_ref[rows, :], oc, od)):
            nrm = _rms(part, on_ref[gi:gi + 1, :]).astype(BF16)
            t = jnp.dot(nrm, wout_ref[gi * GROUP_WIDTH:(gi + 1) * GROUP_WIDTH, :], preferred_element_type=F32)
            acc = t if acc is None else acc + t
        o_ref[rows, :] = x_ref[rows, :] + mod_ref[5:6, :] * acc


def _outproj(x, mod, oa, ob, hf, hb, proj, y, pr, *, layer, seq, tm_pref=512):
    n, d = x.shape
    groups = mod.shape[0]
    tm = _pick(seq if groups > 1 else n, tm_pref)
    per = seq // tm if groups > 1 else 1
    mod_map = (lambda i: (i // per, 0, 0)) if groups > 1 else (lambda i: (0, 0, 0))
    gw = GROUP_WIDTH
    rowblk = pl.BlockSpec((tm, gw), lambda i: (i, 0))
    ws = [pr["s5_d"], pr["s5_w_glu"], pr["out_norm"], pr["w_out"]]
    return pl.pallas_call(
        _outproj_kernel,
        out_shape=jax.ShapeDtypeStruct((n, d), F32),
        grid=(n // tm,),
        in_specs=[pl.BlockSpec((tm, d), lambda i: (i, 0)),
                  pl.BlockSpec((None, N_MOD, d), mod_map),
                  rowblk, rowblk, rowblk, rowblk,
                  pl.BlockSpec((tm, gw), lambda i: (i, _COL["mo"] // gw)),
                  rowblk,
                  pl.BlockSpec((tm, gw), lambda i: (i, _COL["su"] // gw))] + [_layer_spec(w, layer, 1) for w in ws],
        out_specs=pl.BlockSpec((tm, d), lambda i: (i, 0)),
        compiler_params=_cparams(("parallel",)),
        name="merge_out_proj",
    )(x, mod, oa, ob, hf, hb, proj, y, proj, *ws)


def _final_norm_kernel(x_ref, g_ref, o_ref):
    o_ref[...] = _rms(x_ref[...], g_ref[...])


def _final_norm(x, g, tm_pref=1024):
    n, d = x.shape
    tm = _pick(n, tm_pref)
    return pl.pallas_call(
        _final_norm_kernel,
        out_shape=jax.ShapeDtypeStruct((n, d), F32),
        grid=(n // tm,),
        in_specs=[pl.BlockSpec((tm, d), lambda i: (i, 0)), pl.BlockSpec((1, d), lambda i: (0, 0))],
        out_specs=pl.BlockSpec((tm, d), lambda i: (i, 0)),
        compiler_params=_cparams(("parallel",)),
        name="final_norm",
    )(x, g)


def _trunk_layer(x, mod, pr, *, layer, batch, seq, tables, cache):
    latent = cache is not None
    x = _ffn(x, mod, pr, layer=layer, which=0, seq=seq)
    proj = _inproj(x, mod, pr, layer=layer, seq=seq)

    qa, ka, va, ckvn, qb, kb, vb, kbn = _attn_prep(proj, tables, pr, layer=layer, seq=seq)
    r3 = lambda a: a.reshape(batch, seq, a.shape[-1])
    segs_a = [(r3(ka), r3(va), None)]
    segs_b = [(r3(kb), r3(vb), None)]
    if latent:
        segs_a.insert(0, cache["mla_kv"] + (None,))
        segs_b.insert(0, cache["gqa_kv"] + (layer,))
    oa = _attention(r3(qa), segs_a, heads=MLA_HEADS, kv_heads=MLA_HEADS, dk=2 * LANE, dv=MLA_V)
    ob = _attention(r3(qb), segs_b, heads=GQA_HEADS, kv_heads=GQA_KV_HEADS, dk=GQA_HEAD_DIM, dv=GQA_HEAD_DIM)
    oa = oa.reshape(batch * seq, -1)
    ob = ob.reshape(batch * seq, -1)

    m_init = cache["mlstm"] if latent else None
    mres = _mlstm(proj, pr["mlstm_bias"], m_init, layer=layer, batch=batch, seq=seq, emit_state=not latent)

    y, xfin = _s5(proj.reshape(batch, seq, PROJ_COLS), pr, cache["s5"] if latent else None, layer=layer)

    x = _outproj(x, mod, oa, ob, mres[0], mres[1], proj, y.reshape(batch * seq, S5_CH), pr,
                 layer=layer, seq=seq)
    x = _ffn(x, mod, pr, layer=layer, which=1, seq=seq)

    new_ctx = None
    if not latent:
        kr = proj[:, _COL["kr"]:_COL["kr"] + MLA_ROPE]
        gv = proj[:, _COL["gv"]:_COL["gv"] + GQA_KV_HEADS * GQA_HEAD_DIM]
        xs = xfin.reshape(2, batch, 2, S5_GROUPS, S5_STATE).transpose(1, 0, 2, 3, 4)
        new_ctx = (ckvn.reshape(batch, seq, MLA_KV_LORA),
                   kr.reshape(batch, seq, MLA_ROPE),
                   kbn.reshape(batch, seq, GQA_KV_HEADS, GQA_HEAD_DIM),
                   gv.reshape(batch, seq, GQA_KV_HEADS, GQA_HEAD_DIM),
                   mres[2], mres[3], mres[4][..., 0],
                   xs[:, :, 0], xs[:, :, 1])
    return x, new_ctx


def _permute_w_in(w_in):
    parts = [w_in[..., _ORIG[n][0]:_ORIG[n][0] + _ORIG[n][1]] for n in _ORDER]
    pad = PROJ_COLS - sum(p.shape[-1] for p in parts)
    parts.append(jnp.zeros(w_in.shape[:-1] + (pad,), w_in.dtype))
    return jnp.concatenate(parts, axis=-1).astype(BF16)


def _permute_w_uq(w_uq):
    depth, k, _ = w_uq.shape
    w = w_uq.reshape(depth, k, MLA_HEADS, MLA_NOPE + MLA_ROPE)
    w = jnp.pad(w, ((0, 0), (0, 0), (0, 0), (0, 2 * LANE - MLA_NOPE - MLA_ROPE)))
    return w.reshape(depth, k, MLA_HEADS * 2 * LANE).astype(BF16)


def kernel(x_prompt, x_sample, cache_mla_ckv, cache_mla_krope, cache_gqa_k, cache_gqa_v, state_mlstm_c, state_mlstm_n, state_mlstm_m, state_s5_re, state_s5_im, c, c_ctx, ada_w, ada_b, norm_g, ffn_w13, ffn_w2, w_in, mla_q_norm, mla_kv_norm, mla_w_uq, mla_w_ukv, gqa_q_norm, gqa_k_norm, mlstm_gate_b, s5_a_re, s5_a_im, s5_log_dt, s5_b_re, s5_b_im, s5_c_re, s5_c_im, s5_d, s5_w_glu, out_norm, w_out, final_norm):
    bc, sc, d = x_prompt.shape
    bl, sl, _ = x_sample.shape
    depth = ada_w.shape[0]
    past = cache_mla_ckv.shape[2]

    rows = ((1 + bl + SUBLANE - 1) // SUBLANE) * SUBLANE
    cvecs = jnp.concatenate([c_ctx[None, :], c, jnp.zeros((rows - 1 - bl, d), F32)], axis=0)
    mod_all = _modulation(cvecs, ada_w, ada_b).reshape(depth, rows, N_MOD, d)

    tables = _rope_tables(sl)

    abar, bd, cd = _s5_params(s5_a_re, s5_a_im, s5_log_dt, s5_b_re, s5_b_im, s5_c_re, s5_c_im)
    nmg = MLSTM_HEADS * 4
    row = lambda a: a.reshape(depth, 1, a.shape[-1])
    pr = {"norm_g": norm_g.reshape(depth, 3, 1, d),
          "ffn_w13": ffn_w13.astype(BF16), "ffn_w2": ffn_w2.astype(BF16),
          "w_in": _permute_w_in(w_in),
          "mla_q_norm": row(mla_q_norm), "mla_kv_norm": row(mla_kv_norm),
          "mla_w_uq": _permute_w_uq(mla_w_uq), "mla_w_ukv": mla_w_ukv.astype(BF16),
          "gqa_q_norm": row(gqa_q_norm), "gqa_k_norm": row(gqa_k_norm),
          "mlstm_bias": jnp.pad(mlstm_gate_b.reshape(depth, 1, nmg), ((0, 0), (0, 0), (MG_LANE, LANE - MG_LANE - nmg))),
          "s5_abar": abar, "s5_bd": bd, "s5_cd": cd,
          "s5_d": row(s5_d), "s5_w_glu": s5_w_glu.astype(BF16),
          "out_norm": out_norm.reshape(depth, 4, GROUP_WIDTH), "w_out": w_out.astype(BF16)}

    gkv = GQA_KV_HEADS * GQA_HEAD_DIM
    cache_k = cache_gqa_k.reshape(bl, depth, past, gkv)
    cache_v = cache_gqa_v.reshape(bl, depth, past, gkv)
    m0 = jnp.broadcast_to(state_mlstm_m[..., None], state_mlstm_n.shape)
    kr_pad = jnp.pad(cache_mla_krope, ((0, 0), (0, 0), (0, 0), (0, LANE - MLA_ROPE)))

    x_ctx = x_prompt.reshape(bc * sc, d)
    x_lat = x_sample.reshape(bl * sl, d)
    per_layer = []
    for l in range(depth):
        x_ctx, ctx_l = _trunk_layer(x_ctx, mod_all[l, 0:1], pr, layer=l, batch=bc, seq=sc, tables=None, cache=None)
        per_layer.append(ctx_l)

        x0 = jnp.concatenate([state_s5_re[:, l].reshape(bl, 2, S5_NS), state_s5_im[:, l].reshape(bl, 2, S5_NS)],
                             axis=-1).transpose(1, 0, 2)
        cache = {"mla_kv": tuple(_cache_kv(cache_mla_ckv, kr_pad, pr["mla_w_ukv"], layer=l)),
                 "gqa_kv": (cache_k, cache_v),
                 "mlstm": (state_mlstm_c, state_mlstm_n, m0),
                 "s5": x0}
        x_lat, _ = _trunk_layer(x_lat, mod_all[l, 1:1 + bl], pr, layer=l, batch=bl, seq=sl, tables=tables,
                                cache=cache)

    new_ctx = [jnp.stack([t[i] for t in per_layer], axis=1) for i in range(9)]
    y_prompt = _final_norm(x_ctx, final_norm[None, :]).reshape(bc, sc, d)
    y_sample = _final_norm(x_lat, final_norm[None, :]).reshape(bl, sl, d)
    return (y_prompt, y_sample, *new_ctx)
```

```python
import functools
import math

import jax
import jax.numpy as jnp
from jax import lax
from jax.experimental import pallas as pl
from jax.experimental.pallas import tpu as pltpu

F32 = jnp.float32
BF16 = jnp.bfloat16

EPS = 1e-6
ROPE_THETA = 10000.0
GRID_W = 64
N_MOD = 9
LOG2E = math.log2(math.e)

D_FF = 5632
MLA_HEADS, MLA_NOPE, MLA_ROPE, MLA_V = 4, 128, 64, 128
MLA_Q_LORA, MLA_KV_LORA = 384, 256
GQA_HEADS, GQA_KV_HEADS, GQA_HEAD_DIM = 4, 2, 128
MLSTM_HEADS, MLSTM_HEAD_DIM = 4, 128
S5_GROUPS, S5_GROUP, S5_STATE = 32, 16, 64
S5_CH = S5_GROUPS * S5_GROUP
S5_NS = S5_GROUPS * S5_STATE
GROUP_WIDTH = 512

LANE = 128
SUBLANE = 8
MXU_TILE = 256
NORM_ROWS = 128
FFN_TILE = 512
RELAYOUT_SUB = 4
OUTPROJ_SLAB = 256
SCAN_UNROLL = 4
VMEM_BYTES = 64 * 1024 * 1024
VMEM_LIMIT_BYTES = VMEM_BYTES - 8 * 1024 * 1024
FFN_VMEM_LIMIT_BYTES = VMEM_BYTES - 4 * 1024 * 1024

_ORIG = dict(cq=(0, 384), ckv=(384, 256), kr=(640, 64), gq=(704, 512), gk=(1216, 256), gv=(1472, 256),
             mq=(1728, 512), mk=(2240, 512), mv=(2752, 512), mo=(3264, 512), mg=(3776, 16), su=(3792, 512))
_ORDER = ("mq", "mk", "mv", "mo", "gq", "su", "gk", "gv", "ckv", "cq", "kr", "mg")
_COL = {}
_off = 0
for _n in _ORDER:
    _COL[_n] = _off
    _off += _ORIG[_n][1]
PROJ_COLS = ((_off + LANE - 1) // LANE) * LANE
KRMG_BLOCK = _COL["kr"] // LANE
MG_LANE = _COL["mg"] - _COL["kr"]


def _cparams(sem, vmem_limit=VMEM_LIMIT_BYTES):
    return pltpu.CompilerParams(dimension_semantics=sem, vmem_limit_bytes=vmem_limit)


def _pick(n, pref):
    for t in range(min(n, pref), 0, -1):
        if n % t == 0 and (t % SUBLANE == 0 or t == n):
            return t
    return n


def _layer_spec(arr, layer, nargs):
    zeros = (0,) * (arr.ndim - 1)
    imap = {1: lambda i: (layer,) + zeros, 2: lambda i, j: (layer,) + zeros,
            3: lambda i, j, k: (layer,) + zeros}[nargs]
    return pl.BlockSpec((None,) + arr.shape[1:], imap, pipeline_mode=pl.Buffered(1))


def _rms(x, g):
    return x * lax.rsqrt(jnp.mean(x * x, axis=-1, keepdims=True) + EPS) * g


def _modulate(x, g, scale, shift):
    return x * lax.rsqrt(jnp.mean(x * x, axis=-1, keepdims=True) + EPS) * (g * (1.0 + scale)) + shift


def _swap_halves(x, half):
    w = x.shape[-1]
    lane = lax.broadcasted_iota(jnp.int32, x.shape, x.ndim - 1)
    first = (lane & (2 * half - 1)) < half
    return jnp.where(first, pltpu.roll(x, w - half, axis=x.ndim - 1), pltpu.roll(x, half, axis=x.ndim - 1))


def _silu(x):
    return x * jax.nn.sigmoid(x)


def _mod_kernel(c_ref, w_ref, b_ref, o_ref):
    a = _silu(c_ref[...]).astype(BF16)
    o_ref[...] = jnp.dot(a, w_ref[...].astype(BF16), preferred_element_type=F32) + b_ref[...]


def _modulation(cvecs, ada_w, ada_b):
    depth, d, nm = ada_w.shape
    r = cvecs.shape[0]
    tn = _pick(nm, 1024)
    return pl.pallas_call(
        _mod_kernel,
        out_shape=jax.ShapeDtypeStruct((depth, r, nm), F32),
        grid=(depth, nm // tn),
        in_specs=[pl.BlockSpec((r, d), lambda l, j: (0, 0)),
                  pl.BlockSpec((None, d, tn), lambda l, j: (l, 0, j)),
                  pl.BlockSpec((None, 1, tn), lambda l, j: (l, 0, j))],
        out_specs=pl.BlockSpec((None, r, tn), lambda l, j: (l, 0, j)),
        compiler_params=_cparams(("parallel", "parallel")),
        name="adaln_mod",
    )(cvecs, ada_w, ada_b.reshape(depth, 1, nm))


def _rope_kernel(ca_ref, sa_ref, cb_ref, sb_ref, *, log2w):
    s = ca_ref.shape[0]
    t = lax.broadcasted_iota(jnp.int32, (s, 1), 0)
    row = lax.shift_right_logical(t, log2w).astype(F32)
    col = (t & ((1 << log2w) - 1)).astype(F32)

    def tables(width, off, rd):
        lane = lax.broadcasted_iota(jnp.int32, (1, width), 1)
        r = lane - off
        inr = (r >= 0) & (r < rd)
        half, quarter = rd // 2, rd // 4
        is_col = r >= half
        rr = jnp.where(is_col, r - half, r)
        second = rr >= quarter
        j = jnp.where(second, rr - quarter, rr).astype(F32)
        inv = jnp.exp(j * (-2.0 / half * math.log(ROPE_THETA)))
        ang = jnp.where(is_col, col, row) * inv
        sign = jnp.where(second, 1.0, -1.0)
        return jnp.where(inr, jnp.cos(ang), 1.0), jnp.where(inr, sign * jnp.sin(ang), 0.0)

    ca, sa = tables(2 * LANE, MLA_NOPE, MLA_ROPE)
    cb, sb = tables(GQA_HEAD_DIM, 0, GQA_HEAD_DIM)
    ca_ref[...] = ca
    sa_ref[...] = sa
    cb_ref[...] = cb
    sb_ref[...] = sb


def _rope_tables(s):
    log2w = GRID_W.bit_length() - 1
    assert 1 << log2w == GRID_W
    shp = lambda w: jax.ShapeDtypeStruct((s, w), F32)
    return pl.pallas_call(
        functools.partial(_rope_kernel, log2w=log2w),
        out_shape=(shp(2 * LANE), shp(2 * LANE), shp(GQA_HEAD_DIM), shp(GQA_HEAD_DIM)),
        name="rope_tables",
    )()


def _s5_params(a_re, a_im, log_dt, b_re, b_im, c_re, c_im):
    depth, nd = a_re.shape[:2]
    r = depth * nd
    are = a_re.reshape(r, S5_NS)
    aim = a_im.reshape(r, S5_NS)
    ldt = jnp.broadcast_to(log_dt[..., None], (depth, nd, S5_GROUPS, S5_STATE)).reshape(r, S5_NS)
    rep = lambda b: jnp.repeat(jnp.transpose(b, (0, 3, 1, 2)).reshape(depth, 1, S5_GROUP, S5_NS), nd, axis=1)
    bre = rep(b_re).reshape(r, S5_GROUP, S5_NS)
    bim = rep(b_im).reshape(r, S5_GROUP, S5_NS)
    v = jax.ShapeDtypeStruct((r, S5_NS), F32)
    m = jax.ShapeDtypeStruct((r, S5_GROUP, S5_NS), F32)
    ar, ai, bbr, bbi = pl.pallas_call(_s5_param_rows_kernel, out_shape=(v, v, m, m), name="s5_discretise")(
        are, aim, ldt, bre, bim)
    eye = jnp.eye(S5_GROUPS, dtype=F32)

    def dense_b(bb):
        bb = bb.reshape(r, S5_GROUP, S5_GROUPS, S5_STATE)
        return jnp.einsum("dcgp,gh->dgchp", bb, eye).reshape(r, S5_CH, S5_NS)

    def dense_c(cc):
        return jnp.einsum("lgcp,gh->lhpgc", cc, eye).reshape(depth, S5_NS, S5_CH)

    bd = jnp.concatenate([dense_b(bbr), dense_b(bbi)], axis=-1).astype(BF16).reshape(depth, nd, S5_CH, 2 * S5_NS)
    cd = jnp.concatenate([dense_c(c_re), -dense_c(c_im)], axis=1).astype(BF16)
    abar = jnp.concatenate([ar, ai], axis=-1).reshape(depth, nd, 1, 2 * S5_NS)
    return abar, bd, cd


def _s5_param_rows_kernel(are_ref, aim_ref, ldt_ref, bre_ref, bim_ref, oar_ref, oai_ref, obr_ref, obi_ref):
    lr, li = are_ref[...], aim_ref[...]
    dt = jnp.exp(ldt_ref[...])
    mag = jnp.exp(lr * dt)
    ar, ai = mag * jnp.cos(li * dt), mag * jnp.sin(li * dt)
    oar_ref[...] = ar
    oai_ref[...] = ai
    nr, ni = ar - 1.0, ai
    den = lr * lr + li * li
    fr, fi = (nr * lr + ni * li) / den, (ni * lr - nr * li) / den
    for d in range(fr.shape[0]):
        br, bi = bre_ref[d], bim_ref[d]
        obr_ref[d] = fr[d:d + 1] * br - fi[d:d + 1] * bi
        obi_ref[d] = fr[d:d + 1] * bi + fi[d:d + 1] * br


def _ffn_kernel(x_ref, mod_ref, g_ref, w1_ref, w3_ref, w2_ref, o_ref, h_ref, *, row0, slab):
    j = pl.program_id(1)
    nslab = h_ref.shape[0] // slab
    nr = math.gcd(slab, NORM_ROWS)

    def partial_ffn(rows):
        h = h_ref[rows, :]
        gate = jnp.dot(h, w1_ref[...], preferred_element_type=F32)
        up = jnp.dot(h, w3_ref[...], preferred_element_type=F32)
        act = (_silu(gate) * up).astype(BF16)
        return jnp.dot(act, w2_ref[...], preferred_element_type=F32)

    @pl.when(j == 0)
    def _():
        for r in range(nslab):
            for p in range(slab // nr):
                rows = slice(r * slab + p * nr, r * slab + (p + 1) * nr)
                h = _modulate(x_ref[rows, :], g_ref[...], mod_ref[row0 + 1:row0 + 2, :], mod_ref[row0:row0 + 1, :])
                h_ref[rows, :] = h.astype(BF16)
            rows = slice(r * slab, (r + 1) * slab)
            o_ref[rows, :] = partial_ffn(rows)

    last = pl.num_programs(1) - 1

    @pl.when((j > 0) & (j < last))
    def _():
        for r in range(nslab):
            rows = slice(r * slab, (r + 1) * slab)
            o_ref[rows, :] += partial_ffn(rows)

    @pl.when(j == last)
    def _():
        gate = 0.5 * mod_ref[row0 + 2:row0 + 3, :]
        for r in range(nslab):
            rows = slice(r * slab, (r + 1) * slab)
            o_ref[rows, :] = x_ref[rows, :] + gate * (o_ref[rows, :] + partial_ffn(rows))


def _ffn(x, mod, pr, *, layer, which, seq, tm_pref=1024):
    n, d = x.shape
    groups = mod.shape[0]
    tm = _pick(seq if groups > 1 else n, tm_pref)
    tf = FFN_TILE
    nf = D_FF // tf
    assert nf >= 2
    per = seq // tm if groups > 1 else 1
    mod_map = (lambda i, j: (i // per, 0, 0)) if groups > 1 else (lambda i, j: (0, 0, 0))
    return pl.pallas_call(
        functools.partial(_ffn_kernel, row0=6 * which, slab=_pick(tm, 512)),
        out_shape=jax.ShapeDtypeStruct((n, d), F32),
        grid=(n // tm, nf),
        in_specs=[pl.BlockSpec((tm, d), lambda i, j: (i, 0)),
                  pl.BlockSpec((None, N_MOD, d), mod_map),
                  pl.BlockSpec((None, None, 1, d), lambda i, j: (layer, 2 * which, 0, 0)),
                  pl.BlockSpec((None, None, d, tf), lambda i, j: (layer, which, 0, j)),
                  pl.BlockSpec((None, None, d, tf), lambda i, j: (layer, which, 0, nf + j)),
                  pl.BlockSpec((None, None, tf, d), lambda i, j: (layer, which, j, 0))],
        out_specs=pl.BlockSpec((tm, d), lambda i, j: (i, 0)),
        scratch_shapes=[pltpu.VMEM((tm, d), BF16)],
        compiler_params=_cparams(("parallel", "arbitrary"), FFN_VMEM_LIMIT_BYTES),
        name="ffn",
    )(x, mod, pr["norm_g"], pr["ffn_w13"], pr["ffn_w13"], pr["ffn_w2"])


def _inproj_kernel(x_ref, mod_ref, g_ref, w_ref, o_ref):
    h = _modulate(x_ref[...], g_ref[...], mod_ref[4:5, :], mod_ref[3:4, :]).astype(BF16)
    o_ref[...] = jnp.dot(h, w_ref[...], preferred_element_type=F32)


def _inproj(x, mod, pr, *, layer, seq, tm_pref=512):
    n, d = x.shape
    groups = mod.shape[0]
    tm = _pick(seq if groups > 1 else n, tm_pref)
    per = seq // tm if groups > 1 else 1
    mod_map = (lambda i: (i // per, 0, 0)) if groups > 1 else (lambda i: (0, 0, 0))
    return pl.pallas_call(
        _inproj_kernel,
        out_shape=jax.ShapeDtypeStruct((n, PROJ_COLS), F32),
        grid=(n // tm,),
        in_specs=[pl.BlockSpec((tm, d), lambda i: (i, 0)),
                  pl.BlockSpec((None, N_MOD, d), mod_map),
                  pl.BlockSpec((None, None, 1, d), lambda i: (layer, 1, 0, 0)),
                  _layer_spec(pr["w_in"], layer, 1)],
        out_specs=pl.BlockSpec((tm, PROJ_COLS), lambda i: (i, 0)),
        compiler_params=_cparams(("parallel",)),
        name="in_proj",
    )(x, mod, pr["norm_g"], pr["w_in"])


def _mla_kv(ckv_n, kr128, wukv_ref, ka_ref, va_ref):
    kv = jnp.dot(ckv_n.astype(BF16), wukv_ref[...], preferred_element_type=F32)
    krb = kr128.astype(BF16)
    hw = MLA_NOPE + MLA_V
    for h in range(MLA_HEADS):
        ka_ref[:, h * 2 * LANE:h * 2 * LANE + MLA_NOPE] = kv[:, h * hw:h * hw + MLA_NOPE].astype(BF16)
        ka_ref[:, h * 2 * LANE + MLA_NOPE:(h + 1) * 2 * LANE] = krb
        va_ref[:, h * MLA_V:(h + 1) * MLA_V] = kv[:, h * hw + MLA_NOPE:(h + 1) * hw].astype(BF16)


def _attn_prep_kernel(*refs, rope):
    if rope:
        (cq_ref, ckv_ref, krmg_ref, gq_ref, gk_ref, gv_ref, ca_ref, sa_ref, cb_ref, sb_ref,
         qng_ref, kvng_ref, wuq_ref, wukv_ref, gqg_ref, gkg_ref,
         qa_ref, ka_ref, va_ref, qb_ref, kb_ref, vb_ref) = refs
    else:
        (cq_ref, ckv_ref, krmg_ref, gq_ref, gk_ref, gv_ref,
         qng_ref, kvng_ref, wuq_ref, wukv_ref, gqg_ref, gkg_ref,
         qa_ref, ka_ref, va_ref, qb_ref, kb_ref, vb_ref, ckvn_ref, kbn_ref) = refs

    cqn = _rms(cq_ref[...], qng_ref[...]).astype(BF16)
    qa = jnp.dot(cqn, wuq_ref[...], preferred_element_type=F32)
    if rope:
        ca = jnp.concatenate([ca_ref[...]] * MLA_HEADS, axis=1)
        sa = jnp.concatenate([sa_ref[...]] * MLA_HEADS, axis=1)
        qa = qa * ca + _swap_halves(qa, MLA_ROPE // 4) * sa
    qa_ref[...] = (qa * ((MLA_NOPE + MLA_ROPE) ** -0.5 * LOG2E)).astype(BF16)

    ckv_n = _rms(ckv_ref[...], kvng_ref[...])
    if not rope:
        ckvn_ref[...] = ckv_n
    krmg = krmg_ref[...]
    lane = lax.broadcasted_iota(jnp.int32, krmg.shape, 1)
    if rope:
        krmg = krmg * ca_ref[:, LANE:] + _swap_halves(krmg, MLA_ROPE // 4) * sa_ref[:, LANE:]
    kr128 = jnp.where(lane < MLA_ROPE, krmg, 0.0)
    _mla_kv(ckv_n, kr128, wukv_ref, ka_ref, va_ref)

    hd = GQA_HEAD_DIM
    gq, gk = gq_ref[...], gk_ref[...]
    for h in range(GQA_HEADS):
        q = _rms(gq[:, h * hd:(h + 1) * hd], gqg_ref[...])
        if rope:
            q = q * cb_ref[...] + _swap_halves(q, hd // 4) * sb_ref[...]
        qb_ref[:, h * hd:(h + 1) * hd] = (q * (hd ** -0.5 * LOG2E)).astype(BF16)
    for h in range(GQA_KV_HEADS):
        k = _rms(gk[:, h * hd:(h + 1) * hd], gkg_ref[...])
        if rope:
            k = k * cb_ref[...] + _swap_halves(k, hd // 4) * sb_ref[...]
        else:
            kbn_ref[:, h * hd:(h + 1) * hd] = k
        kb_ref[:, h * hd:(h + 1) * hd] = k.astype(BF16)
    vb_ref[...] = gv_ref[...].astype(BF16)


def _attn_prep(proj, tables, pr, *, layer, seq, tm_pref=1024):
    n = proj.shape[0]
    rope = tables is not None
    tm = _pick(seq if rope else n, tm_pref)
    per = max(1, seq // tm)

    def pblock(name, width):
        idx = _COL[name] // width
        assert idx * width == _COL[name]
        return pl.BlockSpec((tm, width), lambda i: (i, idx))

    in_specs = [pblock("cq", MLA_Q_LORA), pblock("ckv", MLA_KV_LORA),
                pl.BlockSpec((tm, LANE), lambda i: (i, KRMG_BLOCK)),
                pblock("gq", 512), pblock("gk", 256), pblock("gv", 256)]
    args = [proj] * 6
    if rope:
        in_specs += [pl.BlockSpec((tm, t.shape[1]), lambda i: (i % per, 0)) for t in tables]
        args += list(tables)
    ws = [pr["mla_q_norm"], pr["mla_kv_norm"], pr["mla_w_uq"], pr["mla_w_ukv"], pr["gqa_q_norm"], pr["gqa_k_norm"]]
    in_specs += [_layer_spec(w, layer, 1) for w in ws]
    args += ws
    gkv = GQA_KV_HEADS * GQA_HEAD_DIM
    widths = [(4 * 2 * LANE, BF16), (4 * 2 * LANE, BF16), (4 * MLA_V, BF16),
              (GQA_HEADS * GQA_HEAD_DIM, BF16), (gkv, BF16), (gkv, BF16)]
    if not rope:
        widths += [(MLA_KV_LORA, F32), (gkv, F32)]
    return pl.pallas_call(
        functools.partial(_attn_prep_kernel, rope=rope),
        out_shape=tuple(jax.ShapeDtypeStruct((n, w), dt) for w, dt in widths),
        grid=(n // tm,),
        in_specs=in_specs,
        out_specs=tuple(pl.BlockSpec((tm, w), lambda i: (i, 0)) for w, _ in widths),
        compiler_params=_cparams(("parallel",)),
        name="attn_prep",
    )(*args)


def _cache_kv_kernel(ckv_ref, kr_ref, wukv_ref, ka_ref, va_ref):
    _mla_kv(ckv_ref[...], kr_ref[...], wukv_ref, ka_ref, va_ref)


def _cache_kv(ckv, kr128, wukv, *, layer):
    b, _, p, _ = ckv.shape
    return pl.pallas_call(
        _cache_kv_kernel,
        out_shape=(jax.ShapeDtypeStruct((b, p, 4 * 2 * LANE), BF16), jax.ShapeDtypeStruct((b, p, 4 * MLA_V), BF16)),
        grid=(b,),
        in_specs=[pl.BlockSpec((None, None, p, MLA_KV_LORA), lambda i: (i, layer, 0, 0)),
                  pl.BlockSpec((None, None, p, LANE), lambda i: (i, layer, 0, 0)),
                  _layer_spec(wukv, layer, 1)],
        out_specs=(pl.BlockSpec((None, p, 4 * 2 * LANE), lambda i: (i, 0, 0)),
                   pl.BlockSpec((None, p, 4 * MLA_V), lambda i: (i, 0, 0))),
        compiler_params=_cparams(("parallel",)),
        name="mla_cache_kv",
    )(ckv, kr128, wukv)


def _attn_kernel(q_ref, *refs, nseg, hp, g, dk, dv):
    o_ref = refs[-1]
    for j in range(hp):
        kv = j // g
        q = q_ref[:, j * dk:(j + 1) * dk]
        ks = [refs[2 * i][:, kv * dk:(kv + 1) * dk].astype(BF16) for i in range(nseg)]
        vs = [refs[2 * i + 1][:, kv * dv:(kv + 1) * dv].astype(BF16) for i in range(nseg)]
        ss = [lax.dot_general(q, k, (((1,), (1,)), ((), ())), preferred_element_type=F32) for k in ks]
        m = functools.reduce(jnp.maximum, [jnp.max(s, axis=-1, keepdims=True) for s in ss])
        ps = [jnp.exp2(s - m) for s in ss]
        l = functools.reduce(jnp.add, [jnp.sum(p, axis=-1, keepdims=True) for p in ps])
        o = functools.reduce(jnp.add, [jnp.dot(p.astype(BF16), v, preferred_element_type=F32)
                                       for p, v in zip(ps, vs)])
        o_ref[:, j * dv:(j + 1) * dv] = o / l


def _attention(q, segs, *, heads, kv_heads, dk, dv, tq_pref=512, hp=4):
    b, sq, _ = q.shape
    tq = _pick(sq, tq_pref)
    g = heads // kv_heads
    assert heads % hp == 0 and (hp % g == 0 or g % hp == 0)
    kvp = max(1, hp // g)
    kcol = (lambda hb: hb) if hp >= g else (lambda hb: hb * hp // g)
    in_specs = [pl.BlockSpec((None, tq, hp * dk), lambda bi, hb, i: (bi, i, hb))]
    args = [q]
    for k, v, layer in segs:
        sk = k.shape[-2]
        if layer is None:
            in_specs += [pl.BlockSpec((None, sk, kvp * dk), lambda bi, hb, i: (bi, 0, kcol(hb))),
                         pl.BlockSpec((None, sk, kvp * dv), lambda bi, hb, i: (bi, 0, kcol(hb)))]
        else:
            in_specs += [pl.BlockSpec((None, None, sk, kvp * dk),
                                      lambda bi, hb, i, layer=layer: (bi, layer, 0, kcol(hb))),
                         pl.BlockSpec((None, None, sk, kvp * dv),
                                      lambda bi, hb, i, layer=layer: (bi, layer, 0, kcol(hb)))]
        args += [k, v]
    return pl.pallas_call(
        functools.partial(_attn_kernel, nseg=len(segs), hp=hp, g=min(g, hp), dk=dk, dv=dv),
        out_shape=jax.ShapeDtypeStruct((b, sq, heads * dv), F32),
        grid=(b, heads // hp, sq // tq),
        in_specs=in_specs,
        out_specs=pl.BlockSpec((None, tq, hp * dv), lambda bi, hb, i: (bi, i, hb)),
        compiler_params=_cparams(("parallel", "parallel", "parallel")),
        name="attention",
    )(*args)


def _log_sigmoid(x):
    return jnp.minimum(x, 0.0) - jnp.log1p(jnp.exp(-jnp.abs(x)))


def _mlstm_chunk_operands(q_ref, k_ref, v_ref, gate_ref, bias_ref):
    nh, hd = MLSTM_HEADS, MLSTM_HEAD_DIM
    g = gate_ref[...] + bias_ref[...]
    lf = _log_sigmoid(g)
    lf_hi = lf.astype(BF16)
    rem = lf - lf_hi.astype(F32)
    lf_mid = rem.astype(BF16)
    lf_lo = (rem - lf_mid.astype(F32)).astype(BF16)
    heads = []
    for h in range(nh):
        q = q_ref[:, h * hd:(h + 1) * hd]
        kh = k_ref[:, h * hd:(h + 1) * hd] * (hd ** -0.5)
        kt = kh.T
        qh = q.astype(BF16)
        heads.append(dict(q=q, qh=qh, kh=kh, kt=kt, vh=v_ref[:, h * hd:(h + 1) * hd].astype(BF16),
                          s=jnp.dot(qh, kt.astype(BF16), preferred_element_type=F32)))
    return dict(g=g, gt=g.T, lf=(lf_hi, lf_mid, lf_lo), heads=heads)


def _prefix_max_lanes(x, rev):
    n = x.shape[1]
    lane = lax.broadcasted_iota(jnp.int32, x.shape, 1)
    k = 1
    while k < n:
        if rev:
            shifted = jnp.where(lane < n - k, pltpu.roll(x, n - k, axis=1), -jnp.inf)
        else:
            shifted = jnp.where(lane >= k, pltpu.roll(x, k, axis=1), -jnp.inf)
        x = jnp.maximum(x, shifted)
        k *= 2
    return x


def _mlstm_chain(ops, h_ref, c_s, n_s, m_s, *, direction):
    nh, hd = MLSTM_HEADS, MLSTM_HEAD_DIM
    rev = direction == 1
    g, gt = ops["g"], ops["gt"]
    cl = g.shape[0]
    ri = lax.broadcasted_iota(jnp.int32, (cl, cl), 0)
    ci = lax.broadcasted_iota(jnp.int32, (cl, cl), 1)
    mask = (ri <= ci) if rev else (ri >= ci)
    tri = jnp.where(mask, 1.0, 0.0).astype(BF16)
    cum = functools.reduce(jnp.add, [jnp.dot(tri, part, preferred_element_type=F32) for part in ops["lf"]])
    cumt = cum.T
    last = 0 if rev else cl - 1
    li0 = MG_LANE + direction * 2 * nh
    lf0 = li0 + nh
    assert li0 % SUBLANE == 0 and 2 * nh == SUBLANE

    b8 = pltpu.roll(gt[li0:li0 + SUBLANE, :], nh, axis=0) - cumt[li0:li0 + SUBLANE, :]
    pm8 = _prefix_max_lanes(b8, rev)
    pm_rows = jnp.concatenate([jnp.zeros((li0, cl), F32), pm8, jnp.zeros((LANE - li0 - SUBLANE, cl), F32)], axis=0)
    pm_c = pm_rows.T
    lane = lax.broadcasted_iota(jnp.int32, (1, LANE), 1)
    m_lanes = functools.reduce(jnp.add, [jnp.where(lane == lf0 + h, m_s[direction, h:h + 1, :], 0.0)
                                         for h in range(nh)])
    mx_all = jnp.maximum(m_lanes, pm_c)
    interw_all = jnp.exp(m_lanes - mx_all)
    em_all = jnp.exp(-(cum + mx_all))

    for h in range(nh):
        hv = ops["heads"][h]
        cum_c = cum[:, lf0 + h:lf0 + h + 1]
        cum_r = cumt[lf0 + h:lf0 + h + 1, :]
        li_r = gt[li0 + h:li0 + h + 1, :]
        li_c = g[:, li0 + h:li0 + h + 1]
        m_prev = m_s[direction, h:h + 1, 0:1]
        w = jnp.where(mask, jnp.exp(b8[nh + h:nh + h + 1, :] - mx_all[:, lf0 + h:lf0 + h + 1]), 0.0)
        sc = hv["s"] * w
        inter_w = interw_all[:, lf0 + h:lf0 + h + 1]
        cmat = c_s[direction, h]
        nrow = n_s[direction, h:h + 1, :]
        num = (jnp.dot(sc.astype(BF16), hv["vh"], preferred_element_type=F32)
               + inter_w * jnp.dot(hv["qh"], cmat.astype(BF16), preferred_element_type=F32))
        qn = jnp.sum(hv["q"] * nrow, axis=1, keepdims=True)
        den = jnp.sum(sc, axis=1, keepdims=True) + inter_w * qn
        h_ref[:, h * hd:(h + 1) * hd] = num / jnp.maximum(jnp.abs(den), em_all[:, lf0 + h:lf0 + h + 1])

        tot = cum_c[last:last + 1, :]
        g_r = tot - cum_r + li_r
        g_c = tot - cum_c + li_c
        m_new = jnp.maximum(tot + m_prev, jnp.max(g_r, axis=1, keepdims=True))
        decay = jnp.exp(tot + m_prev - m_new)
        ws_r = jnp.exp(g_r - m_new)
        ws_c = jnp.exp(g_c - m_new)
        c_s[direction, h] = decay * cmat + jnp.dot((hv["kt"] * ws_r).astype(BF16), hv["vh"],
                                                   preferred_element_type=F32)
        n_s[direction, h:h + 1, :] = decay * nrow + jnp.sum(hv["kh"] * ws_c, axis=0, keepdims=True)
        m_s[direction, h:h + 1, :] = jnp.broadcast_to(m_new, (1, hd))


def _mlstm_kernel(*refs, shared, has_init, emit_state):
    nin = 4 if shared else 8
    fwd_in = refs[:4]
    bwd_in = fwd_in if shared else refs[4:8]
    bias_ref = refs[nin]
    pos = nin + 1
    if has_init:
        c0_ref, n0_ref, m0_ref = refs[pos:pos + 3]
        pos += 3
    hf_ref, hb_ref = refs[pos:pos + 2]
    pos += 2
    if emit_state:
        co_ref, no_ref, mo_ref = refs[pos:pos + 3]
        pos += 3
    c_s, n_s, m_s = refs[pos:pos + 3]
    c = pl.program_id(1)

    @pl.when(c == 0)
    def _():
        if has_init:
            c_s[...] = c0_ref[...]
            n_s[...] = n0_ref[...]
            m_s[...] = m0_ref[...]
        else:
            c_s[...] = jnp.zeros_like(c_s)
            n_s[...] = jnp.zeros_like(n_s)
            m_s[...] = jnp.zeros_like(m_s)

    ops_f = _mlstm_chunk_operands(*fwd_in, bias_ref)
    ops_b = ops_f if shared else _mlstm_chunk_operands(*bwd_in, bias_ref)
    _mlstm_chain(ops_f, hf_ref, c_s, n_s, m_s, direction=0)
    _mlstm_chain(ops_b, hb_ref, c_s, n_s, m_s, direction=1)

    if emit_state:
        @pl.when(c == pl.num_programs(1) - 1)
        def _():
            co_ref[...] = c_s[...]
            no_ref[...] = n_s[...]
            mo_ref[...] = m_s[...]


def _mlstm(proj, bias128, init, *, layer, batch, seq, emit_state, chunk_pref=256):
    n = proj.shape[0]
    cl = _pick(seq, chunk_pref)
    nc = seq // cl
    shared = nc == 1
    nh, hd = MLSTM_HEADS, MLSTM_HEAD_DIM
    rows = (lambda b, c: b * nc + c), (lambda b, c: b * nc + (nc - 1 - c))

    def chunk_specs(row):
        qkv = lambda name: pl.BlockSpec((cl, nh * hd), lambda b, c: (row(b, c), _COL[name] // (nh * hd)))
        return [qkv("mq"), qkv("mk"), qkv("mv"), pl.BlockSpec((cl, LANE), lambda b, c: (row(b, c), KRMG_BLOCK))]

    in_specs = chunk_specs(rows[0]) + ([] if shared else chunk_specs(rows[1]))
    in_specs.append(pl.BlockSpec((None, 1, LANE), lambda b, c: (layer, 0, 0)))
    args = [proj] * (len(in_specs) - 1) + [bias128]
    has_init = init is not None
    if has_init:
        c0, n0, m0 = init
        in_specs += [pl.BlockSpec((None, None, 2, nh, hd, hd), lambda b, c: (b, layer, 0, 0, 0, 0)),
                     pl.BlockSpec((None, None, 2, nh, hd), lambda b, c: (b, layer, 0, 0, 0)),
                     pl.BlockSpec((None, None, 2, nh, hd), lambda b, c: (b, layer, 0, 0, 0))]
        args += [c0, n0, m0]
    out_shape = [jax.ShapeDtypeStruct((n, nh * hd), F32)] * 2
    out_specs = [pl.BlockSpec((cl, nh * hd), lambda b, c, row=row: (row(b, c), 0)) for row in rows]
    if emit_state:
        out_shape += [jax.ShapeDtypeStruct((batch, 2, nh, hd, hd), F32), jax.ShapeDtypeStruct((batch, 2, nh, hd), F32),
                      jax.ShapeDtypeStruct((batch, 2, nh, hd), F32)]
        out_specs += [pl.BlockSpec((None, 2, nh, hd, hd), lambda b, c: (b, 0, 0, 0, 0)),
                      pl.BlockSpec((None, 2, nh, hd), lambda b, c: (b, 0, 0, 0)),
                      pl.BlockSpec((None, 2, nh, hd), lambda b, c: (b, 0, 0, 0))]
    return pl.pallas_call(
        functools.partial(_mlstm_kernel, shared=shared, has_init=has_init, emit_state=emit_state),
        out_shape=tuple(out_shape),
        grid=(batch, nc),
        in_specs=in_specs,
        out_specs=tuple(out_specs),
        scratch_shapes=[pltpu.VMEM((2, nh, hd, hd), F32), pltpu.VMEM((2, nh, hd), F32), pltpu.VMEM((2, nh, hd), F32)],
        compiler_params=_cparams(("parallel", "arbitrary")),
        name="mlstm",
    )(*args)


_S5_STATE_TILES = 2 * S5_NS // MXU_TILE
_S5_GROUPS_PER_TILE = MXU_TILE // S5_STATE


def _s5_channel_tile(state_tile):
    first_group = (state_tile % (S5_NS // MXU_TILE)) * _S5_GROUPS_PER_TILE
    return first_group * S5_GROUP // MXU_TILE


def _to_time_major_kernel(u_ref, p_ref, o_ref):
    rows = p_ref.shape[0]
    sg, ch = u_ref.shape[0], u_ref.shape[2]
    tsteps = rows // sg
    for k in range(u_ref.shape[1] // tsteps):
        u = u_ref[:, k * tsteps:(k + 1) * tsteps, :].reshape(rows, ch).astype(BF16)
        o_ref[k * rows:(k + 1) * rows, :] = jnp.dot(p_ref[...], u, preferred_element_type=F32).astype(BF16)


def _from_time_major_kernel(yf_ref, yb_ref, pt_ref, o_ref):
    rows = pt_ref.shape[0]
    sg, ch = o_ref.shape[0], o_ref.shape[2]
    tsteps = rows // sg
    pt = pt_ref[...]
    for k in range(o_ref.shape[1] // tsteps):
        y = yf_ref[k * rows:(k + 1) * rows, :] + yb_ref[k * rows:(k + 1) * rows, :]
        hi = y.astype(BF16)
        lo = (y - hi.astype(F32)).astype(BF16)
        out = jnp.dot(pt, hi, preferred_element_type=F32) + jnp.dot(pt, lo, preferred_element_type=F32)
        o_ref[:, k * tsteps:(k + 1) * tsteps, :] = out.reshape(sg, tsteps, ch)


def _s5_kernel(*refs, has_init):
    if has_init:
        u_ref, bd_ref, a_ref, cd_ref, x0_ref, y_ref, xf_ref, bu_s, x_s = refs
    else:
        u_ref, bd_ref, a_ref, cd_ref, y_ref, xf_ref, bu_s, x_s = refs
    d = pl.program_id(0)
    c = pl.program_id(2)
    ns = S5_NS
    sg = x_s.shape[0]
    tsteps = u_ref.shape[0] // sg
    mt = MXU_TILE

    @pl.when(c == 0)
    def _():
        if has_init:
            x_s[...] = x0_ref[...]
        else:
            x_s[...] = jnp.zeros_like(x_s)

    u = u_ref[...]
    for st in range(_S5_STATE_TILES):
        ct = _s5_channel_tile(st)
        bu_s[:, st * mt:(st + 1) * mt] = jnp.dot(u[:, ct * mt:(ct + 1) * mt],
                                                 bd_ref[ct * mt:(ct + 1) * mt, st * mt:(st + 1) * mt],
                                                 preferred_element_type=F32)
    a_re = jnp.broadcast_to(a_ref[:, :ns], (sg, ns))
    a_im = jnp.broadcast_to(a_ref[:, ns:], (sg, ns))

    def body(t, carry):
        xr, xi = carry
        tt = t + d * (tsteps - 1 - 2 * t)
        r0 = pl.multiple_of(tt * sg, sg)
        br = bu_s[pl.ds(r0, sg), :ns]
        bi = bu_s[pl.ds(r0, sg), ns:]
        nr = a_re * xr - a_im * xi + br
        ni = a_re * xi + a_im * xr + bi
        bu_s[pl.ds(r0, sg), :ns] = nr
        bu_s[pl.ds(r0, sg), ns:] = ni
        return nr, ni

    xr, xi = lax.fori_loop(0, tsteps, body, (x_s[:, :ns], x_s[:, ns:]), unroll=SCAN_UNROLL)
    x_s[:, :ns] = xr
    x_s[:, ns:] = xi

    for ct in range(S5_CH // mt):
        acc = None
        for st in range(_S5_STATE_TILES):
            if _s5_channel_tile(st) != ct:
                continue
            t = jnp.dot(bu_s[:, st * mt:(st + 1) * mt].astype(BF16),
                        cd_ref[st * mt:(st + 1) * mt, ct * mt:(ct + 1) * mt], preferred_element_type=F32)
            acc = t if acc is None else acc + t
        y_ref[:, ct * mt:(ct + 1) * mt] = acc

    @pl.when(c == pl.num_programs(2) - 1)
    def _():
        xf_ref[...] = x_s[...]


def _s5(proj3, pr, x0, *, layer, rows_pref=512, scan_rows_pref=1024):
    batch, seq, _ = proj3.shape
    sg = SUBLANE if batch % SUBLANE == 0 else batch
    ng = batch // sg
    tsteps = _pick(seq, max(SUBLANE, rows_pref // sg))
    rows = sg * tsteps
    nc = seq // tsteps
    r = jnp.arange(rows)
    perm = (jnp.arange(rows)[None, :] == ((r % sg) * tsteps + r // sg)[:, None]).astype(BF16)
    const2 = lambda a: pl.BlockSpec(a.shape, lambda g, c: (0,) * a.ndim, pipeline_mode=pl.Buffered(1))
    nsub = math.gcd(nc, RELAYOUT_SUB)
    nc_r = nc // nsub

    u_tm = pl.pallas_call(
        _to_time_major_kernel,
        out_shape=jax.ShapeDtypeStruct((ng, seq * sg, S5_CH), BF16),
        grid=(ng, nc_r),
        in_specs=[pl.BlockSpec((sg, nsub * tsteps, S5_CH), lambda g, c: (g, c, _COL["su"] // S5_CH)), const2(perm)],
        out_specs=pl.BlockSpec((None, nsub * rows, S5_CH), lambda g, c: (g, c, 0)),
        compiler_params=_cparams(("parallel", "parallel")),
        name="s5_to_time_major",
    )(proj3, perm)

    ts_scan = _pick(seq, max(SUBLANE, scan_rows_pref // sg))
    rows_scan = sg * ts_scan
    nc_scan = seq // ts_scan
    cpos = lambda d, c: c + d * (nc_scan - 1 - 2 * c)
    has_init = x0 is not None
    in_specs = [pl.BlockSpec((None, rows_scan, S5_CH), lambda d, g, c: (g, cpos(d, c), 0)),
                pl.BlockSpec((None, None, S5_CH, 2 * S5_NS), lambda d, g, c: (layer, d, 0, 0)),
                pl.BlockSpec((None, None, 1, 2 * S5_NS), lambda d, g, c: (layer, d, 0, 0)),
                _layer_spec(pr["s5_cd"], layer, 3)]
    args = [u_tm, pr["s5_bd"], pr["s5_abar"], pr["s5_cd"]]
    if has_init:
        in_specs.append(pl.BlockSpec((None, sg, 2 * S5_NS), lambda d, g, c: (d, g, 0)))
        args.append(x0)
    y_tm, xfin = pl.pallas_call(
        functools.partial(_s5_kernel, has_init=has_init),
        out_shape=(jax.ShapeDtypeStruct((2, ng, seq * sg, S5_CH), F32),
                   jax.ShapeDtypeStruct((2, batch, 2 * S5_NS), F32)),
        grid=(2, ng, nc_scan),
        in_specs=in_specs,
        out_specs=(pl.BlockSpec((None, None, rows_scan, S5_CH), lambda d, g, c: (d, g, cpos(d, c), 0)),
                   pl.BlockSpec((None, sg, 2 * S5_NS), lambda d, g, c: (d, g, 0))),
        scratch_shapes=[pltpu.VMEM((rows_scan, 2 * S5_NS), F32), pltpu.VMEM((sg, 2 * S5_NS), F32)],
        compiler_params=_cparams(("parallel", "parallel", "arbitrary")),
        name="s5_scan",
    )(*args)

    y = pl.pallas_call(
        _from_time_major_kernel,
        out_shape=jax.ShapeDtypeStruct((batch, seq, S5_CH), F32),
        grid=(ng, nc_r),
        in_specs=[pl.BlockSpec((None, None, nsub * rows, S5_CH), lambda g, c: (0, g, c, 0)),
                  pl.BlockSpec((None, None, nsub * rows, S5_CH), lambda g, c: (1, g, c, 0)),
                  const2(perm)],
        out_specs=pl.BlockSpec((sg, nsub * tsteps, S5_CH), lambda g, c: (g, c, 0)),
        compiler_params=_cparams(("parallel", "parallel")),
        name="s5_from_time_major",
    )(y_tm, y_tm, perm.T)
    return y, xfin


def _gelu_tanh(x):
    return 0.5 * x * (1.0 + jnp.tanh(math.sqrt(2.0 / math.pi) * (x + 0.044715 * (x * x * x))))


def _outproj_kernel(x_ref, mod_ref, oa_ref, ob_ref, hf_ref, hb_ref, mo_ref, y_ref, su_ref,
                    d_ref, wglu_ref, on_ref, wout_ref, o_ref):
    slab = math.gcd(x_ref.shape[0], OUTPROJ_SLAB)
    for r in range(x_ref.shape[0] // slab):
        rows = slice(r * slab, (r + 1) * slab)
        oc = jax.nn.sigmoid(mo_ref[rows, :]) * (hf_ref[rows, :] + hb_ref[rows, :])
        y = _gelu_tanh(y_ref[rows, :] + d_ref[...] * su_ref[rows, :])
        od = y * jax.nn.sigmoid(jnp.dot(y.astype(BF16), wglu_ref[...], preferred_element_type=F32))
        acc = None
        for gi, part in enumerate((oa_ref[rows, :], ob_ref[rows, :], oc, od)):
            nrm = _rms(part, on_ref[gi:gi + 1, :]).astype(BF16)
            t = jnp.dot(nrm, wout_ref[gi * GROUP_WIDTH:(gi + 1) * GROUP_WIDTH, :], preferred_element_type=F32)
            acc = t if acc is None else acc + t
        o_ref[rows, :] = x_ref[rows, :] + mod_ref[5:6, :] * acc


def _outproj(x, mod, oa, ob, hf, hb, proj, y, pr, *, layer, seq, tm_pref=512):
    n, d = x.shape
    groups = mod.shape[0]
    tm = _pick(seq if groups > 1 else n, tm_pref)
    per = seq // tm if groups > 1 else 1
    mod_map = (lambda i: (i // per, 0, 0)) if groups > 1 else (lambda i: (0, 0, 0))
    gw = GROUP_WIDTH
    rowblk = pl.BlockSpec((tm, gw), lambda i: (i, 0))
    ws = [pr["s5_d"], pr["s5_w_glu"], pr["out_norm"], pr["w_out"]]
    return pl.pallas_call(
        _outproj_kernel,
        out_shape=jax.ShapeDtypeStruct((n, d), F32),
        grid=(n // tm,),
        in_specs=[pl.BlockSpec((tm, d), lambda i: (i, 0)),
                  pl.BlockSpec((None, N_MOD, d), mod_map),
                  rowblk, rowblk, rowblk, rowblk,
                  pl.BlockSpec((tm, gw), lambda i: (i, _COL["mo"] // gw)),
                  rowblk,
                  pl.BlockSpec((tm, gw), lambda i: (i, _COL["su"] // gw))] + [_layer_spec(w, layer, 1) for w in ws],
        out_specs=pl.BlockSpec((tm, d), lambda i: (i, 0)),
        compiler_params=_cparams(("parallel",)),
        name="merge_out_proj",
    )(x, mod, oa, ob, hf, hb, proj, y, proj, *ws)


def _final_norm_kernel(x_ref, g_ref, o_ref):
    o_ref[...] = _rms(x_ref[...], g_ref[...])


def _final_norm(x, g, tm_pref=1024):
    n, d = x.shape
    tm = _pick(n, tm_pref)
    return pl.pallas_call(
        _final_norm_kernel,
        out_shape=jax.ShapeDtypeStruct((n, d), F32),
        grid=(n // tm,),
        in_specs=[pl.BlockSpec((tm, d), lambda i: (i, 0)), pl.BlockSpec((1, d), lambda i: (0, 0))],
        out_specs=pl.BlockSpec((tm, d), lambda i: (i, 0)),
        compiler_params=_cparams(("parallel",)),
        name="final_norm",
    )(x, g)


def _trunk_layer(x, mod, pr, *, layer, batch, seq, tables, cache):
    latent = cache is not None
    x = _ffn(x, mod, pr, layer=layer, which=0, seq=seq)
    proj = _inproj(x, mod, pr, layer=layer, seq=seq)

    prep = _attn_prep(proj, tables, pr, layer=layer, seq=seq)
    qa, ka, va, qb, kb, vb = prep[:6]
    r3 = lambda a: a.reshape(batch, seq, a.shape[-1])
    segs_a = [(r3(ka), r3(va), None)]
    segs_b = [(r3(kb), r3(vb), None)]
    if latent:
        segs_a.insert(0, cache["mla_kv"] + (None,))
        segs_b.insert(0, cache["gqa_kv"] + (layer,))
    oa = _attention(r3(qa), segs_a, heads=MLA_HEADS, kv_heads=MLA_HEADS, dk=2 * LANE, dv=MLA_V)
    ob = _attention(r3(qb), segs_b, heads=GQA_HEADS, kv_heads=GQA_KV_HEADS, dk=GQA_HEAD_DIM, dv=GQA_HEAD_DIM)
    oa = oa.reshape(batch * seq, -1)
    ob = ob.reshape(batch * seq, -1)

    m_init = cache["mlstm"] if latent else None
    mres = _mlstm(proj, pr["mlstm_bias"], m_init, layer=layer, batch=batch, seq=seq, emit_state=not latent)

    y, xfin = _s5(proj.reshape(batch, seq, PROJ_COLS), pr, cache["s5"] if latent else None, layer=layer)

    x = _outproj(x, mod, oa, ob, mres[0], mres[1], proj, y.reshape(batch * seq, S5_CH), pr,
                 layer=layer, seq=seq)
    x = _ffn(x, mod, pr, layer=layer, which=1, seq=seq)

    new_ctx = None
    if not latent:
        ckvn, kbn = prep[6:]
        kr = proj[:, _COL["kr"]:_COL["kr"] + MLA_ROPE]
        gv = proj[:, _COL["gv"]:_COL["gv"] + GQA_KV_HEADS * GQA_HEAD_DIM]
        xs = xfin.reshape(2, batch, 2, S5_GROUPS, S5_STATE).transpose(1, 0, 2, 3, 4)
        new_ctx = (ckvn.reshape(batch, seq, MLA_KV_LORA),
                   kr.reshape(batch, seq, MLA_ROPE),
                   kbn.reshape(batch, seq, GQA_KV_HEADS, GQA_HEAD_DIM),
                   gv.reshape(batch, seq, GQA_KV_HEADS, GQA_HEAD_DIM),
                   mres[2], mres[3], mres[4][..., 0],
                   xs[:, :, 0], xs[:, :, 1])
    return x, new_ctx


def _permute_w_in(w_in):
    parts = [w_in[..., _ORIG[n][0]:_ORIG[n][0] + _ORIG[n][1]] for n in _ORDER]
    pad = PROJ_COLS - sum(p.shape[-1] for p in parts)
    parts.append(jnp.zeros(w_in.shape[:-1] + (pad,), w_in.dtype))
    return jnp.concatenate(parts, axis=-1).astype(BF16)


def _permute_w_uq(w_uq):
    depth, k, _ = w_uq.shape
    w = w_uq.reshape(depth, k, MLA_HEADS, MLA_NOPE + MLA_ROPE)
    w = jnp.pad(w, ((0, 0), (0, 0), (0, 0), (0, 2 * LANE - MLA_NOPE - MLA_ROPE)))
    return w.reshape(depth, k, MLA_HEADS * 2 * LANE).astype(BF16)


def kernel(x_prompt, x_sample, cache_mla_ckv, cache_mla_krope, cache_gqa_k, cache_gqa_v, state_mlstm_c, state_mlstm_n, state_mlstm_m, state_s5_re, state_s5_im, c, c_ctx, ada_w, ada_b, norm_g, ffn_w13, ffn_w2, w_in, mla_q_norm, mla_kv_norm, mla_w_uq, mla_w_ukv, gqa_q_norm, gqa_k_norm, mlstm_gate_b, s5_a_re, s5_a_im, s5_log_dt, s5_b_re, s5_b_im, s5_c_re, s5_c_im, s5_d, s5_w_glu, out_norm, w_out, final_norm):
    bc, sc, d = x_prompt.shape
    bl, sl, _ = x_sample.shape
    depth = ada_w.shape[0]
    past = cache_mla_ckv.shape[2]

    rows = ((1 + bl + SUBLANE - 1) // SUBLANE) * SUBLANE
    cvecs = jnp.concatenate([c_ctx[None, :], c, jnp.zeros((rows - 1 - bl, d), F32)], axis=0)
    mod_all = _modulation(cvecs, ada_w, ada_b).reshape(depth, rows, N_MOD, d)

    tables = _rope_tables(sl)

    abar, bd, cd = _s5_params(s5_a_re, s5_a_im, s5_log_dt, s5_b_re, s5_b_im, s5_c_re, s5_c_im)
    nmg = MLSTM_HEADS * 4
    row = lambda a: a.reshape(depth, 1, a.shape[-1])
    pr = {"norm_g": norm_g.reshape(depth, 3, 1, d),
          "ffn_w13": ffn_w13.astype(BF16), "ffn_w2": ffn_w2.astype(BF16),
          "w_in": _permute_w_in(w_in),
          "mla_q_norm": row(mla_q_norm), "mla_kv_norm": row(mla_kv_norm),
          "mla_w_uq": _permute_w_uq(mla_w_uq), "mla_w_ukv": mla_w_ukv.astype(BF16),
          "gqa_q_norm": row(gqa_q_norm), "gqa_k_norm": row(gqa_k_norm),
          "mlstm_bias": jnp.pad(mlstm_gate_b.reshape(depth, 1, nmg), ((0, 0), (0, 0), (MG_LANE, LANE - MG_LANE - nmg))),
          "s5_abar": abar, "s5_bd": bd, "s5_cd": cd,
          "s5_d": row(s5_d), "s5_w_glu": s5_w_glu.astype(BF16),
          "out_norm": out_norm.reshape(depth, 4, GROUP_WIDTH), "w_out": w_out.astype(BF16)}

    gkv = GQA_KV_HEADS * GQA_HEAD_DIM
    cache_k = cache_gqa_k.reshape(bl, depth, past, gkv)
    cache_v = cache_gqa_v.reshape(bl, depth, past, gkv)
    m0 = jnp.broadcast_to(state_mlstm_m[..., None], state_mlstm_n.shape)
    kr_pad = jnp.pad(cache_mla_krope, ((0, 0), (0, 0), (0, 0), (0, LANE - MLA_ROPE)))

    x_ctx = x_prompt.reshape(bc * sc, d)
    x_lat = x_sample.reshape(bl * sl, d)
    per_layer = []
    for l in range(depth):
        x_ctx, ctx_l = _trunk_layer(x_ctx, mod_all[l, 0:1], pr, layer=l, batch=bc, seq=sc, tables=None, cache=None)
        per_layer.append(ctx_l)

        x0 = jnp.concatenate([state_s5_re[:, l].reshape(bl, 2, S5_NS), state_s5_im[:, l].reshape(bl, 2, S5_NS)],
                             axis=-1).transpose(1, 0, 2)
        cache = {"mla_kv": tuple(_cache_kv(cache_mla_ckv, kr_pad, pr["mla_w_ukv"], layer=l)),
                 "gqa_kv": (cache_k, cache_v),
                 "mlstm": (state_mlstm_c, state_mlstm_n, m0),
                 "s5": x0}
        x_lat, _ = _trunk_layer(x_lat, mod_all[l, 1:1 + bl], pr, layer=l, batch=bl, seq=sl, tables=tables,
                                cache=cache)

    new_ctx = [jnp.stack([t[i] for t in per_layer], axis=1) for i in range(9)]
    y_prompt = _final_norm(x_ctx, final_norm[None, :]).reshape(bc, sc, d)
    y_sample = _final_norm(x_lat, final_norm[None, :]).reshape(bl, sl, d)
    return (y_prompt, y_sample, *new_ctx)
```

```python
import functools
import math

import jax
import jax.numpy as jnp
from jax import lax
from jax.experimental import pallas as pl
from jax.experimental.pallas import tpu as pltpu

F32 = jnp.float32
BF16 = jnp.bfloat16

EPS = 1e-6
ROPE_THETA = 10000.0
GRID_W = 64
N_MOD = 9
LOG2E = math.log2(math.e)

D_FF = 5632
MLA_HEADS, MLA_NOPE, MLA_ROPE, MLA_V = 4, 128, 64, 128
MLA_Q_LORA, MLA_KV_LORA = 384, 256
GQA_HEADS, GQA_KV_HEADS, GQA_HEAD_DIM = 4, 2, 128
MLSTM_HEADS, MLSTM_HEAD_DIM = 4, 128
S5_GROUPS, S5_GROUP, S5_STATE = 32, 16, 64
S5_CH = S5_GROUPS * S5_GROUP
S5_NS = S5_GROUPS * S5_STATE
GROUP_WIDTH = 512

LANE = 128
SUBLANE = 8
MXU_TILE = 256
NORM_ROWS = 128
FFN_TILE = 512
RELAYOUT_SUB = 4
OUTPROJ_SLAB = 256
SCAN_UNROLL = 4
VMEM_BYTES = 64 * 1024 * 1024
VMEM_LIMIT_BYTES = VMEM_BYTES - 8 * 1024 * 1024
FFN_VMEM_LIMIT_BYTES = VMEM_BYTES - 4 * 1024 * 1024

_ORIG = dict(cq=(0, 384), ckv=(384, 256), kr=(640, 64), gq=(704, 512), gk=(1216, 256), gv=(1472, 256),
             mq=(1728, 512), mk=(2240, 512), mv=(2752, 512), mo=(3264, 512), mg=(3776, 16), su=(3792, 512))
_ORDER = ("mq", "mk", "mv", "mo", "gq", "su", "gk", "gv", "ckv", "cq", "kr", "mg")
_COL = {}
_off = 0
for _n in _ORDER:
    _COL[_n] = _off
    _off += _ORIG[_n][1]
PROJ_COLS = ((_off + LANE - 1) // LANE) * LANE
KRMG_BLOCK = _COL["kr"] // LANE
MG_LANE = _COL["mg"] - _COL["kr"]


def _cparams(sem, vmem_limit=VMEM_LIMIT_BYTES):
    return pltpu.CompilerParams(dimension_semantics=sem, vmem_limit_bytes=vmem_limit)


def _pick(n, pref):
    for t in range(min(n, pref), 0, -1):
        if n % t == 0 and (t % SUBLANE == 0 or t == n):
            return t
    return n


def _layer_spec(arr, layer, nargs):
    zeros = (0,) * (arr.ndim - 1)
    imap = {1: lambda i: (layer,) + zeros, 2: lambda i, j: (layer,) + zeros,
            3: lambda i, j, k: (layer,) + zeros}[nargs]
    return pl.BlockSpec((None,) + arr.shape[1:], imap, pipeline_mode=pl.Buffered(1))


def _rms(x, g):
    return x * lax.rsqrt(jnp.mean(x * x, axis=-1, keepdims=True) + EPS) * g


def _modulate(x, g, scale, shift):
    return x * lax.rsqrt(jnp.mean(x * x, axis=-1, keepdims=True) + EPS) * (g * (1.0 + scale)) + shift


def _swap_halves(x, half):
    w = x.shape[-1]
    lane = lax.broadcasted_iota(jnp.int32, x.shape, x.ndim - 1)
    first = (lane & (2 * half - 1)) < half
    return jnp.where(first, pltpu.roll(x, w - half, axis=x.ndim - 1), pltpu.roll(x, half, axis=x.ndim - 1))


def _silu(x):
    return x * jax.nn.sigmoid(x)


def _mod_kernel(c_ref, w_ref, b_ref, o_ref):
    a = _silu(c_ref[...]).astype(BF16)
    o_ref[...] = jnp.dot(a, w_ref[...].astype(BF16), preferred_element_type=F32) + b_ref[...]


def _modulation(cvecs, ada_w, ada_b):
    depth, d, nm = ada_w.shape
    r = cvecs.shape[0]
    tn = _pick(nm, 1024)
    return pl.pallas_call(
        _mod_kernel,
        out_shape=jax.ShapeDtypeStruct((depth, r, nm), F32),
        grid=(depth, nm // tn),
        in_specs=[pl.BlockSpec((r, d), lambda l, j: (0, 0)),
                  pl.BlockSpec((None, d, tn), lambda l, j: (l, 0, j)),
                  pl.BlockSpec((None, 1, tn), lambda l, j: (l, 0, j))],
        out_specs=pl.BlockSpec((None, r, tn), lambda l, j: (l, 0, j)),
        compiler_params=_cparams(("parallel", "parallel")),
        name="adaln_mod",
    )(cvecs, ada_w, ada_b.reshape(depth, 1, nm))


def _rope_kernel(ca_ref, sa_ref, cb_ref, sb_ref, *, log2w):
    s = ca_ref.shape[0]
    t = lax.broadcasted_iota(jnp.int32, (s, 1), 0)
    row = lax.shift_right_logical(t, log2w).astype(F32)
    col = (t & ((1 << log2w) - 1)).astype(F32)

    def tables(width, off, rd):
        lane = lax.broadcasted_iota(jnp.int32, (1, width), 1)
        r = lane - off
        inr = (r >= 0) & (r < rd)
        half, quarter = rd // 2, rd // 4
        is_col = r >= half
        rr = jnp.where(is_col, r - half, r)
        second = rr >= quarter
        j = jnp.where(second, rr - quarter, rr).astype(F32)
        inv = jnp.exp(j * (-2.0 / half * math.log(ROPE_THETA)))
        ang = jnp.where(is_col, col, row) * inv
        sign = jnp.where(second, 1.0, -1.0)
        return jnp.where(inr, jnp.cos(ang), 1.0), jnp.where(inr, sign * jnp.sin(ang), 0.0)

    ca, sa = tables(2 * LANE, MLA_NOPE, MLA_ROPE)
    cb, sb = tables(GQA_HEAD_DIM, 0, GQA_HEAD_DIM)
    ca_ref[...] = ca
    sa_ref[...] = sa
    cb_ref[...] = cb
    sb_ref[...] = sb


def _rope_tables(s):
    log2w = GRID_W.bit_length() - 1
    assert 1 << log2w == GRID_W
    shp = lambda w: jax.ShapeDtypeStruct((s, w), F32)
    return pl.pallas_call(
        functools.partial(_rope_kernel, log2w=log2w),
        out_shape=(shp(2 * LANE), shp(2 * LANE), shp(GQA_HEAD_DIM), shp(GQA_HEAD_DIM)),
        name="rope_tables",
    )()


def _s5_params(a_re, a_im, log_dt, b_re, b_im, c_re, c_im):
    depth, nd = a_re.shape[:2]
    r = depth * nd
    are = a_re.reshape(r, S5_NS)
    aim = a_im.reshape(r, S5_NS)
    ldt = jnp.broadcast_to(log_dt[..., None], (depth, nd, S5_GROUPS, S5_STATE)).reshape(r, S5_NS)
    rep = lambda b: jnp.repeat(jnp.transpose(b, (0, 3, 1, 2)).reshape(depth, 1, S5_GROUP, S5_NS), nd, axis=1)
    bre = rep(b_re).reshape(r, S5_GROUP, S5_NS)
    bim = rep(b_im).reshape(r, S5_GROUP, S5_NS)
    v = jax.ShapeDtypeStruct((r, S5_NS), F32)
    m = jax.ShapeDtypeStruct((r, S5_GROUP, S5_NS), F32)
    ar, ai, bbr, bbi = pl.pallas_call(_s5_param_rows_kernel, out_shape=(v, v, m, m), name="s5_discretise")(
        are, aim, ldt, bre, bim)
    eye = jnp.eye(S5_GROUPS, dtype=F32)

    def dense_b(bb):
        bb = bb.reshape(r, S5_GROUP, S5_GROUPS, S5_STATE)
        return jnp.einsum("dcgp,gh->dgchp", bb, eye).reshape(r, S5_CH, S5_NS)

    def dense_c(cc):
        return jnp.einsum("lgcp,gh->lhpgc", cc, eye).reshape(depth, S5_NS, S5_CH)

    bd = jnp.concatenate([dense_b(bbr), dense_b(bbi)], axis=-1).astype(BF16).reshape(depth, nd, S5_CH, 2 * S5_NS)
    cd = jnp.concatenate([dense_c(c_re), -dense_c(c_im)], axis=1).astype(BF16)
    abar = jnp.concatenate([ar, ai], axis=-1).reshape(depth, nd, 1, 2 * S5_NS)
    return abar, bd, cd


def _s5_param_rows_kernel(are_ref, aim_ref, ldt_ref, bre_ref, bim_ref, oar_ref, oai_ref, obr_ref, obi_ref):
    lr, li = are_ref[...], aim_ref[...]
    dt = jnp.exp(ldt_ref[...])
    mag = jnp.exp(lr * dt)
    ar, ai = mag * jnp.cos(li * dt), mag * jnp.sin(li * dt)
    oar_ref[...] = ar
    oai_ref[...] = ai
    nr, ni = ar - 1.0, ai
    den = lr * lr + li * li
    fr, fi = (nr * lr + ni * li) / den, (ni * lr - nr * li) / den
    for d in range(fr.shape[0]):
        br, bi = bre_ref[d], bim_ref[d]
        obr_ref[d] = fr[d:d + 1] * br - fi[d:d + 1] * bi
        obi_ref[d] = fr[d:d + 1] * bi + fi[d:d + 1] * br


def _ffn_kernel(*refs, row0, slab, final):
    if final:
        x_ref, mod_ref, g_ref, w1_ref, w3_ref, w2_ref, fg_ref, o_ref, h_ref = refs
    else:
        x_ref, mod_ref, g_ref, w1_ref, w3_ref, w2_ref, o_ref, h_ref = refs
    j = pl.program_id(1)
    nslab = h_ref.shape[0] // slab
    nr = math.gcd(slab, NORM_ROWS)

    def partial_ffn(rows):
        h = h_ref[rows, :]
        gate = jnp.dot(h, w1_ref[...], preferred_element_type=F32)
        up = jnp.dot(h, w3_ref[...], preferred_element_type=F32)
        act = (_silu(gate) * up).astype(BF16)
        return jnp.dot(act, w2_ref[...], preferred_element_type=F32)

    @pl.when(j == 0)
    def _():
        for r in range(nslab):
            for p in range(slab // nr):
                rows = slice(r * slab + p * nr, r * slab + (p + 1) * nr)
                h = _modulate(x_ref[rows, :], g_ref[...], mod_ref[row0 + 1:row0 + 2, :], mod_ref[row0:row0 + 1, :])
                h_ref[rows, :] = h.astype(BF16)
            rows = slice(r * slab, (r + 1) * slab)
            o_ref[rows, :] = partial_ffn(rows)

    last = pl.num_programs(1) - 1

    @pl.when((j > 0) & (j < last))
    def _():
        for r in range(nslab):
            rows = slice(r * slab, (r + 1) * slab)
            o_ref[rows, :] += partial_ffn(rows)

    @pl.when(j == last)
    def _():
        gate = 0.5 * mod_ref[row0 + 2:row0 + 3, :]
        for r in range(nslab):
            rows = slice(r * slab, (r + 1) * slab)
            o_ref[rows, :] = x_ref[rows, :] + gate * (o_ref[rows, :] + partial_ffn(rows))
            if final:
                for p in range(slab // nr):
                    piece = slice(r * slab + p * nr, r * slab + (p + 1) * nr)
                    o_ref[piece, :] = _rms(o_ref[piece, :], fg_ref[...])


def _ffn(x, mod, pr, *, layer, which, seq, final_g=None, tm_pref=1024):
    n, d = x.shape
    final = final_g is not None
    groups = mod.shape[0]
    tm = _pick(seq if groups > 1 else n, tm_pref)
    tf = FFN_TILE
    nf = D_FF // tf
    assert nf >= 2
    per = seq // tm if groups > 1 else 1
    mod_map = (lambda i, j: (i // per, 0, 0)) if groups > 1 else (lambda i, j: (0, 0, 0))
    in_specs = [pl.BlockSpec((tm, d), lambda i, j: (i, 0)),
                pl.BlockSpec((None, N_MOD, d), mod_map),
                pl.BlockSpec((None, None, 1, d), lambda i, j: (layer, 2 * which, 0, 0)),
                pl.BlockSpec((None, None, d, tf), lambda i, j: (layer, which, 0, j)),
                pl.BlockSpec((None, None, d, tf), lambda i, j: (layer, which, 0, nf + j)),
                pl.BlockSpec((None, None, tf, d), lambda i, j: (layer, which, j, 0))]
    args = [x, mod, pr["norm_g"], pr["ffn_w13"], pr["ffn_w13"], pr["ffn_w2"]]
    if final:
        in_specs.append(pl.BlockSpec((1, d), lambda i, j: (0, 0)))
        args.append(final_g)
    return pl.pallas_call(
        functools.partial(_ffn_kernel, row0=6 * which, slab=_pick(tm, 512), final=final),
        out_shape=jax.ShapeDtypeStruct((n, d), F32),
        grid=(n // tm, nf),
        in_specs=in_specs,
        out_specs=pl.BlockSpec((tm, d), lambda i, j: (i, 0)),
        scratch_shapes=[pltpu.VMEM((tm, d), BF16)],
        compiler_params=_cparams(("parallel", "arbitrary"), FFN_VMEM_LIMIT_BYTES),
        name="ffn",
    )(*args)


def _inproj_kernel(x_ref, mod_ref, g_ref, w_ref, o_ref):
    h = _modulate(x_ref[...], g_ref[...], mod_ref[4:5, :], mod_ref[3:4, :]).astype(BF16)
    o_ref[...] = jnp.dot(h, w_ref[...], preferred_element_type=F32)


def _inproj(x, mod, pr, *, layer, seq, tm_pref=512):
    n, d = x.shape
    groups = mod.shape[0]
    tm = _pick(seq if groups > 1 else n, tm_pref)
    per = seq // tm if groups > 1 else 1
    mod_map = (lambda i: (i // per, 0, 0)) if groups > 1 else (lambda i: (0, 0, 0))
    return pl.pallas_call(
        _inproj_kernel,
        out_shape=jax.ShapeDtypeStruct((n, PROJ_COLS), F32),
        grid=(n // tm,),
        in_specs=[pl.BlockSpec((tm, d), lambda i: (i, 0)),
                  pl.BlockSpec((None, N_MOD, d), mod_map),
                  pl.BlockSpec((None, None, 1, d), lambda i: (layer, 1, 0, 0)),
                  _layer_spec(pr["w_in"], layer, 1)],
        out_specs=pl.BlockSpec((tm, PROJ_COLS), lambda i: (i, 0)),
        compiler_params=_cparams(("parallel",)),
        name="in_proj",
    )(x, mod, pr["norm_g"], pr["w_in"])


def _mla_kv(ckv_n, kr128, wukv_ref, ka_ref, va_ref):
    kv = jnp.dot(ckv_n.astype(BF16), wukv_ref[...], preferred_element_type=F32)
    krb = kr128.astype(BF16)
    hw = MLA_NOPE + MLA_V
    for h in range(MLA_HEADS):
        ka_ref[:, h * 2 * LANE:h * 2 * LANE + MLA_NOPE] = kv[:, h * hw:h * hw + MLA_NOPE].astype(BF16)
        ka_ref[:, h * 2 * LANE + MLA_NOPE:(h + 1) * 2 * LANE] = krb
        va_ref[:, h * MLA_V:(h + 1) * MLA_V] = kv[:, h * hw + MLA_NOPE:(h + 1) * hw].astype(BF16)


def _attn_prep_kernel(*refs, rope):
    if rope:
        (cq_ref, ckv_ref, krmg_ref, gq_ref, gk_ref, gv_ref, ca_ref, sa_ref, cb_ref, sb_ref,
         qng_ref, kvng_ref, wuq_ref, wukv_ref, gqg_ref, gkg_ref,
         qa_ref, ka_ref, va_ref, qb_ref, kb_ref, vb_ref) = refs
    else:
        (cq_ref, ckv_ref, krmg_ref, gq_ref, gk_ref, gv_ref,
         qng_ref, kvng_ref, wuq_ref, wukv_ref, gqg_ref, gkg_ref,
         qa_ref, ka_ref, va_ref, qb_ref, kb_ref, vb_ref, ckvn_ref, kbn_ref) = refs

    cqn = _rms(cq_ref[...], qng_ref[...]).astype(BF16)
    qa = jnp.dot(cqn, wuq_ref[...], preferred_element_type=F32)
    if rope:
        ca = jnp.concatenate([ca_ref[...]] * MLA_HEADS, axis=1)
        sa = jnp.concatenate([sa_ref[...]] * MLA_HEADS, axis=1)
        qa = qa * ca + _swap_halves(qa, MLA_ROPE // 4) * sa
    qa_ref[...] = (qa * ((MLA_NOPE + MLA_ROPE) ** -0.5 * LOG2E)).astype(BF16)

    ckv_n = _rms(ckv_ref[...], kvng_ref[...])
    if not rope:
        ckvn_ref[...] = ckv_n
    krmg = krmg_ref[...]
    lane = lax.broadcasted_iota(jnp.int32, krmg.shape, 1)
    if rope:
        krmg = krmg * ca_ref[:, LANE:] + _swap_halves(krmg, MLA_ROPE // 4) * sa_ref[:, LANE:]
    kr128 = jnp.where(lane < MLA_ROPE, krmg, 0.0)
    _mla_kv(ckv_n, kr128, wukv_ref, ka_ref, va_ref)

    hd = GQA_HEAD_DIM
    gq, gk = gq_ref[...], gk_ref[...]
    for h in range(GQA_HEADS):
        q = _rms(gq[:, h * hd:(h + 1) * hd], gqg_ref[...])
        if rope:
            q = q * cb_ref[...] + _swap_halves(q, hd // 4) * sb_ref[...]
        qb_ref[:, h * hd:(h + 1) * hd] = (q * (hd ** -0.5 * LOG2E)).astype(BF16)
    for h in range(GQA_KV_HEADS):
        k = _rms(gk[:, h * hd:(h + 1) * hd], gkg_ref[...])
        if rope:
            k = k * cb_ref[...] + _swap_halves(k, hd // 4) * sb_ref[...]
        else:
            kbn_ref[:, h * hd:(h + 1) * hd] = k
        kb_ref[:, h * hd:(h + 1) * hd] = k.astype(BF16)
    vb_ref[...] = gv_ref[...].astype(BF16)


def _attn_prep(proj, tables, pr, *, layer, seq, tm_pref=1024):
    n = proj.shape[0]
    rope = tables is not None
    tm = _pick(seq if rope else n, tm_pref)
    per = max(1, seq // tm)

    def pblock(name, width):
        idx = _COL[name] // width
        assert idx * width == _COL[name]
        return pl.BlockSpec((tm, width), lambda i: (i, idx))

    in_specs = [pblock("cq", MLA_Q_LORA), pblock("ckv", MLA_KV_LORA),
                pl.BlockSpec((tm, LANE), lambda i: (i, KRMG_BLOCK)),
                pblock("gq", 512), pblock("gk", 256), pblock("gv", 256)]
    args = [proj] * 6
    if rope:
        in_specs += [pl.BlockSpec((tm, t.shape[1]), lambda i: (i % per, 0)) for t in tables]
        args += list(tables)
    ws = [pr["mla_q_norm"], pr["mla_kv_norm"], pr["mla_w_uq"], pr["mla_w_ukv"], pr["gqa_q_norm"], pr["gqa_k_norm"]]
    in_specs += [_layer_spec(w, layer, 1) for w in ws]
    args += ws
    gkv = GQA_KV_HEADS * GQA_HEAD_DIM
    widths = [(4 * 2 * LANE, BF16), (4 * 2 * LANE, BF16), (4 * MLA_V, BF16),
              (GQA_HEADS * GQA_HEAD_DIM, BF16), (gkv, BF16), (gkv, BF16)]
    if not rope:
        widths += [(MLA_KV_LORA, F32), (gkv, F32)]
    return pl.pallas_call(
        functools.partial(_attn_prep_kernel, rope=rope),
        out_shape=tuple(jax.ShapeDtypeStruct((n, w), dt) for w, dt in widths),
        grid=(n // tm,),
        in_specs=in_specs,
        out_specs=tuple(pl.BlockSpec((tm, w), lambda i: (i, 0)) for w, _ in widths),
        compiler_params=_cparams(("parallel",)),
        name="attn_prep",
    )(*args)


def _cache_kv_kernel(ckv_ref, kr_ref, wukv_ref, ka_ref, va_ref):
    _mla_kv(ckv_ref[...], kr_ref[...], wukv_ref, ka_ref, va_ref)


def _cache_kv(ckv, kr128, wukv, *, layer):
    b, _, p, _ = ckv.shape
    return pl.pallas_call(
        _cache_kv_kernel,
        out_shape=(jax.ShapeDtypeStruct((b, p, 4 * 2 * LANE), BF16), jax.ShapeDtypeStruct((b, p, 4 * MLA_V), BF16)),
        grid=(b,),
        in_specs=[pl.BlockSpec((None, None, p, MLA_KV_LORA), lambda i: (i, layer, 0, 0)),
                  pl.BlockSpec((None, None, p, LANE), lambda i: (i, layer, 0, 0)),
                  _layer_spec(wukv, layer, 1)],
        out_specs=(pl.BlockSpec((None, p, 4 * 2 * LANE), lambda i: (i, 0, 0)),
                   pl.BlockSpec((None, p, 4 * MLA_V), lambda i: (i, 0, 0))),
        compiler_params=_cparams(("parallel",)),
        name="mla_cache_kv",
    )(ckv, kr128, wukv)


def _attn_kernel(q_ref, *refs, nseg, hp, g, dk, dv):
    o_ref = refs[-1]
    for j in range(hp):
        kv = j // g
        q = q_ref[:, j * dk:(j + 1) * dk]
        ks = [refs[2 * i][:, kv * dk:(kv + 1) * dk].astype(BF16) for i in range(nseg)]
        vs = [refs[2 * i + 1][:, kv * dv:(kv + 1) * dv].astype(BF16) for i in range(nseg)]
        ss = [lax.dot_general(q, k, (((1,), (1,)), ((), ())), preferred_element_type=F32) for k in ks]
        m = functools.reduce(jnp.maximum, [jnp.max(s, axis=-1, keepdims=True) for s in ss])
        ps = [jnp.exp2(s - m) for s in ss]
        l = functools.reduce(jnp.add, [jnp.sum(p, axis=-1, keepdims=True) for p in ps])
        o = functools.reduce(jnp.add, [jnp.dot(p.astype(BF16), v, preferred_element_type=F32)
                                       for p, v in zip(ps, vs)])
        o_ref[:, j * dv:(j + 1) * dv] = o / l


def _attention(q, segs, *, heads, kv_heads, dk, dv, tq_pref=512, hp=4):
    b, sq, _ = q.shape
    tq = _pick(sq, tq_pref)
    g = heads // kv_heads
    assert heads % hp == 0 and (hp % g == 0 or g % hp == 0)
    kvp = max(1, hp // g)
    kcol = (lambda hb: hb) if hp >= g else (lambda hb: hb * hp // g)
    in_specs = [pl.BlockSpec((None, tq, hp * dk), lambda bi, hb, i: (bi, i, hb))]
    args = [q]
    for k, v, layer in segs:
        sk = k.shape[-2]
        if layer is None:
            in_specs += [pl.BlockSpec((None, sk, kvp * dk), lambda bi, hb, i: (bi, 0, kcol(hb))),
                         pl.BlockSpec((None, sk, kvp * dv), lambda bi, hb, i: (bi, 0, kcol(hb)))]
        else:
            in_specs += [pl.BlockSpec((None, None, sk, kvp * dk),
                                      lambda bi, hb, i, layer=layer: (bi, layer, 0, kcol(hb))),
                         pl.BlockSpec((None, None, sk, kvp * dv),
                                      lambda bi, hb, i, layer=layer: (bi, layer, 0, kcol(hb)))]
        args += [k, v]
    return pl.pallas_call(
        functools.partial(_attn_kernel, nseg=len(segs), hp=hp, g=min(g, hp), dk=dk, dv=dv),
        out_shape=jax.ShapeDtypeStruct((b, sq, heads * dv), F32),
        grid=(b, heads // hp, sq // tq),
        in_specs=in_specs,
        out_specs=pl.BlockSpec((None, tq, hp * dv), lambda bi, hb, i: (bi, i, hb)),
        compiler_params=_cparams(("parallel", "parallel", "parallel")),
        name="attention",
    )(*args)


def _log_sigmoid(x):
    return jnp.minimum(x, 0.0) - jnp.log1p(jnp.exp(-jnp.abs(x)))


def _mlstm_chunk_operands(q_ref, k_ref, v_ref, gate_ref, bias_ref):
    nh, hd = MLSTM_HEADS, MLSTM_HEAD_DIM
    g = gate_ref[...] + bias_ref[...]
    lf = _log_sigmoid(g)
    lf_hi = lf.astype(BF16)
    rem = lf - lf_hi.astype(F32)
    lf_mid = rem.astype(BF16)
    lf_lo = (rem - lf_mid.astype(F32)).astype(BF16)
    heads = []
    for h in range(nh):
        q = q_ref[:, h * hd:(h + 1) * hd]
        kh = k_ref[:, h * hd:(h + 1) * hd] * (hd ** -0.5)
        kt = kh.T
        qh = q.astype(BF16)
        heads.append(dict(q=q, qh=qh, kh=kh, kt=kt, vh=v_ref[:, h * hd:(h + 1) * hd].astype(BF16),
                          s=jnp.dot(qh, kt.astype(BF16), preferred_element_type=F32)))
    return dict(g=g, gt=g.T, lf=(lf_hi, lf_mid, lf_lo), heads=heads)


def _prefix_max_lanes(x, rev):
    n = x.shape[1]
    lane = lax.broadcasted_iota(jnp.int32, x.shape, 1)
    k = 1
    while k < n:
        if rev:
            shifted = jnp.where(lane < n - k, pltpu.roll(x, n - k, axis=1), -jnp.inf)
        else:
            shifted = jnp.where(lane >= k, pltpu.roll(x, k, axis=1), -jnp.inf)
        x = jnp.maximum(x, shifted)
        k *= 2
    return x


def _mlstm_chain(ops, h_ref, c_s, n_s, m_s, *, direction):
    nh, hd = MLSTM_HEADS, MLSTM_HEAD_DIM
    rev = direction == 1
    g, gt = ops["g"], ops["gt"]
    cl = g.shape[0]
    ri = lax.broadcasted_iota(jnp.int32, (cl, cl), 0)
    ci = lax.broadcasted_iota(jnp.int32, (cl, cl), 1)
    mask = (ri <= ci) if rev else (ri >= ci)
    tri = jnp.where(mask, 1.0, 0.0).astype(BF16)
    cum = functools.reduce(jnp.add, [jnp.dot(tri, part, preferred_element_type=F32) for part in ops["lf"]])
    cumt = cum.T
    last = 0 if rev else cl - 1
    li0 = MG_LANE + direction * 2 * nh
    lf0 = li0 + nh
    assert li0 % SUBLANE == 0 and 2 * nh == SUBLANE

    b8 = pltpu.roll(gt[li0:li0 + SUBLANE, :], nh, axis=0) - cumt[li0:li0 + SUBLANE, :]
    pm8 = _prefix_max_lanes(b8, rev)
    pm_rows = jnp.concatenate([jnp.zeros((li0, cl), F32), pm8, jnp.zeros((LANE - li0 - SUBLANE, cl), F32)], axis=0)
    pm_c = pm_rows.T
    lane = lax.broadcasted_iota(jnp.int32, (1, LANE), 1)
    m_lanes = functools.reduce(jnp.add, [jnp.where(lane == lf0 + h, m_s[direction, h:h + 1, :], 0.0)
                                         for h in range(nh)])
    mx_all = jnp.maximum(m_lanes, pm_c)
    interw_all = jnp.exp(m_lanes - mx_all)
    em_all = jnp.exp(-(cum + mx_all))

    for h in range(nh):
        hv = ops["heads"][h]
        cum_c = cum[:, lf0 + h:lf0 + h + 1]
        cum_r = cumt[lf0 + h:lf0 + h + 1, :]
        li_r = gt[li0 + h:li0 + h + 1, :]
        li_c = g[:, li0 + h:li0 + h + 1]
        m_prev = m_s[direction, h:h + 1, 0:1]
        w = jnp.where(mask, jnp.exp(b8[nh + h:nh + h + 1, :] - mx_all[:, lf0 + h:lf0 + h + 1]), 0.0)
        sc = hv["s"] * w
        inter_w = interw_all[:, lf0 + h:lf0 + h + 1]
        cmat = c_s[direction, h]
        nrow = n_s[direction, h:h + 1, :]
        num = (jnp.dot(sc.astype(BF16), hv["vh"], preferred_element_type=F32)
               + inter_w * jnp.dot(hv["qh"], cmat.astype(BF16), preferred_element_type=F32))
        qn = jnp.sum(hv["q"] * nrow, axis=1, keepdims=True)
        den = jnp.sum(sc, axis=1, keepdims=True) + inter_w * qn
        h_ref[:, h * hd:(h + 1) * hd] = num / jnp.maximum(jnp.abs(den), em_all[:, lf0 + h:lf0 + h + 1])

        tot = cum_c[last:last + 1, :]
        g_r = tot - cum_r + li_r
        g_c = tot - cum_c + li_c
        m_new = jnp.maximum(tot + m_prev, jnp.max(g_r, axis=1, keepdims=True))
        decay = jnp.exp(tot + m_prev - m_new)
        ws_r = jnp.exp(g_r - m_new)
        ws_c = jnp.exp(g_c - m_new)
        c_s[direction, h] = decay * cmat + jnp.dot((hv["kt"] * ws_r).astype(BF16), hv["vh"],
                                                   preferred_element_type=F32)
        n_s[direction, h:h + 1, :] = decay * nrow + jnp.sum(hv["kh"] * ws_c, axis=0, keepdims=True)
        m_s[direction, h:h + 1, :] = jnp.broadcast_to(m_new, (1, hd))


def _mlstm_kernel(*refs, shared, has_init, emit_state):
    nin = 4 if shared else 8
    fwd_in = refs[:4]
    bwd_in = fwd_in if shared else refs[4:8]
    bias_ref = refs[nin]
    pos = nin + 1
    if has_init:
        c0_ref, n0_ref, m0_ref = refs[pos:pos + 3]
        pos += 3
    hf_ref, hb_ref = refs[pos:pos + 2]
    pos += 2
    if emit_state:
        co_ref, no_ref, mo_ref = refs[pos:pos + 3]
        pos += 3
    c_s, n_s, m_s = refs[pos:pos + 3]
    c = pl.program_id(1)

    @pl.when(c == 0)
    def _():
        if has_init:
            c_s[...] = c0_ref[...]
            n_s[...] = n0_ref[...]
            m_s[...] = m0_ref[...]
        else:
            c_s[...] = jnp.zeros_like(c_s)
            n_s[...] = jnp.zeros_like(n_s)
            m_s[...] = jnp.zeros_like(m_s)

    ops_f = _mlstm_chunk_operands(*fwd_in, bias_ref)
    ops_b = ops_f if shared else _mlstm_chunk_operands(*bwd_in, bias_ref)
    _mlstm_chain(ops_f, hf_ref, c_s, n_s, m_s, direction=0)
    _mlstm_chain(ops_b, hb_ref, c_s, n_s, m_s, direction=1)

    if emit_state:
        @pl.when(c == pl.num_programs(1) - 1)
        def _():
            co_ref[...] = c_s[...]
            no_ref[...] = n_s[...]
            mo_ref[...] = m_s[...]


def _mlstm(proj, bias128, init, *, layer, batch, seq, emit_state, chunk_pref=256):
    n = proj.shape[0]
    cl = _pick(seq, chunk_pref)
    nc = seq // cl
    shared = nc == 1
    nh, hd = MLSTM_HEADS, MLSTM_HEAD_DIM
    rows = (lambda b, c: b * nc + c), (lambda b, c: b * nc + (nc - 1 - c))

    def chunk_specs(row):
        qkv = lambda name: pl.BlockSpec((cl, nh * hd), lambda b, c: (row(b, c), _COL[name] // (nh * hd)))
        return [qkv("mq"), qkv("mk"), qkv("mv"), pl.BlockSpec((cl, LANE), lambda b, c: (row(b, c), KRMG_BLOCK))]

    in_specs = chunk_specs(rows[0]) + ([] if shared else chunk_specs(rows[1]))
    in_specs.append(pl.BlockSpec((None, 1, LANE), lambda b, c: (layer, 0, 0)))
    args = [proj] * (len(in_specs) - 1) + [bias128]
    has_init = init is not None
    if has_init:
        c0, n0, m0 = init
        in_specs += [pl.BlockSpec((None, None, 2, nh, hd, hd), lambda b, c: (b, layer, 0, 0, 0, 0)),
                     pl.BlockSpec((None, None, 2, nh, hd), lambda b, c: (b, layer, 0, 0, 0)),
                     pl.BlockSpec((None, None, 2, nh, hd), lambda b, c: (b, layer, 0, 0, 0))]
        args += [c0, n0, m0]
    out_shape = [jax.ShapeDtypeStruct((n, nh * hd), F32)] * 2
    out_specs = [pl.BlockSpec((cl, nh * hd), lambda b, c, row=row: (row(b, c), 0)) for row in rows]
    if emit_state:
        out_shape += [jax.ShapeDtypeStruct((batch, 2, nh, hd, hd), F32), jax.ShapeDtypeStruct((batch, 2, nh, hd), F32),
                      jax.ShapeDtypeStruct((batch, 2, nh, hd), F32)]
        out_specs += [pl.BlockSpec((None, 2, nh, hd, hd), lambda b, c: (b, 0, 0, 0, 0)),
                      pl.BlockSpec((None, 2, nh, hd), lambda b, c: (b, 0, 0, 0)),
                      pl.BlockSpec((None, 2, nh, hd), lambda b, c: (b, 0, 0, 0))]
    return pl.pallas_call(
        functools.partial(_mlstm_kernel, shared=shared, has_init=has_init, emit_state=emit_state),
        out_shape=tuple(out_shape),
        grid=(batch, nc),
        in_specs=in_specs,
        out_specs=tuple(out_specs),
        scratch_shapes=[pltpu.VMEM((2, nh, hd, hd), F32), pltpu.VMEM((2, nh, hd), F32), pltpu.VMEM((2, nh, hd), F32)],
        compiler_params=_cparams(("parallel", "arbitrary")),
        name="mlstm",
    )(*args)


_S5_STATE_TILES = 2 * S5_NS // MXU_TILE
_S5_GROUPS_PER_TILE = MXU_TILE // S5_STATE


def _s5_channel_tile(state_tile):
    first_group = (state_tile % (S5_NS // MXU_TILE)) * _S5_GROUPS_PER_TILE
    return first_group * S5_GROUP // MXU_TILE


def _to_time_major_kernel(u_ref, p_ref, o_ref):
    rows = p_ref.shape[0]
    sg, ch = u_ref.shape[0], u_ref.shape[2]
    tsteps = rows // sg
    for k in range(u_ref.shape[1] // tsteps):
        u = u_ref[:, k * tsteps:(k + 1) * tsteps, :].reshape(rows, ch).astype(BF16)
        o_ref[k * rows:(k + 1) * rows, :] = jnp.dot(p_ref[...], u, preferred_element_type=F32).astype(BF16)


def _from_time_major_kernel(yf_ref, yb_ref, pt_ref, o_ref):
    rows = pt_ref.shape[0]
    sg, ch = o_ref.shape[0], o_ref.shape[2]
    tsteps = rows // sg
    pt = pt_ref[...]
    for k in range(o_ref.shape[1] // tsteps):
        y = yf_ref[k * rows:(k + 1) * rows, :] + yb_ref[k * rows:(k + 1) * rows, :]
        hi = y.astype(BF16)
        lo = (y - hi.astype(F32)).astype(BF16)
        out = jnp.dot(pt, hi, preferred_element_type=F32) + jnp.dot(pt, lo, preferred_element_type=F32)
        o_ref[:, k * tsteps:(k + 1) * tsteps, :] = out.reshape(sg, tsteps, ch)


def _s5_kernel(*refs, has_init):
    if has_init:
        u_ref, bd_ref, a_ref, cd_ref, x0_ref, y_ref, xf_ref, bu_s, x_s = refs
    else:
        u_ref, bd_ref, a_ref, cd_ref, y_ref, xf_ref, bu_s, x_s = refs
    d = pl.program_id(0)
    c = pl.program_id(2)
    ns = S5_NS
    sg = x_s.shape[0]
    tsteps = u_ref.shape[0] // sg
    mt = MXU_TILE

    @pl.when(c == 0)
    def _():
        if has_init:
            x_s[...] = x0_ref[...]
        else:
            x_s[...] = jnp.zeros_like(x_s)

    u = u_ref[...]
    for st in range(_S5_STATE_TILES):
        ct = _s5_channel_tile(st)
        bu_s[:, st * mt:(st + 1) * mt] = jnp.dot(u[:, ct * mt:(ct + 1) * mt],
                                                 bd_ref[ct * mt:(ct + 1) * mt, st * mt:(st + 1) * mt],
                                                 preferred_element_type=F32)
    a_re = jnp.broadcast_to(a_ref[:, :ns], (sg, ns))
    a_im = jnp.broadcast_to(a_ref[:, ns:], (sg, ns))

    def body(t, carry):
        xr, xi = carry
        tt = t + d * (tsteps - 1 - 2 * t)
        r0 = pl.multiple_of(tt * sg, sg)
        br = bu_s[pl.ds(r0, sg), :ns]
        bi = bu_s[pl.ds(r0, sg), ns:]
        nr = a_re * xr - a_im * xi + br
        ni = a_re * xi + a_im * xr + bi
        bu_s[pl.ds(r0, sg), :ns] = nr
        bu_s[pl.ds(r0, sg), ns:] = ni
        return nr, ni

    xr, xi = lax.fori_loop(0, tsteps, body, (x_s[:, :ns], x_s[:, ns:]), unroll=SCAN_UNROLL)
    x_s[:, :ns] = xr
    x_s[:, ns:] = xi

    for ct in range(S5_CH // mt):
        acc = None
        for st in range(_S5_STATE_TILES):
            if _s5_channel_tile(st) != ct:
                continue
            t = jnp.dot(bu_s[:, st * mt:(st + 1) * mt].astype(BF16),
                        cd_ref[st * mt:(st + 1) * mt, ct * mt:(ct + 1) * mt], preferred_element_type=F32)
            acc = t if acc is None else acc + t
        y_ref[:, ct * mt:(ct + 1) * mt] = acc

    @pl.when(c == pl.num_programs(2) - 1)
    def _():
        xf_ref[...] = x_s[...]


def _s5(proj3, pr, x0, *, layer, rows_pref=512, scan_rows_pref=1024):
    batch, seq, _ = proj3.shape
    sg = SUBLANE if batch % SUBLANE == 0 else batch
    ng = batch // sg
    tsteps = _pick(seq, max(SUBLANE, rows_pref // sg))
    rows = sg * tsteps
    nc = seq // tsteps
    r = jnp.arange(rows)
    perm = (jnp.arange(rows)[None, :] == ((r % sg) * tsteps + r // sg)[:, None]).astype(BF16)
    const2 = lambda a: pl.BlockSpec(a.shape, lambda g, c: (0,) * a.ndim, pipeline_mode=pl.Buffered(1))
    nsub = math.gcd(nc, RELAYOUT_SUB)
    nc_r = nc // nsub

    u_tm = pl.pallas_call(
        _to_time_major_kernel,
        out_shape=jax.ShapeDtypeStruct((ng, seq * sg, S5_CH), BF16),
        grid=(ng, nc_r),
        in_specs=[pl.BlockSpec((sg, nsub * tsteps, S5_CH), lambda g, c: (g, c, _COL["su"] // S5_CH)), const2(perm)],
        out_specs=pl.BlockSpec((None, nsub * rows, S5_CH), lambda g, c: (g, c, 0)),
        compiler_params=_cparams(("parallel", "parallel")),
        name="s5_to_time_major",
    )(proj3, perm)

    ts_scan = _pick(seq, max(SUBLANE, scan_rows_pref // sg))
    rows_scan = sg * ts_scan
    nc_scan = seq // ts_scan
    cpos = lambda d, c: c + d * (nc_scan - 1 - 2 * c)
    has_init = x0 is not None
    in_specs = [pl.BlockSpec((None, rows_scan, S5_CH), lambda d, g, c: (g, cpos(d, c), 0)),
                pl.BlockSpec((None, None, S5_CH, 2 * S5_NS), lambda d, g, c: (layer, d, 0, 0)),
                pl.BlockSpec((None, None, 1, 2 * S5_NS), lambda d, g, c: (layer, d, 0, 0)),
                _layer_spec(pr["s5_cd"], layer, 3)]
    args = [u_tm, pr["s5_bd"], pr["s5_abar"], pr["s5_cd"]]
    if has_init:
        in_specs.append(pl.BlockSpec((None, sg, 2 * S5_NS), lambda d, g, c: (d, g, 0)))
        args.append(x0)
    y_tm, xfin = pl.pallas_call(
        functools.partial(_s5_kernel, has_init=has_init),
        out_shape=(jax.ShapeDtypeStruct((2, ng, seq * sg, S5_CH), F32),
                   jax.ShapeDtypeStruct((2, batch, 2 * S5_NS), F32)),
        grid=(2, ng, nc_scan),
        in_specs=in_specs,
        out_specs=(pl.BlockSpec((None, None, rows_scan, S5_CH), lambda d, g, c: (d, g, cpos(d, c), 0)),
                   pl.BlockSpec((None, sg, 2 * S5_NS), lambda d, g, c: (d, g, 0))),
        scratch_shapes=[pltpu.VMEM((rows_scan, 2 * S5_NS), F32), pltpu.VMEM((sg, 2 * S5_NS), F32)],
        compiler_params=_cparams(("parallel", "parallel", "arbitrary")),
        name="s5_scan",
    )(*args)

    y = pl.pallas_call(
        _from_time_major_kernel,
        out_shape=jax.ShapeDtypeStruct((batch, seq, S5_CH), F32),
        grid=(ng, nc_r),
        in_specs=[pl.BlockSpec((None, None, nsub * rows, S5_CH), lambda g, c: (0, g, c, 0)),
                  pl.BlockSpec((None, None, nsub * rows, S5_CH), lambda g, c: (1, g, c, 0)),
                  const2(perm)],
        out_specs=pl.BlockSpec((sg, nsub * tsteps, S5_CH), lambda g, c: (g, c, 0)),
        compiler_params=_cparams(("parallel", "parallel")),
        name="s5_from_time_major",
    )(y_tm, y_tm, perm.T)
    return y, xfin


def _gelu_tanh(x):
    return 0.5 * x * (1.0 + jnp.tanh(math.sqrt(2.0 / math.pi) * (x + 0.044715 * (x * x * x))))


def _outproj_kernel(x_ref, mod_ref, oa_ref, ob_ref, hf_ref, hb_ref, mo_ref, y_ref, su_ref,
                    d_ref, wglu_ref, on_ref, wout_ref, o_ref):
    slab = math.gcd(x_ref.shape[0], OUTPROJ_SLAB)
    for r in range(x_ref.shape[0] // slab):
        rows = slice(r * slab, (r + 1) * slab)
        oc = jax.nn.sigmoid(mo_ref[rows, :]) * (hf_ref[rows, :] + hb_ref[rows, :])
        y = _gelu_tanh(y_ref[rows, :] + d_ref[...] * su_ref[rows, :])
        od = y * jax.nn.sigmoid(jnp.dot(y.astype(BF16), wglu_ref[...], preferred_element_type=F32))
        acc = None
        for gi, part in enumerate((oa_ref[rows, :], ob_ref[rows, :], oc, od)):
            nrm = _rms(part, on_ref[gi:gi + 1, :]).astype(BF16)
            t = jnp.dot(nrm, wout_ref[gi * GROUP_WIDTH:(gi + 1) * GROUP_WIDTH, :], preferred_element_type=F32)
            acc = t if acc is None else acc + t
        o_ref[rows, :] = x_ref[rows, :] + mod_ref[5:6, :] * acc


def _outproj(x, mod, oa, ob, hf, hb, proj, y, pr, *, layer, seq, tm_pref=512):
    n, d = x.shape
    groups = mod.shape[0]
    tm = _pick(seq if groups > 1 else n, tm_pref)
    per = seq // tm if groups > 1 else 1
    mod_map = (lambda i: (i // per, 0, 0)) if groups > 1 else (lambda i: (0, 0, 0))
    gw = GROUP_WIDTH
    rowblk = pl.BlockSpec((tm, gw), lambda i: (i, 0))
    ws = [pr["s5_d"], pr["s5_w_glu"], pr["out_norm"], pr["w_out"]]
    return pl.pallas_call(
        _outproj_kernel,
        out_shape=jax.ShapeDtypeStruct((n, d), F32),
        grid=(n // tm,),
        in_specs=[pl.BlockSpec((tm, d), lambda i: (i, 0)),
                  pl.BlockSpec((None, N_MOD, d), mod_map),
                  rowblk, rowblk, rowblk, rowblk,
                  pl.BlockSpec((tm, gw), lambda i: (i, _COL["mo"] // gw)),
                  rowblk,
                  pl.BlockSpec((tm, gw), lambda i: (i, _COL["su"] // gw))] + [_layer_spec(w, layer, 1) for w in ws],
        out_specs=pl.BlockSpec((tm, d), lambda i: (i, 0)),
        compiler_params=_cparams(("parallel",)),
        name="merge_out_proj",
    )(x, mod, oa, ob, hf, hb, proj, y, proj, *ws)


def _trunk_layer(x, mod, pr, *, layer, batch, seq, tables, cache, final_g=None):
    latent = cache is not None
    x = _ffn(x, mod, pr, layer=layer, which=0, seq=seq)
    proj = _inproj(x, mod, pr, layer=layer, seq=seq)

    prep = _attn_prep(proj, tables, pr, layer=layer, seq=seq)
    qa, ka, va, qb, kb, vb = prep[:6]
    r3 = lambda a: a.reshape(batch, seq, a.shape[-1])
    segs_a = [(r3(ka), r3(va), None)]
    segs_b = [(r3(kb), r3(vb), None)]
    if latent:
        segs_a.insert(0, cache["mla_kv"] + (None,))
        segs_b.insert(0, cache["gqa_kv"] + (layer,))
    oa = _attention(r3(qa), segs_a, heads=MLA_HEADS, kv_heads=MLA_HEADS, dk=2 * LANE, dv=MLA_V)
    ob = _attention(r3(qb), segs_b, heads=GQA_HEADS, kv_heads=GQA_KV_HEADS, dk=GQA_HEAD_DIM, dv=GQA_HEAD_DIM)
    oa = oa.reshape(batch * seq, -1)
    ob = ob.reshape(batch * seq, -1)

    m_init = cache["mlstm"] if latent else None
    mres = _mlstm(proj, pr["mlstm_bias"], m_init, layer=layer, batch=batch, seq=seq, emit_state=not latent)

    y, xfin = _s5(proj.reshape(batch, seq, PROJ_COLS), pr, cache["s5"] if latent else None, layer=layer)

    x = _outproj(x, mod, oa, ob, mres[0], mres[1], proj, y.reshape(batch * seq, S5_CH), pr,
                 layer=layer, seq=seq)
    x = _ffn(x, mod, pr, layer=layer, which=1, seq=seq, final_g=final_g)

    new_ctx = None
    if not latent:
        ckvn, kbn = prep[6:]
        kr = proj[:, _COL["kr"]:_COL["kr"] + MLA_ROPE]
        gv = proj[:, _COL["gv"]:_COL["gv"] + GQA_KV_HEADS * GQA_HEAD_DIM]
        xs = xfin.reshape(2, batch, 2, S5_GROUPS, S5_STATE).transpose(1, 0, 2, 3, 4)
        new_ctx = (ckvn.reshape(batch, seq, MLA_KV_LORA),
                   kr.reshape(batch, seq, MLA_ROPE),
                   kbn.reshape(batch, seq, GQA_KV_HEADS, GQA_HEAD_DIM),
                   gv.reshape(batch, seq, GQA_KV_HEADS, GQA_HEAD_DIM),
                   mres[2], mres[3], mres[4][..., 0],
                   xs[:, :, 0], xs[:, :, 1])
    return x, new_ctx


def _permute_w_in(w_in):
    parts = [w_in[..., _ORIG[n][0]:_ORIG[n][0] + _ORIG[n][1]] for n in _ORDER]
    pad = PROJ_COLS - sum(p.shape[-1] for p in parts)
    parts.append(jnp.zeros(w_in.shape[:-1] + (pad,), w_in.dtype))
    return jnp.concatenate(parts, axis=-1).astype(BF16)


def _permute_w_uq(w_uq):
    depth, k, _ = w_uq.shape
    w = w_uq.reshape(depth, k, MLA_HEADS, MLA_NOPE + MLA_ROPE)
    w = jnp.pad(w, ((0, 0), (0, 0), (0, 0), (0, 2 * LANE - MLA_NOPE - MLA_ROPE)))
    return w.reshape(depth, k, MLA_HEADS * 2 * LANE).astype(BF16)


def kernel(x_prompt, x_sample, cache_mla_ckv, cache_mla_krope, cache_gqa_k, cache_gqa_v, state_mlstm_c, state_mlstm_n, state_mlstm_m, state_s5_re, state_s5_im, c, c_ctx, ada_w, ada_b, norm_g, ffn_w13, ffn_w2, w_in, mla_q_norm, mla_kv_norm, mla_w_uq, mla_w_ukv, gqa_q_norm, gqa_k_norm, mlstm_gate_b, s5_a_re, s5_a_im, s5_log_dt, s5_b_re, s5_b_im, s5_c_re, s5_c_im, s5_d, s5_w_glu, out_norm, w_out, final_norm):
    bc, sc, d = x_prompt.shape
    bl, sl, _ = x_sample.shape
    depth = ada_w.shape[0]
    past = cache_mla_ckv.shape[2]

    rows = ((1 + bl + SUBLANE - 1) // SUBLANE) * SUBLANE
    cvecs = jnp.concatenate([c_ctx[None, :], c, jnp.zeros((rows - 1 - bl, d), F32)], axis=0)
    mod_all = _modulation(cvecs, ada_w, ada_b).reshape(depth, rows, N_MOD, d)

    tables = _rope_tables(sl)

    abar, bd, cd = _s5_params(s5_a_re, s5_a_im, s5_log_dt, s5_b_re, s5_b_im, s5_c_re, s5_c_im)
    nmg = MLSTM_HEADS * 4
    row = lambda a: a.reshape(depth, 1, a.shape[-1])
    pr = {"norm_g": norm_g.reshape(depth, 3, 1, d),
          "ffn_w13": ffn_w13.astype(BF16), "ffn_w2": ffn_w2.astype(BF16),
          "w_in": _permute_w_in(w_in),
          "mla_q_norm": row(mla_q_norm), "mla_kv_norm": row(mla_kv_norm),
          "mla_w_uq": _permute_w_uq(mla_w_uq), "mla_w_ukv": mla_w_ukv.astype(BF16),
          "gqa_q_norm": row(gqa_q_norm), "gqa_k_norm": row(gqa_k_norm),
          "mlstm_bias": jnp.pad(mlstm_gate_b.reshape(depth, 1, nmg), ((0, 0), (0, 0), (MG_LANE, LANE - MG_LANE - nmg))),
          "s5_abar": abar, "s5_bd": bd, "s5_cd": cd,
          "s5_d": row(s5_d), "s5_w_glu": s5_w_glu.astype(BF16),
          "out_norm": out_norm.reshape(depth, 4, GROUP_WIDTH), "w_out": w_out.astype(BF16)}

    gkv = GQA_KV_HEADS * GQA_HEAD_DIM
    cache_k = cache_gqa_k.reshape(bl, depth, past, gkv)
    cache_v = cache_gqa_v.reshape(bl, depth, past, gkv)
    m0 = jnp.broadcast_to(state_mlstm_m[..., None], state_mlstm_n.shape)
    kr_pad = jnp.pad(cache_mla_krope, ((0, 0), (0, 0), (0, 0), (0, LANE - MLA_ROPE)))

    x_ctx = x_prompt.reshape(bc * sc, d)
    x_lat = x_sample.reshape(bl * sl, d)
    per_layer = []
    for l in range(depth):
        final_g = final_norm[None, :] if l == depth - 1 else None
        x_ctx, ctx_l = _trunk_layer(x_ctx, mod_all[l, 0:1], pr, layer=l, batch=bc, seq=sc, tables=None, cache=None,
                                    final_g=final_g)
        per_layer.append(ctx_l)

        x0 = jnp.concatenate([state_s5_re[:, l].reshape(bl, 2, S5_NS), state_s5_im[:, l].reshape(bl, 2, S5_NS)],
                             axis=-1).transpose(1, 0, 2)
        cache = {"mla_kv": tuple(_cache_kv(cache_mla_ckv, kr_pad, pr["mla_w_ukv"], layer=l)),
                 "gqa_kv": (cache_k, cache_v),
                 "mlstm": (state_mlstm_c, state_mlstm_n, m0),
                 "s5": x0}
        x_lat, _ = _trunk_layer(x_lat, mod_all[l, 1:1 + bl], pr, layer=l, batch=bl, seq=sl, tables=tables,
                                cache=cache, final_g=final_g)

    new_ctx = [jnp.stack([t[i] for t in per_layer], axis=1) for i in range(9)]
    return (x_ctx.reshape(bc, sc, d), x_lat.reshape(bl, sl, d), *new_ctx)
```

```python
import functools
import math

import jax
import jax.numpy as jnp
from jax import lax
from jax.experimental import pallas as pl
from jax.experimental.pallas import tpu as pltpu

F32 = jnp.float32
BF16 = jnp.bfloat16

EPS = 1e-6
ROPE_THETA = 10000.0
GRID_W = 64
N_MOD = 9
LOG2E = math.log2(math.e)

D_FF = 5632
MLA_HEADS, MLA_NOPE, MLA_ROPE, MLA_V = 4, 128, 64, 128
MLA_Q_LORA, MLA_KV_LORA = 384, 256
GQA_HEADS, GQA_KV_HEADS, GQA_HEAD_DIM = 4, 2, 128
MLSTM_HEADS, MLSTM_HEAD_DIM = 4, 128
S5_GROUPS, S5_GROUP, S5_STATE = 32, 16, 64
S5_CH = S5_GROUPS * S5_GROUP
S5_NS = S5_GROUPS * S5_STATE
GROUP_WIDTH = 512

LANE = 128
SUBLANE = 8
MXU_TILE = 256
NORM_ROWS = 128
FFN_TILE = 512
RELAYOUT_SUB = 4
OUTPROJ_SLAB = 128
SCAN_UNROLL = 4
ATTN_SUB_ROWS = 256
VMEM_BYTES = 64 * 1024 * 1024
VMEM_LIMIT_BYTES = VMEM_BYTES - 8 * 1024 * 1024
FFN_VMEM_LIMIT_BYTES = VMEM_BYTES - 4 * 1024 * 1024

_ORIG = dict(cq=(0, 384), ckv=(384, 256), kr=(640, 64), gq=(704, 512), gk=(1216, 256), gv=(1472, 256),
             mq=(1728, 512), mk=(2240, 512), mv=(2752, 512), mo=(3264, 512), mg=(3776, 16), su=(3792, 512))
_ORDER = ("mq", "mk", "mv", "mo", "gq", "su", "gk", "gv", "ckv", "cq", "kr", "mg")
_COL = {}
_off = 0
for _n in _ORDER:
    _COL[_n] = _off
    _off += _ORIG[_n][1]
PROJ_COLS = ((_off + LANE - 1) // LANE) * LANE
KRMG_BLOCK = _COL["kr"] // LANE
MG_LANE = _COL["mg"] - _COL["kr"]


def _cparams(sem, vmem_limit=VMEM_LIMIT_BYTES):
    return pltpu.CompilerParams(dimension_semantics=sem, vmem_limit_bytes=vmem_limit)


def _pick(n, pref):
    for t in range(min(n, pref), 0, -1):
        if n % t == 0 and (t % SUBLANE == 0 or t == n):
            return t
    return n


def _layer_spec(arr, layer, nargs):
    zeros = (0,) * (arr.ndim - 1)
    imap = {1: lambda i: (layer,) + zeros, 2: lambda i, j: (layer,) + zeros,
            3: lambda i, j, k: (layer,) + zeros}[nargs]
    return pl.BlockSpec((None,) + arr.shape[1:], imap, pipeline_mode=pl.Buffered(1))


def _rms(x, g):
    return x * lax.rsqrt(jnp.mean(x * x, axis=-1, keepdims=True) + EPS) * g


def _modulate(x, g, scale, shift):
    return x * lax.rsqrt(jnp.mean(x * x, axis=-1, keepdims=True) + EPS) * (g * (1.0 + scale)) + shift


def _swap_halves(x, half):
    w = x.shape[-1]
    lane = lax.broadcasted_iota(jnp.int32, x.shape, x.ndim - 1)
    first = (lane & (2 * half - 1)) < half
    return jnp.where(first, pltpu.roll(x, w - half, axis=x.ndim - 1), pltpu.roll(x, half, axis=x.ndim - 1))


def _silu(x):
    return x * jax.nn.sigmoid(x)


def _mod_kernel(c_ref, w_ref, b_ref, o_ref):
    a = _silu(c_ref[...]).astype(BF16)
    o_ref[...] = jnp.dot(a, w_ref[...].astype(BF16), preferred_element_type=F32) + b_ref[...]


def _modulation(cvecs, ada_w, ada_b):
    depth, d, nm = ada_w.shape
    r = cvecs.shape[0]
    tn = _pick(nm, 1024)
    return pl.pallas_call(
        _mod_kernel,
        out_shape=jax.ShapeDtypeStruct((depth, r, nm), F32),
        grid=(depth, nm // tn),
        in_specs=[pl.BlockSpec((r, d), lambda l, j: (0, 0)),
                  pl.BlockSpec((None, d, tn), lambda l, j: (l, 0, j)),
                  pl.BlockSpec((None, 1, tn), lambda l, j: (l, 0, j))],
        out_specs=pl.BlockSpec((None, r, tn), lambda l, j: (l, 0, j)),
        compiler_params=_cparams(("parallel", "parallel")),
        name="adaln_mod",
    )(cvecs, ada_w, ada_b.reshape(depth, 1, nm))


def _rope_kernel(ca_ref, sa_ref, cb_ref, sb_ref, *, log2w):
    s = ca_ref.shape[0]
    t = lax.broadcasted_iota(jnp.int32, (s, 1), 0)
    row = lax.shift_right_logical(t, log2w).astype(F32)
    col = (t & ((1 << log2w) - 1)).astype(F32)

    def tables(width, off, rd):
        lane = lax.broadcasted_iota(jnp.int32, (1, width), 1)
        r = lane - off
        inr = (r >= 0) & (r < rd)
        half, quarter = rd // 2, rd // 4
        is_col = r >= half
        rr = jnp.where(is_col, r - half, r)
        second = rr >= quarter
        j = jnp.where(second, rr - quarter, rr).astype(F32)
        inv = jnp.exp(j * (-2.0 / half * math.log(ROPE_THETA)))
        ang = jnp.where(is_col, col, row) * inv
        sign = jnp.where(second, 1.0, -1.0)
        return jnp.where(inr, jnp.cos(ang), 1.0), jnp.where(inr, sign * jnp.sin(ang), 0.0)

    ca, sa = tables(2 * LANE, MLA_NOPE, MLA_ROPE)
    cb, sb = tables(GQA_HEAD_DIM, 0, GQA_HEAD_DIM)
    ca_ref[...] = ca
    sa_ref[...] = sa
    cb_ref[...] = cb
    sb_ref[...] = sb


def _rope_tables(s):
    log2w = GRID_W.bit_length() - 1
    assert 1 << log2w == GRID_W
    shp = lambda w: jax.ShapeDtypeStruct((s, w), F32)
    return pl.pallas_call(
        functools.partial(_rope_kernel, log2w=log2w),
        out_shape=(shp(2 * LANE), shp(2 * LANE), shp(GQA_HEAD_DIM), shp(GQA_HEAD_DIM)),
        name="rope_tables",
    )()


def _s5_params(a_re, a_im, log_dt, b_re, b_im, c_re, c_im):
    depth, nd = a_re.shape[:2]
    r = depth * nd
    are = a_re.reshape(r, S5_NS)
    aim = a_im.reshape(r, S5_NS)
    ldt = jnp.broadcast_to(log_dt[..., None], (depth, nd, S5_GROUPS, S5_STATE)).reshape(r, S5_NS)
    rep = lambda b: jnp.repeat(jnp.transpose(b, (0, 3, 1, 2)).reshape(depth, 1, S5_GROUP, S5_NS), nd, axis=1)
    bre = rep(b_re).reshape(r, S5_GROUP, S5_NS)
    bim = rep(b_im).reshape(r, S5_GROUP, S5_NS)
    v = jax.ShapeDtypeStruct((r, S5_NS), F32)
    m = jax.ShapeDtypeStruct((r, S5_GROUP, S5_NS), F32)
    ar, ai, bbr, bbi = pl.pallas_call(_s5_param_rows_kernel, out_shape=(v, v, m, m), name="s5_discretise")(
        are, aim, ldt, bre, bim)
    eye = jnp.eye(S5_GROUPS, dtype=F32)

    def dense_b(bb):
        bb = bb.reshape(r, S5_GROUP, S5_GROUPS, S5_STATE)
        return jnp.einsum("dcgp,gh->dgchp", bb, eye).reshape(r, S5_CH, S5_NS)

    def dense_c(cc):
        return jnp.einsum("lgcp,gh->lhpgc", cc, eye).reshape(depth, S5_NS, S5_CH)

    bd = jnp.concatenate([dense_b(bbr), dense_b(bbi)], axis=-1).astype(BF16).reshape(depth, nd, S5_CH, 2 * S5_NS)
    cd = jnp.concatenate([dense_c(c_re), -dense_c(c_im)], axis=1).astype(BF16)
    abar = jnp.concatenate([ar, ai], axis=-1).reshape(depth, nd, 1, 2 * S5_NS)
    return abar, bd, cd


def _s5_param_rows_kernel(are_ref, aim_ref, ldt_ref, bre_ref, bim_ref, oar_ref, oai_ref, obr_ref, obi_ref):
    lr, li = are_ref[...], aim_ref[...]
    dt = jnp.exp(ldt_ref[...])
    mag = jnp.exp(lr * dt)
    ar, ai = mag * jnp.cos(li * dt), mag * jnp.sin(li * dt)
    oar_ref[...] = ar
    oai_ref[...] = ai
    nr, ni = ar - 1.0, ai
    den = lr * lr + li * li
    fr, fi = (nr * lr + ni * li) / den, (ni * lr - nr * li) / den
    for d in range(fr.shape[0]):
        br, bi = bre_ref[d], bim_ref[d]
        obr_ref[d] = fr[d:d + 1] * br - fi[d:d + 1] * bi
        obi_ref[d] = fr[d:d + 1] * bi + fi[d:d + 1] * br


def _ffn_kernel(*refs, row0, slab, final):
    if final:
        x_ref, mod_ref, g_ref, w1_ref, w3_ref, w2_ref, fg_ref, o_ref, h_ref = refs
    else:
        x_ref, mod_ref, g_ref, w1_ref, w3_ref, w2_ref, o_ref, h_ref = refs
    j = pl.program_id(1)
    nslab = h_ref.shape[0] // slab
    nr = math.gcd(slab, NORM_ROWS)

    def partial_ffn(rows):
        h = h_ref[rows, :]
        gate = jnp.dot(h, w1_ref[...], preferred_element_type=F32)
        up = jnp.dot(h, w3_ref[...], preferred_element_type=F32)
        act = (_silu(gate) * up).astype(BF16)
        return jnp.dot(act, w2_ref[...], preferred_element_type=F32)

    @pl.when(j == 0)
    def _():
        for r in range(nslab):
            for p in range(slab // nr):
                rows = slice(r * slab + p * nr, r * slab + (p + 1) * nr)
                h = _modulate(x_ref[rows, :], g_ref[...], mod_ref[row0 + 1:row0 + 2, :], mod_ref[row0:row0 + 1, :])
                h_ref[rows, :] = h.astype(BF16)
            rows = slice(r * slab, (r + 1) * slab)
            o_ref[rows, :] = partial_ffn(rows)

    last = pl.num_programs(1) - 1

    @pl.when((j > 0) & (j < last))
    def _():
        for r in range(nslab):
            rows = slice(r * slab, (r + 1) * slab)
            o_ref[rows, :] += partial_ffn(rows)

    @pl.when(j == last)
    def _():
        gate = 0.5 * mod_ref[row0 + 2:row0 + 3, :]
        for r in range(nslab):
            rows = slice(r * slab, (r + 1) * slab)
            o_ref[rows, :] = x_ref[rows, :] + gate * (o_ref[rows, :] + partial_ffn(rows))
            if final:
                for p in range(slab // nr):
                    piece = slice(r * slab + p * nr, r * slab + (p + 1) * nr)
                    o_ref[piece, :] = _rms(o_ref[piece, :], fg_ref[...])


def _ffn(x, mod, pr, *, layer, which, seq, final_g=None, tm_pref=1024):
    n, d = x.shape
    final = final_g is not None
    groups = mod.shape[0]
    tm = _pick(seq if groups > 1 else n, tm_pref)
    tf = FFN_TILE
    nf = D_FF // tf
    assert nf >= 2
    per = seq // tm if groups > 1 else 1
    mod_map = (lambda i, j: (i // per, 0, 0)) if groups > 1 else (lambda i, j: (0, 0, 0))
    in_specs = [pl.BlockSpec((tm, d), lambda i, j: (i, 0)),
                pl.BlockSpec((None, N_MOD, d), mod_map),
                pl.BlockSpec((None, None, 1, d), lambda i, j: (layer, 2 * which, 0, 0)),
                pl.BlockSpec((None, None, d, tf), lambda i, j: (layer, which, 0, j)),
                pl.BlockSpec((None, None, d, tf), lambda i, j: (layer, which, 0, nf + j)),
                pl.BlockSpec((None, None, tf, d), lambda i, j: (layer, which, j, 0))]
    args = [x, mod, pr["norm_g"], pr["ffn_w13"], pr["ffn_w13"], pr["ffn_w2"]]
    if final:
        in_specs.append(pl.BlockSpec((1, d), lambda i, j: (0, 0)))
        args.append(final_g)
    return pl.pallas_call(
        functools.partial(_ffn_kernel, row0=6 * which, slab=_pick(tm, 512), final=final),
        out_shape=jax.ShapeDtypeStruct((n, d), F32),
        grid=(n // tm, nf),
        in_specs=in_specs,
        out_specs=pl.BlockSpec((tm, d), lambda i, j: (i, 0)),
        scratch_shapes=[pltpu.VMEM((tm, d), BF16)],
        compiler_params=_cparams(("parallel", "arbitrary"), FFN_VMEM_LIMIT_BYTES),
        name="ffn",
    )(*args)


def _inproj_kernel(x_ref, mod_ref, g_ref, w_ref, o_ref):
    h = _modulate(x_ref[...], g_ref[...], mod_ref[4:5, :], mod_ref[3:4, :]).astype(BF16)
    o_ref[...] = jnp.dot(h, w_ref[...], preferred_element_type=F32)


def _inproj(x, mod, pr, *, layer, seq, tm_pref=512):
    n, d = x.shape
    groups = mod.shape[0]
    tm = _pick(seq if groups > 1 else n, tm_pref)
    per = seq // tm if groups > 1 else 1
    mod_map = (lambda i: (i // per, 0, 0)) if groups > 1 else (lambda i: (0, 0, 0))
    return pl.pallas_call(
        _inproj_kernel,
        out_shape=jax.ShapeDtypeStruct((n, PROJ_COLS), F32),
        grid=(n // tm,),
        in_specs=[pl.BlockSpec((tm, d), lambda i: (i, 0)),
                  pl.BlockSpec((None, N_MOD, d), mod_map),
                  pl.BlockSpec((None, None, 1, d), lambda i: (layer, 1, 0, 0)),
                  _layer_spec(pr["w_in"], layer, 1)],
        out_specs=pl.BlockSpec((tm, PROJ_COLS), lambda i: (i, 0)),
        compiler_params=_cparams(("parallel",)),
        name="in_proj",
    )(x, mod, pr["norm_g"], pr["w_in"])


def _mla_kv(ckv_n, kr128, wukv_ref, ka_ref, va_ref):
    kv = jnp.dot(ckv_n.astype(BF16), wukv_ref[...], preferred_element_type=F32)
    krb = kr128.astype(BF16)
    hw = MLA_NOPE + MLA_V
    for h in range(MLA_HEADS):
        ka_ref[:, h * 2 * LANE:h * 2 * LANE + MLA_NOPE] = kv[:, h * hw:h * hw + MLA_NOPE].astype(BF16)
        ka_ref[:, h * 2 * LANE + MLA_NOPE:(h + 1) * 2 * LANE] = krb
        va_ref[:, h * MLA_V:(h + 1) * MLA_V] = kv[:, h * hw + MLA_NOPE:(h + 1) * hw].astype(BF16)


def _attn_prep_kernel(*refs, rope):
    if rope:
        (cq_ref, ckv_ref, krmg_ref, gq_ref, gk_ref, gv_ref, ca_ref, sa_ref, cb_ref, sb_ref,
         qng_ref, kvng_ref, wuq_ref, wukv_ref, gqg_ref, gkg_ref,
         qa_ref, ka_ref, va_ref, qb_ref, kb_ref, vb_ref) = refs
    else:
        (cq_ref, ckv_ref, krmg_ref, gq_ref, gk_ref, gv_ref,
         qng_ref, kvng_ref, wuq_ref, wukv_ref, gqg_ref, gkg_ref,
         qa_ref, ka_ref, va_ref, qb_ref, kb_ref, vb_ref, ckvn_ref, kbn_ref) = refs

    cqn = _rms(cq_ref[...], qng_ref[...]).astype(BF16)
    qa = jnp.dot(cqn, wuq_ref[...], preferred_element_type=F32)
    if rope:
        ca = jnp.concatenate([ca_ref[...]] * MLA_HEADS, axis=1)
        sa = jnp.concatenate([sa_ref[...]] * MLA_HEADS, axis=1)
        qa = qa * ca + _swap_halves(qa, MLA_ROPE // 4) * sa
    qa_ref[...] = (qa * ((MLA_NOPE + MLA_ROPE) ** -0.5 * LOG2E)).astype(BF16)

    ckv_n = _rms(ckv_ref[...], kvng_ref[...])
    if not rope:
        ckvn_ref[...] = ckv_n
    krmg = krmg_ref[...]
    lane = lax.broadcasted_iota(jnp.int32, krmg.shape, 1)
    if rope:
        krmg = krmg * ca_ref[:, LANE:] + _swap_halves(krmg, MLA_ROPE // 4) * sa_ref[:, LANE:]
    kr128 = jnp.where(lane < MLA_ROPE, krmg, 0.0)
    _mla_kv(ckv_n, kr128, wukv_ref, ka_ref, va_ref)

    hd = GQA_HEAD_DIM
    gq, gk = gq_ref[...], gk_ref[...]
    for h in range(GQA_HEADS):
        q = _rms(gq[:, h * hd:(h + 1) * hd], gqg_ref[...])
        if rope:
            q = q * cb_ref[...] + _swap_halves(q, hd // 4) * sb_ref[...]
        qb_ref[:, h * hd:(h + 1) * hd] = (q * (hd ** -0.5 * LOG2E)).astype(BF16)
    for h in range(GQA_KV_HEADS):
        k = _rms(gk[:, h * hd:(h + 1) * hd], gkg_ref[...])
        if rope:
            k = k * cb_ref[...] + _swap_halves(k, hd // 4) * sb_ref[...]
        else:
            kbn_ref[:, h * hd:(h + 1) * hd] = k
        kb_ref[:, h * hd:(h + 1) * hd] = k.astype(BF16)
    vb_ref[...] = gv_ref[...].astype(BF16)


def _attn_prep(proj, tables, pr, *, layer, seq, tm_pref=1024):
    n = proj.shape[0]
    rope = tables is not None
    tm = _pick(seq if rope else n, tm_pref)
    per = max(1, seq // tm)

    def pblock(name, width):
        idx = _COL[name] // width
        assert idx * width == _COL[name]
        return pl.BlockSpec((tm, width), lambda i: (i, idx))

    in_specs = [pblock("cq", MLA_Q_LORA), pblock("ckv", MLA_KV_LORA),
                pl.BlockSpec((tm, LANE), lambda i: (i, KRMG_BLOCK)),
                pblock("gq", 512), pblock("gk", 256), pblock("gv", 256)]
    args = [proj] * 6
    if rope:
        in_specs += [pl.BlockSpec((tm, t.shape[1]), lambda i: (i % per, 0)) for t in tables]
        args += list(tables)
    ws = [pr["mla_q_norm"], pr["mla_kv_norm"], pr["mla_w_uq"], pr["mla_w_ukv"], pr["gqa_q_norm"], pr["gqa_k_norm"]]
    in_specs += [_layer_spec(w, layer, 1) for w in ws]
    args += ws
    gkv = GQA_KV_HEADS * GQA_HEAD_DIM
    widths = [(4 * 2 * LANE, BF16), (4 * 2 * LANE, BF16), (4 * MLA_V, BF16),
              (GQA_HEADS * GQA_HEAD_DIM, BF16), (gkv, BF16), (gkv, BF16)]
    if not rope:
        widths += [(MLA_KV_LORA, F32), (gkv, F32)]
    return pl.pallas_call(
        functools.partial(_attn_prep_kernel, rope=rope),
        out_shape=tuple(jax.ShapeDtypeStruct((n, w), dt) for w, dt in widths),
        grid=(n // tm,),
        in_specs=in_specs,
        out_specs=tuple(pl.BlockSpec((tm, w), lambda i: (i, 0)) for w, _ in widths),
        compiler_params=_cparams(("parallel",)),
        name="attn_prep",
    )(*args)


def _cache_kv_kernel(ckv_ref, kr_ref, wukv_ref, ka_ref, va_ref):
    _mla_kv(ckv_ref[...], kr_ref[...], wukv_ref, ka_ref, va_ref)


def _cache_kv(ckv, kr128, wukv, *, layer):
    b, _, p, _ = ckv.shape
    return pl.pallas_call(
        _cache_kv_kernel,
        out_shape=(jax.ShapeDtypeStruct((b, p, 4 * 2 * LANE), BF16), jax.ShapeDtypeStruct((b, p, 4 * MLA_V), BF16)),
        grid=(b,),
        in_specs=[pl.BlockSpec((None, None, p, MLA_KV_LORA), lambda i: (i, layer, 0, 0)),
                  pl.BlockSpec((None, None, p, LANE), lambda i: (i, layer, 0, 0)),
                  _layer_spec(wukv, layer, 1)],
        out_specs=(pl.BlockSpec((None, p, 4 * 2 * LANE), lambda i: (i, 0, 0)),
                   pl.BlockSpec((None, p, 4 * MLA_V), lambda i: (i, 0, 0))),
        compiler_params=_cparams(("parallel",)),
        name="mla_cache_kv",
    )(ckv, kr128, wukv)


def _attn_kernel(q_ref, *refs, nseg, hp, g, dk, dv):
    o_ref = refs[-1]
    sub = math.gcd(q_ref.shape[0], ATTN_SUB_ROWS)
    for j in range(hp):
        kv = j // g
        ks = [refs[2 * i][:, kv * dk:(kv + 1) * dk].astype(BF16) for i in range(nseg)]
        vs = [refs[2 * i + 1][:, kv * dv:(kv + 1) * dv].astype(BF16) for i in range(nseg)]
        for r in range(q_ref.shape[0] // sub):
            rows = slice(r * sub, (r + 1) * sub)
            q = q_ref[rows, j * dk:(j + 1) * dk]
            ss = [lax.dot_general(q, k, (((1,), (1,)), ((), ())), preferred_element_type=F32) for k in ks]
            m = functools.reduce(jnp.maximum, [jnp.max(s, axis=-1, keepdims=True) for s in ss])
            ps = [jnp.exp2(s - m) for s in ss]
            l = functools.reduce(jnp.add, [jnp.sum(p, axis=-1, keepdims=True) for p in ps])
            o = functools.reduce(jnp.add, [jnp.dot(p.astype(BF16), v, preferred_element_type=F32)
                                           for p, v in zip(ps, vs)])
            o_ref[rows, j * dv:(j + 1) * dv] = o / l


def _attention(q, segs, *, heads, kv_heads, dk, dv, tq_pref=512, hp=4):
    b, sq, _ = q.shape
    tq = _pick(sq, tq_pref)
    g = heads // kv_heads
    assert heads % hp == 0 and (hp % g == 0 or g % hp == 0)
    kvp = max(1, hp // g)
    kcol = (lambda hb: hb) if hp >= g else (lambda hb: hb * hp // g)
    in_specs = [pl.BlockSpec((None, tq, hp * dk), lambda bi, hb, i: (bi, i, hb))]
    args = [q]
    for k, v, layer in segs:
        sk = k.shape[-2]
        if layer is None:
            in_specs += [pl.BlockSpec((None, sk, kvp * dk), lambda bi, hb, i: (bi, 0, kcol(hb))),
                         pl.BlockSpec((None, sk, kvp * dv), lambda bi, hb, i: (bi, 0, kcol(hb)))]
        else:
            in_specs += [pl.BlockSpec((None, None, sk, kvp * dk),
                                      lambda bi, hb, i, layer=layer: (bi, layer, 0, kcol(hb))),
                         pl.BlockSpec((None, None, sk, kvp * dv),
                                      lambda bi, hb, i, layer=layer: (bi, layer, 0, kcol(hb)))]
        args += [k, v]
    return pl.pallas_call(
        functools.partial(_attn_kernel, nseg=len(segs), hp=hp, g=min(g, hp), dk=dk, dv=dv),
        out_shape=jax.ShapeDtypeStruct((b, sq, heads * dv), F32),
        grid=(b, heads // hp, sq // tq),
        in_specs=in_specs,
        out_specs=pl.BlockSpec((None, tq, hp * dv), lambda bi, hb, i: (bi, i, hb)),
        compiler_params=_cparams(("parallel", "parallel", "parallel")),
        name="attention",
    )(*args)


def _log_sigmoid(x):
    return jnp.minimum(x, 0.0) - jnp.log1p(jnp.exp(-jnp.abs(x)))


def _mlstm_chunk_operands(q_ref, k_ref, v_ref, gate_ref, bias_ref):
    nh, hd = MLSTM_HEADS, MLSTM_HEAD_DIM
    g = gate_ref[...] + bias_ref[...]
    lf = _log_sigmoid(g)
    lf_hi = lf.astype(BF16)
    rem = lf - lf_hi.astype(F32)
    lf_mid = rem.astype(BF16)
    lf_lo = (rem - lf_mid.astype(F32)).astype(BF16)
    heads = []
    for h in range(nh):
        q = q_ref[:, h * hd:(h + 1) * hd]
        kh = k_ref[:, h * hd:(h + 1) * hd] * (hd ** -0.5)
        kt = kh.T
        qh = q.astype(BF16)
        heads.append(dict(q=q, qh=qh, kh=kh, kt=kt, vh=v_ref[:, h * hd:(h + 1) * hd].astype(BF16),
                          s=jnp.dot(qh, kt.astype(BF16), preferred_element_type=F32)))
    return dict(g=g, gt=g.T, lf=(lf_hi, lf_mid, lf_lo), heads=heads)


def _prefix_max_lanes(x, rev):
    n = x.shape[1]
    lane = lax.broadcasted_iota(jnp.int32, x.shape, 1)
    k = 1
    while k < n:
        if rev:
            shifted = jnp.where(lane < n - k, pltpu.roll(x, n - k, axis=1), -jnp.inf)
        else:
            shifted = jnp.where(lane >= k, pltpu.roll(x, k, axis=1), -jnp.inf)
        x = jnp.maximum(x, shifted)
        k *= 2
    return x


def _mlstm_chain(ops, h_ref, c_s, n_s, m_s, *, direction):
    nh, hd = MLSTM_HEADS, MLSTM_HEAD_DIM
    rev = direction == 1
    g, gt = ops["g"], ops["gt"]
    cl = g.shape[0]
    ri = lax.broadcasted_iota(jnp.int32, (cl, cl), 0)
    ci = lax.broadcasted_iota(jnp.int32, (cl, cl), 1)
    mask = (ri <= ci) if rev else (ri >= ci)
    tri = jnp.where(mask, 1.0, 0.0).astype(BF16)
    cum = functools.reduce(jnp.add, [jnp.dot(tri, part, preferred_element_type=F32) for part in ops["lf"]])
    cumt = cum.T
    last = 0 if rev else cl - 1
    li0 = MG_LANE + direction * 2 * nh
    lf0 = li0 + nh
    assert li0 % SUBLANE == 0 and 2 * nh == SUBLANE

    b8 = pltpu.roll(gt[li0:li0 + SUBLANE, :], nh, axis=0) - cumt[li0:li0 + SUBLANE, :]
    pm8 = _prefix_max_lanes(b8, rev)
    pm_rows = jnp.concatenate([jnp.zeros((li0, cl), F32), pm8, jnp.zeros((LANE - li0 - SUBLANE, cl), F32)], axis=0)
    pm_c = pm_rows.T
    lane = lax.broadcasted_iota(jnp.int32, (1, LANE), 1)
    m_lanes = functools.reduce(jnp.add, [jnp.where(lane == lf0 + h, m_s[direction, h:h + 1, :], 0.0)
                                         for h in range(nh)])
    mx_all = jnp.maximum(m_lanes, pm_c)
    interw_all = jnp.exp(m_lanes - mx_all)
    em_all = jnp.exp(-(cum + mx_all))

    for h in range(nh):
        hv = ops["heads"][h]
        cum_c = cum[:, lf0 + h:lf0 + h + 1]
        cum_r = cumt[lf0 + h:lf0 + h + 1, :]
        li_r = gt[li0 + h:li0 + h + 1, :]
        li_c = g[:, li0 + h:li0 + h + 1]
        m_prev = m_s[direction, h:h + 1, 0:1]
        w = jnp.where(mask, jnp.exp(b8[nh + h:nh + h + 1, :] - mx_all[:, lf0 + h:lf0 + h + 1]), 0.0)
        sc = hv["s"] * w
        inter_w = interw_all[:, lf0 + h:lf0 + h + 1]
        cmat = c_s[direction, h]
        nrow = n_s[direction, h:h + 1, :]
        num = (jnp.dot(sc.astype(BF16), hv["vh"], preferred_element_type=F32)
               + inter_w * jnp.dot(hv["qh"], cmat.astype(BF16), preferred_element_type=F32))
        qn = jnp.sum(hv["q"] * nrow, axis=1, keepdims=True)
        den = jnp.sum(sc, axis=1, keepdims=True) + inter_w * qn
        h_ref[:, h * hd:(h + 1) * hd] = num / jnp.maximum(jnp.abs(den), em_all[:, lf0 + h:lf0 + h + 1])

        tot = cum_c[last:last + 1, :]
        g_r = tot - cum_r + li_r
        g_c = tot - cum_c + li_c
        m_new = jnp.maximum(tot + m_prev, jnp.max(g_r, axis=1, keepdims=True))
        decay = jnp.exp(tot + m_prev - m_new)
        ws_r = jnp.exp(g_r - m_new)
        ws_c = jnp.exp(g_c - m_new)
        c_s[direction, h] = decay * cmat + jnp.dot((hv["kt"] * ws_r).astype(BF16), hv["vh"],
                                                   preferred_element_type=F32)
        n_s[direction, h:h + 1, :] = decay * nrow + jnp.sum(hv["kh"] * ws_c, axis=0, keepdims=True)
        m_s[direction, h:h + 1, :] = jnp.broadcast_to(m_new, (1, hd))


def _mlstm_kernel(*refs, shared, has_init, emit_state):
    nin = 4 if shared else 8
    fwd_in = refs[:4]
    bwd_in = fwd_in if shared else refs[4:8]
    bias_ref = refs[nin]
    pos = nin + 1
    if has_init:
        c0_ref, n0_ref, m0_ref = refs[pos:pos + 3]
        pos += 3
    hf_ref, hb_ref = refs[pos:pos + 2]
    pos += 2
    if emit_state:
        co_ref, no_ref, mo_ref = refs[pos:pos + 3]
        pos += 3
    c_s, n_s, m_s = refs[pos:pos + 3]
    c = pl.program_id(1)

    @pl.when(c == 0)
    def _():
        if has_init:
            c_s[...] = c0_ref[...]
            n_s[...] = n0_ref[...]
            m_s[...] = m0_ref[...]
        else:
            c_s[...] = jnp.zeros_like(c_s)
            n_s[...] = jnp.zeros_like(n_s)
            m_s[...] = jnp.zeros_like(m_s)

    ops_f = _mlstm_chunk_operands(*fwd_in, bias_ref)
    ops_b = ops_f if shared else _mlstm_chunk_operands(*bwd_in, bias_ref)
    _mlstm_chain(ops_f, hf_ref, c_s, n_s, m_s, direction=0)
    _mlstm_chain(ops_b, hb_ref, c_s, n_s, m_s, direction=1)

    if emit_state:
        @pl.when(c == pl.num_programs(1) - 1)
        def _():
            co_ref[...] = c_s[...]
            no_ref[...] = n_s[...]
            mo_ref[...] = m_s[...]


def _mlstm(proj, bias128, init, *, layer, batch, seq, emit_state, chunk_pref=256):
    n = proj.shape[0]
    cl = _pick(seq, chunk_pref)
    nc = seq // cl
    shared = nc == 1
    nh, hd = MLSTM_HEADS, MLSTM_HEAD_DIM
    rows = (lambda b, c: b * nc + c), (lambda b, c: b * nc + (nc - 1 - c))

    def chunk_specs(row):
        qkv = lambda name: pl.BlockSpec((cl, nh * hd), lambda b, c: (row(b, c), _COL[name] // (nh * hd)))
        return [qkv("mq"), qkv("mk"), qkv("mv"), pl.BlockSpec((cl, LANE), lambda b, c: (row(b, c), KRMG_BLOCK))]

    in_specs = chunk_specs(rows[0]) + ([] if shared else chunk_specs(rows[1]))
    in_specs.append(pl.BlockSpec((None, 1, LANE), lambda b, c: (layer, 0, 0)))
    args = [proj] * (len(in_specs) - 1) + [bias128]
    has_init = init is not None
    if has_init:
        c0, n0, m0 = init
        in_specs += [pl.BlockSpec((None, None, 2, nh, hd, hd), lambda b, c: (b, layer, 0, 0, 0, 0)),
                     pl.BlockSpec((None, None, 2, nh, hd), lambda b, c: (b, layer, 0, 0, 0)),
                     pl.BlockSpec((None, None, 2, nh, hd), lambda b, c: (b, layer, 0, 0, 0))]
        args += [c0, n0, m0]
    out_shape = [jax.ShapeDtypeStruct((n, nh * hd), F32)] * 2
    out_specs = [pl.BlockSpec((cl, nh * hd), lambda b, c, row=row: (row(b, c), 0)) for row in rows]
    if emit_state:
        out_shape += [jax.ShapeDtypeStruct((batch, 2, nh, hd, hd), F32), jax.ShapeDtypeStruct((batch, 2, nh, hd), F32),
                      jax.ShapeDtypeStruct((batch, 2, nh, hd), F32)]
        out_specs += [pl.BlockSpec((None, 2, nh, hd, hd), lambda b, c: (b, 0, 0, 0, 0)),
                      pl.BlockSpec((None, 2, nh, hd), lambda b, c: (b, 0, 0, 0)),
                      pl.BlockSpec((None, 2, nh, hd), lambda b, c: (b, 0, 0, 0))]
    return pl.pallas_call(
        functools.partial(_mlstm_kernel, shared=shared, has_init=has_init, emit_state=emit_state),
        out_shape=tuple(out_shape),
        grid=(batch, nc),
        in_specs=in_specs,
        out_specs=tuple(out_specs),
        scratch_shapes=[pltpu.VMEM((2, nh, hd, hd), F32), pltpu.VMEM((2, nh, hd), F32), pltpu.VMEM((2, nh, hd), F32)],
        compiler_params=_cparams(("parallel", "arbitrary")),
        name="mlstm",
    )(*args)


_S5_STATE_TILES = 2 * S5_NS // MXU_TILE
_S5_GROUPS_PER_TILE = MXU_TILE // S5_STATE


def _s5_channel_tile(state_tile):
    first_group = (state_tile % (S5_NS // MXU_TILE)) * _S5_GROUPS_PER_TILE
    return first_group * S5_GROUP // MXU_TILE


def _to_time_major_kernel(u_ref, p_ref, o_ref):
    rows = p_ref.shape[0]
    sg, ch = u_ref.shape[0], u_ref.shape[2]
    tsteps = rows // sg
    for k in range(u_ref.shape[1] // tsteps):
        u = u_ref[:, k * tsteps:(k + 1) * tsteps, :].reshape(rows, ch).astype(BF16)
        o_ref[k * rows:(k + 1) * rows, :] = jnp.dot(p_ref[...], u, preferred_element_type=F32).astype(BF16)


def _from_time_major_kernel(yf_ref, yb_ref, pt_ref, o_ref):
    rows = pt_ref.shape[0]
    sg, ch = o_ref.shape[0], o_ref.shape[2]
    tsteps = rows // sg
    pt = pt_ref[...]
    for k in range(o_ref.shape[1] // tsteps):
        y = yf_ref[k * rows:(k + 1) * rows, :] + yb_ref[k * rows:(k + 1) * rows, :]
        hi = y.astype(BF16)
        lo = (y - hi.astype(F32)).astype(BF16)
        out = jnp.dot(pt, hi, preferred_element_type=F32) + jnp.dot(pt, lo, preferred_element_type=F32)
        o_ref[:, k * tsteps:(k + 1) * tsteps, :] = out.reshape(sg, tsteps, ch)


def _s5_kernel(*refs, has_init):
    if has_init:
        u_ref, bd_ref, a_ref, cd_ref, x0_ref, y_ref, xf_ref, bu_s, x_s = refs
    else:
        u_ref, bd_ref, a_ref, cd_ref, y_ref, xf_ref, bu_s, x_s = refs
    d = pl.program_id(0)
    c = pl.program_id(2)
    ns = S5_NS
    sg = x_s.shape[0]
    tsteps = u_ref.shape[0] // sg
    mt = MXU_TILE

    @pl.when(c == 0)
    def _():
        if has_init:
            x_s[...] = x0_ref[...]
        else:
            x_s[...] = jnp.zeros_like(x_s)

    u = u_ref[...]
    for st in range(_S5_STATE_TILES):
        ct = _s5_channel_tile(st)
        bu_s[:, st * mt:(st + 1) * mt] = jnp.dot(u[:, ct * mt:(ct + 1) * mt],
                                                 bd_ref[ct * mt:(ct + 1) * mt, st * mt:(st + 1) * mt],
                                                 preferred_element_type=F32)
    a_re = jnp.broadcast_to(a_ref[:, :ns], (sg, ns))
    a_im = jnp.broadcast_to(a_ref[:, ns:], (sg, ns))

    def body(t, carry):
        xr, xi = carry
        tt = t + d * (tsteps - 1 - 2 * t)
        r0 = pl.multiple_of(tt * sg, sg)
        br = bu_s[pl.ds(r0, sg), :ns]
        bi = bu_s[pl.ds(r0, sg), ns:]
        nr = a_re * xr - a_im * xi + br
        ni = a_re * xi + a_im * xr + bi
        bu_s[pl.ds(r0, sg), :ns] = nr
        bu_s[pl.ds(r0, sg), ns:] = ni
        return nr, ni

    xr, xi = lax.fori_loop(0, tsteps, body, (x_s[:, :ns], x_s[:, ns:]), unroll=SCAN_UNROLL)
    x_s[:, :ns] = xr
    x_s[:, ns:] = xi

    for ct in range(S5_CH // mt):
        acc = None
        for st in range(_S5_STATE_TILES):
            if _s5_channel_tile(st) != ct:
                continue
            t = jnp.dot(bu_s[:, st * mt:(st + 1) * mt].astype(BF16),
                        cd_ref[st * mt:(st + 1) * mt, ct * mt:(ct + 1) * mt], preferred_element_type=F32)
            acc = t if acc is None else acc + t
        y_ref[:, ct * mt:(ct + 1) * mt] = acc

    @pl.when(c == pl.num_programs(2) - 1)
    def _():
        xf_ref[...] = x_s[...]


def _s5(proj3, pr, x0, *, layer, rows_pref=512, scan_rows_pref=1024):
    batch, seq, _ = proj3.shape
    sg = SUBLANE if batch % SUBLANE == 0 else batch
    ng = batch // sg
    tsteps = _pick(seq, max(SUBLANE, rows_pref // sg))
    rows = sg * tsteps
    nc = seq // tsteps
    r = jnp.arange(rows)
    perm = (jnp.arange(rows)[None, :] == ((r % sg) * tsteps + r // sg)[:, None]).astype(BF16)
    const2 = lambda a: pl.BlockSpec(a.shape, lambda g, c: (0,) * a.ndim, pipeline_mode=pl.Buffered(1))
    nsub = math.gcd(nc, RELAYOUT_SUB)
    nc_r = nc // nsub

    u_tm = pl.pallas_call(
        _to_time_major_kernel,
        out_shape=jax.ShapeDtypeStruct((ng, seq * sg, S5_CH), BF16),
        grid=(ng, nc_r),
        in_specs=[pl.BlockSpec((sg, nsub * tsteps, S5_CH), lambda g, c: (g, c, _COL["su"] // S5_CH)), const2(perm)],
        out_specs=pl.BlockSpec((None, nsub * rows, S5_CH), lambda g, c: (g, c, 0)),
        compiler_params=_cparams(("parallel", "parallel")),
        name="s5_to_time_major",
    )(proj3, perm)

    ts_scan = _pick(seq, max(SUBLANE, scan_rows_pref // sg))
    rows_scan = sg * ts_scan
    nc_scan = seq // ts_scan
    cpos = lambda d, c: c + d * (nc_scan - 1 - 2 * c)
    has_init = x0 is not None
    in_specs = [pl.BlockSpec((None, rows_scan, S5_CH), lambda d, g, c: (g, cpos(d, c), 0)),
                pl.BlockSpec((None, None, S5_CH, 2 * S5_NS), lambda d, g, c: (layer, d, 0, 0)),
                pl.BlockSpec((None, None, 1, 2 * S5_NS), lambda d, g, c: (layer, d, 0, 0)),
                _layer_spec(pr["s5_cd"], layer, 3)]
    args = [u_tm, pr["s5_bd"], pr["s5_abar"], pr["s5_cd"]]
    if has_init:
        in_specs.append(pl.BlockSpec((None, sg, 2 * S5_NS), lambda d, g, c: (d, g, 0)))
        args.append(x0)
    y_tm, xfin = pl.pallas_call(
        functools.partial(_s5_kernel, has_init=has_init),
        out_shape=(jax.ShapeDtypeStruct((2, ng, seq * sg, S5_CH), F32),
                   jax.ShapeDtypeStruct((2, batch, 2 * S5_NS), F32)),
        grid=(2, ng, nc_scan),
        in_specs=in_specs,
        out_specs=(pl.BlockSpec((None, None, rows_scan, S5_CH), lambda d, g, c: (d, g, cpos(d, c), 0)),
                   pl.BlockSpec((None, sg, 2 * S5_NS), lambda d, g, c: (d, g, 0))),
        scratch_shapes=[pltpu.VMEM((rows_scan, 2 * S5_NS), F32), pltpu.VMEM((sg, 2 * S5_NS), F32)],
        compiler_params=_cparams(("parallel", "parallel", "arbitrary")),
        name="s5_scan",
    )(*args)

    y = pl.pallas_call(
        _from_time_major_kernel,
        out_shape=jax.ShapeDtypeStruct((batch, seq, S5_CH), F32),
        grid=(ng, nc_r),
        in_specs=[pl.BlockSpec((None, None, nsub * rows, S5_CH), lambda g, c: (0, g, c, 0)),
                  pl.BlockSpec((None, None, nsub * rows, S5_CH), lambda g, c: (1, g, c, 0)),
                  const2(perm)],
        out_specs=pl.BlockSpec((sg, nsub * tsteps, S5_CH), lambda g, c: (g, c, 0)),
        compiler_params=_cparams(("parallel", "parallel")),
        name="s5_from_time_major",
    )(y_tm, y_tm, perm.T)
    return y, xfin


def _gelu_tanh(x):
    return 0.5 * x * (1.0 + jnp.tanh(math.sqrt(2.0 / math.pi) * (x + 0.044715 * (x * x * x))))


def _outproj_kernel(x_ref, mod_ref, oa_ref, ob_ref, hf_ref, hb_ref, mo_ref, y_ref, su_ref,
                    d_ref, wglu_ref, on_ref, wout_ref, o_ref):
    slab = math.gcd(x_ref.shape[0], OUTPROJ_SLAB)
    for r in range(x_ref.shape[0] // slab):
        rows = slice(r * slab, (r + 1) * slab)
        oc = jax.nn.sigmoid(mo_ref[rows, :]) * (hf_ref[rows, :] + hb_ref[rows, :])
        y = _gelu_tanh(y_ref[rows, :] + d_ref[...] * su_ref[rows, :])
        od = y * jax.nn.sigmoid(jnp.dot(y.astype(BF16), wglu_ref[...], preferred_element_type=F32))
        acc = None
        for gi, part in enumerate((oa_ref[rows, :], ob_ref[rows, :], oc, od)):
            nrm = _rms(part, on_ref[gi:gi + 1, :]).astype(BF16)
            t = jnp.dot(nrm, wout_ref[gi * GROUP_WIDTH:(gi + 1) * GROUP_WIDTH, :], preferred_element_type=F32)
            acc = t if acc is None else acc + t
        o_ref[rows, :] = x_ref[rows, :] + mod_ref[5:6, :] * acc


def _outproj(x, mod, oa, ob, hf, hb, proj, y, pr, *, layer, seq, tm_pref=512):
    n, d = x.shape
    groups = mod.shape[0]
    tm = _pick(seq if groups > 1 else n, tm_pref)
    per = seq // tm if groups > 1 else 1
    mod_map = (lambda i: (i // per, 0, 0)) if groups > 1 else (lambda i: (0, 0, 0))
    gw = GROUP_WIDTH
    rowblk = pl.BlockSpec((tm, gw), lambda i: (i, 0))
    ws = [pr["s5_d"], pr["s5_w_glu"], pr["out_norm"], pr["w_out"]]
    return pl.pallas_call(
        _outproj_kernel,
        out_shape=jax.ShapeDtypeStruct((n, d), F32),
        grid=(n // tm,),
        in_specs=[pl.BlockSpec((tm, d), lambda i: (i, 0)),
                  pl.BlockSpec((None, N_MOD, d), mod_map),
                  rowblk, rowblk, rowblk, rowblk,
                  pl.BlockSpec((tm, gw), lambda i: (i, _COL["mo"] // gw)),
                  rowblk,
                  pl.BlockSpec((tm, gw), lambda i: (i, _COL["su"] // gw))] + [_layer_spec(w, layer, 1) for w in ws],
        out_specs=pl.BlockSpec((tm, d), lambda i: (i, 0)),
        compiler_params=_cparams(("parallel",)),
        name="merge_out_proj",
    )(x, mod, oa, ob, hf, hb, proj, y, proj, *ws)


def _trunk_layer(x, mod, pr, *, layer, batch, seq, tables, cache, final_g=None):
    latent = cache is not None
    x = _ffn(x, mod, pr, layer=layer, which=0, seq=seq)
    proj = _inproj(x, mod, pr, layer=layer, seq=seq)

    prep = _attn_prep(proj, tables, pr, layer=layer, seq=seq)
    qa, ka, va, qb, kb, vb = prep[:6]
    r3 = lambda a: a.reshape(batch, seq, a.shape[-1])
    segs_a = [(r3(ka), r3(va), None)]
    segs_b = [(r3(kb), r3(vb), None)]
    if latent:
        segs_a.insert(0, cache["mla_kv"] + (None,))
        segs_b.insert(0, cache["gqa_kv"] + (layer,))
    oa = _attention(r3(qa), segs_a, heads=MLA_HEADS, kv_heads=MLA_HEADS, dk=2 * LANE, dv=MLA_V)
    ob = _attention(r3(qb), segs_b, heads=GQA_HEADS, kv_heads=GQA_KV_HEADS, dk=GQA_HEAD_DIM, dv=GQA_HEAD_DIM)
    oa = oa.reshape(batch * seq, -1)
    ob = ob.reshape(batch * seq, -1)

    m_init = cache["mlstm"] if latent else None
    mres = _mlstm(proj, pr["mlstm_bias"], m_init, layer=layer, batch=batch, seq=seq, emit_state=not latent)

    y, xfin = _s5(proj.reshape(batch, seq, PROJ_COLS), pr, cache["s5"] if latent else None, layer=layer)

    x = _outproj(x, mod, oa, ob, mres[0], mres[1], proj, y.reshape(batch * seq, S5_CH), pr,
                 layer=layer, seq=seq)
    x = _ffn(x, mod, pr, layer=layer, which=1, seq=seq, final_g=final_g)

    new_ctx = None
    if not latent:
        ckvn, kbn = prep[6:]
        kr = proj[:, _COL["kr"]:_COL["kr"] + MLA_ROPE]
        gv = proj[:, _COL["gv"]:_COL["gv"] + GQA_KV_HEADS * GQA_HEAD_DIM]
        xs = xfin.reshape(2, batch, 2, S5_GROUPS, S5_STATE).transpose(1, 0, 2, 3, 4)
        new_ctx = (ckvn.reshape(batch, seq, MLA_KV_LORA),
                   kr.reshape(batch, seq, MLA_ROPE),
                   kbn.reshape(batch, seq, GQA_KV_HEADS, GQA_HEAD_DIM),
                   gv.reshape(batch, seq, GQA_KV_HEADS, GQA_HEAD_DIM),
                   mres[2], mres[3], mres[4][..., 0],
                   xs[:, :, 0], xs[:, :, 1])
    return x, new_ctx


def _permute_w_in(w_in):
    parts = [w_in[..., _ORIG[n][0]:_ORIG[n][0] + _ORIG[n][1]] for n in _ORDER]
    pad = PROJ_COLS - sum(p.shape[-1] for p in parts)
    parts.append(jnp.zeros(w_in.shape[:-1] + (pad,), w_in.dtype))
    return jnp.concatenate(parts, axis=-1).astype(BF16)


def _permute_w_uq(w_uq):
    depth, k, _ = w_uq.shape
    w = w_uq.reshape(depth, k, MLA_HEADS, MLA_NOPE + MLA_ROPE)
    w = jnp.pad(w, ((0, 0), (0, 0), (0, 0), (0, 2 * LANE - MLA_NOPE - MLA_ROPE)))
    return w.reshape(depth, k, MLA_HEADS * 2 * LANE).astype(BF16)


def kernel(x_prompt, x_sample, cache_mla_ckv, cache_mla_krope, cache_gqa_k, cache_gqa_v, state_mlstm_c, state_mlstm_n, state_mlstm_m, state_s5_re, state_s5_im, c, c_ctx, ada_w, ada_b, norm_g, ffn_w13, ffn_w2, w_in, mla_q_norm, mla_kv_norm, mla_w_uq, mla_w_ukv, gqa_q_norm, gqa_k_norm, mlstm_gate_b, s5_a_re, s5_a_im, s5_log_dt, s5_b_re, s5_b_im, s5_c_re, s5_c_im, s5_d, s5_w_glu, out_norm, w_out, final_norm):
    bc, sc, d = x_prompt.shape
    bl, sl, _ = x_sample.shape
    depth = ada_w.shape[0]
    past = cache_mla_ckv.shape[2]

    rows = ((1 + bl + SUBLANE - 1) // SUBLANE) * SUBLANE
    cvecs = jnp.concatenate([c_ctx[None, :], c, jnp.zeros((rows - 1 - bl, d), F32)], axis=0)
    mod_all = _modulation(cvecs, ada_w, ada_b).reshape(depth, rows, N_MOD, d)

    tables = _rope_tables(sl)

    abar, bd, cd = _s5_params(s5_a_re, s5_a_im, s5_log_dt, s5_b_re, s5_b_im, s5_c_re, s5_c_im)
    nmg = MLSTM_HEADS * 4
    row = lambda a: a.reshape(depth, 1, a.shape[-1])
    pr = {"norm_g": norm_g.reshape(depth, 3, 1, d),
          "ffn_w13": ffn_w13.astype(BF16), "ffn_w2": ffn_w2.astype(BF16),
          "w_in": _permute_w_in(w_in),
          "mla_q_norm": row(mla_q_norm), "mla_kv_norm": row(mla_kv_norm),
          "mla_w_uq": _permute_w_uq(mla_w_uq), "mla_w_ukv": mla_w_ukv.astype(BF16),
          "gqa_q_norm": row(gqa_q_norm), "gqa_k_norm": row(gqa_k_norm),
          "mlstm_bias": jnp.pad(mlstm_gate_b.reshape(depth, 1, nmg), ((0, 0), (0, 0), (MG_LANE, LANE - MG_LANE - nmg))),
          "s5_abar": abar, "s5_bd": bd, "s5_cd": cd,
          "s5_d": row(s5_d), "s5_w_glu": s5_w_glu.astype(BF16),
          "out_norm": out_norm.reshape(depth, 4, GROUP_WIDTH), "w_out": w_out.astype(BF16)}

    gkv = GQA_KV_HEADS * GQA_HEAD_DIM
    cache_k = cache_gqa_k.reshape(bl, depth, past, gkv)
    cache_v = cache_gqa_v.reshape(bl, depth, past, gkv)
    m0 = jnp.broadcast_to(state_mlstm_m[..., None], state_mlstm_n.shape)
    kr_pad = jnp.pad(cache_mla_krope, ((0, 0), (0, 0), (0, 0), (0, LANE - MLA_ROPE)))

    x_ctx = x_prompt.reshape(bc * sc, d)
    x_lat = x_sample.reshape(bl * sl, d)
    per_layer = []
    for l in range(depth):
        final_g = final_norm[None, :] if l == depth - 1 else None
        x_ctx, ctx_l = _trunk_layer(x_ctx, mod_all[l, 0:1], pr, layer=l, batch=bc, seq=sc, tables=None, cache=None,
                                    final_g=final_g)
        per_layer.append(ctx_l)

        x0 = jnp.concatenate([state_s5_re[:, l].reshape(bl, 2, S5_NS), state_s5_im[:, l].reshape(bl, 2, S5_NS)],
                             axis=-1).transpose(1, 0, 2)
        cache = {"mla_kv": tuple(_cache_kv(cache_mla_ckv, kr_pad, pr["mla_w_ukv"], layer=l)),
                 "gqa_kv": (cache_k, cache_v),
                 "mlstm": (state_mlstm_c, state_mlstm_n, m0),
                 "s5": x0}
        x_lat, _ = _trunk_layer(x_lat, mod_all[l, 1:1 + bl], pr, layer=l, batch=bl, seq=sl, tables=tables,
                                cache=cache, final_g=final_g)

    new_ctx = [jnp.stack([t[i] for t in per_layer], axis=1) for i in range(9)]
    return (x_ctx.reshape(bc, sc, d), x_lat.reshape(bl, sl, d), *new_ctx)
```

```python
import functools
import math

import jax
import jax.numpy as jnp
from jax import lax
from jax.experimental import pallas as pl
from jax.experimental.pallas import tpu as pltpu

F32 = jnp.float32
BF16 = jnp.bfloat16

EPS = 1e-6
ROPE_THETA = 10000.0
GRID_W = 64
N_MOD = 9
LOG2E = math.log2(math.e)

D_FF = 5632
MLA_HEADS, MLA_NOPE, MLA_ROPE, MLA_V = 4, 128, 64, 128
MLA_Q_LORA, MLA_KV_LORA = 384, 256
GQA_HEADS, GQA_KV_HEADS, GQA_HEAD_DIM = 4, 2, 128
MLSTM_HEADS, MLSTM_HEAD_DIM = 4, 128
S5_GROUPS, S5_GROUP, S5_STATE = 32, 16, 64
S5_CH = S5_GROUPS * S5_GROUP
S5_NS = S5_GROUPS * S5_STATE
GROUP_WIDTH = 512

LANE = 128
SUBLANE = 8
MXU_TILE = 256
NORM_ROWS = 128
FFN_TILE = 512
RELAYOUT_SUB = 4
OUTPROJ_SLAB = 256
SCAN_UNROLL = 4
VMEM_BYTES = 64 * 1024 * 1024
VMEM_LIMIT_BYTES = VMEM_BYTES - 8 * 1024 * 1024
FFN_VMEM_LIMIT_BYTES = VMEM_BYTES - 4 * 1024 * 1024

_ORIG = dict(cq=(0, 384), ckv=(384, 256), kr=(640, 64), gq=(704, 512), gk=(1216, 256), gv=(1472, 256),
             mq=(1728, 512), mk=(2240, 512), mv=(2752, 512), mo=(3264, 512), mg=(3776, 16), su=(3792, 512))
_ORDER = ("mq", "mk", "mv", "mo", "gq", "su", "gk", "gv", "ckv", "cq", "kr", "mg")
_COL = {}
_off = 0
for _n in _ORDER:
    _COL[_n] = _off
    _off += _ORIG[_n][1]
PROJ_COLS = ((_off + LANE - 1) // LANE) * LANE
KRMG_BLOCK = _COL["kr"] // LANE
MG_LANE = _COL["mg"] - _COL["kr"]


def _cparams(sem, vmem_limit=VMEM_LIMIT_BYTES):
    return pltpu.CompilerParams(dimension_semantics=sem, vmem_limit_bytes=vmem_limit)


def _pick(n, pref):
    for t in range(min(n, pref), 0, -1):
        if n % t == 0 and (t % SUBLANE == 0 or t == n):
            return t
    return n


def _layer_spec(arr, layer, nargs):
    zeros = (0,) * (arr.ndim - 1)
    imap = {1: lambda i: (layer,) + zeros, 2: lambda i, j: (layer,) + zeros,
            3: lambda i, j, k: (layer,) + zeros}[nargs]
    return pl.BlockSpec((None,) + arr.shape[1:], imap, pipeline_mode=pl.Buffered(1))


def _rms(x, g):
    return x * lax.rsqrt(jnp.mean(x * x, axis=-1, keepdims=True) + EPS) * g


def _modulate(x, g, scale, shift):
    return x * lax.rsqrt(jnp.mean(x * x, axis=-1, keepdims=True) + EPS) * (g * (1.0 + scale)) + shift


def _swap_halves(x, half):
    w = x.shape[-1]
    lane = lax.broadcasted_iota(jnp.int32, x.shape, x.ndim - 1)
    first = (lane & (2 * half - 1)) < half
    return jnp.where(first, pltpu.roll(x, w - half, axis=x.ndim - 1), pltpu.roll(x, half, axis=x.ndim - 1))


def _silu(x):
    return x * jax.nn.sigmoid(x)


def _mod_kernel(c_ref, w_ref, b_ref, o_ref):
    a = _silu(c_ref[...]).astype(BF16)
    o_ref[...] = jnp.dot(a, w_ref[...].astype(BF16), preferred_element_type=F32) + b_ref[...]


def _modulation(cvecs, ada_w, ada_b):
    depth, d, nm = ada_w.shape
    r = cvecs.shape[0]
    tn = _pick(nm, 1024)
    return pl.pallas_call(
        _mod_kernel,
        out_shape=jax.ShapeDtypeStruct((depth, r, nm), F32),
        grid=(depth, nm // tn),
        in_specs=[pl.BlockSpec((r, d), lambda l, j: (0, 0)),
                  pl.BlockSpec((None, d, tn), lambda l, j: (l, 0, j)),
                  pl.BlockSpec((None, 1, tn), lambda l, j: (l, 0, j))],
        out_specs=pl.BlockSpec((None, r, tn), lambda l, j: (l, 0, j)),
        compiler_params=_cparams(("parallel", "parallel")),
        name="adaln_mod",
    )(cvecs, ada_w, ada_b.reshape(depth, 1, nm))


def _rope_kernel(ca_ref, sa_ref, cb_ref, sb_ref, *, log2w):
    s = ca_ref.shape[0]
    t = lax.broadcasted_iota(jnp.int32, (s, 1), 0)
    row = lax.shift_right_logical(t, log2w).astype(F32)
    col = (t & ((1 << log2w) - 1)).astype(F32)

    def tables(width, off, rd):
        lane = lax.broadcasted_iota(jnp.int32, (1, width), 1)
        r = lane - off
        inr = (r >= 0) & (r < rd)
        half, quarter = rd // 2, rd // 4
        is_col = r >= half
        rr = jnp.where(is_col, r - half, r)
        second = rr >= quarter
        j = jnp.where(second, rr - quarter, rr).astype(F32)
        inv = jnp.exp(j * (-2.0 / half * math.log(ROPE_THETA)))
        ang = jnp.where(is_col, col, row) * inv
        sign = jnp.where(second, 1.0, -1.0)
        return jnp.where(inr, jnp.cos(ang), 1.0), jnp.where(inr, sign * jnp.sin(ang), 0.0)

    ca, sa = tables(2 * LANE, MLA_NOPE, MLA_ROPE)
    cb, sb = tables(GQA_HEAD_DIM, 0, GQA_HEAD_DIM)
    ca_ref[...] = ca
    sa_ref[...] = sa
    cb_ref[...] = cb
    sb_ref[...] = sb


def _rope_tables(s):
    log2w = GRID_W.bit_length() - 1
    assert 1 << log2w == GRID_W
    shp = lambda w: jax.ShapeDtypeStruct((s, w), F32)
    return pl.pallas_call(
        functools.partial(_rope_kernel, log2w=log2w),
        out_shape=(shp(2 * LANE), shp(2 * LANE), shp(GQA_HEAD_DIM), shp(GQA_HEAD_DIM)),
        name="rope_tables",
    )()


def _s5_params(a_re, a_im, log_dt, b_re, b_im, c_re, c_im):
    depth, nd = a_re.shape[:2]
    r = depth * nd
    are = a_re.reshape(r, S5_NS)
    aim = a_im.reshape(r, S5_NS)
    ldt = jnp.broadcast_to(log_dt[..., None], (depth, nd, S5_GROUPS, S5_STATE)).reshape(r, S5_NS)
    rep = lambda b: jnp.repeat(jnp.transpose(b, (0, 3, 1, 2)).reshape(depth, 1, S5_GROUP, S5_NS), nd, axis=1)
    bre = rep(b_re).reshape(r, S5_GROUP, S5_NS)
    bim = rep(b_im).reshape(r, S5_GROUP, S5_NS)
    v = jax.ShapeDtypeStruct((r, S5_NS), F32)
    m = jax.ShapeDtypeStruct((r, S5_GROUP, S5_NS), F32)
    ar, ai, bbr, bbi = pl.pallas_call(_s5_param_rows_kernel, out_shape=(v, v, m, m), name="s5_discretise")(
        are, aim, ldt, bre, bim)
    eye = jnp.eye(S5_GROUPS, dtype=F32)

    def dense_b(bb):
        bb = bb.reshape(r, S5_GROUP, S5_GROUPS, S5_STATE)
        return jnp.einsum("dcgp,gh->dgchp", bb, eye).reshape(r, S5_CH, S5_NS)

    def dense_c(cc):
        return jnp.einsum("lgcp,gh->lhpgc", cc, eye).reshape(depth, S5_NS, S5_CH)

    bd = jnp.concatenate([dense_b(bbr), dense_b(bbi)], axis=-1).astype(BF16).reshape(depth, nd, S5_CH, 2 * S5_NS)
    cd = jnp.concatenate([dense_c(c_re), -dense_c(c_im)], axis=1).astype(BF16)
    abar = jnp.concatenate([ar, ai], axis=-1).reshape(depth, nd, 1, 2 * S5_NS)
    return abar, bd, cd


def _s5_param_rows_kernel(are_ref, aim_ref, ldt_ref, bre_ref, bim_ref, oar_ref, oai_ref, obr_ref, obi_ref):
    lr, li = are_ref[...], aim_ref[...]
    dt = jnp.exp(ldt_ref[...])
    mag = jnp.exp(lr * dt)
    ar, ai = mag * jnp.cos(li * dt), mag * jnp.sin(li * dt)
    oar_ref[...] = ar
    oai_ref[...] = ai
    nr, ni = ar - 1.0, ai
    den = lr * lr + li * li
    fr, fi = (nr * lr + ni * li) / den, (ni * lr - nr * li) / den
    for d in range(fr.shape[0]):
        br, bi = bre_ref[d], bim_ref[d]
        obr_ref[d] = fr[d:d + 1] * br - fi[d:d + 1] * bi
        obi_ref[d] = fr[d:d + 1] * bi + fi[d:d + 1] * br


def _ffn_kernel(*refs, row0, slab, final):
    if final:
        x_ref, mod_ref, g_ref, w1_ref, w3_ref, w2_ref, fg_ref, o_ref, h_ref = refs
    else:
        x_ref, mod_ref, g_ref, w1_ref, w3_ref, w2_ref, o_ref, h_ref = refs
    j = pl.program_id(1)
    nslab = h_ref.shape[0] // slab
    nr = math.gcd(slab, NORM_ROWS)

    def partial_ffn(rows):
        h = h_ref[rows, :]
        gate = jnp.dot(h, w1_ref[...], preferred_element_type=F32)
        up = jnp.dot(h, w3_ref[...], preferred_element_type=F32)
        act = (_silu(gate) * up).astype(BF16)
        return jnp.dot(act, w2_ref[...], preferred_element_type=F32)

    @pl.when(j == 0)
    def _():
        for r in range(nslab):
            for p in range(slab // nr):
                rows = slice(r * slab + p * nr, r * slab + (p + 1) * nr)
                h = _modulate(x_ref[rows, :], g_ref[...], mod_ref[row0 + 1:row0 + 2, :], mod_ref[row0:row0 + 1, :])
                h_ref[rows, :] = h.astype(BF16)
            rows = slice(r * slab, (r + 1) * slab)
            o_ref[rows, :] = partial_ffn(rows)

    last = pl.num_programs(1) - 1

    @pl.when((j > 0) & (j < last))
    def _():
        for r in range(nslab):
            rows = slice(r * slab, (r + 1) * slab)
            o_ref[rows, :] += partial_ffn(rows)

    @pl.when(j == last)
    def _():
        gate = 0.5 * mod_ref[row0 + 2:row0 + 3, :]
        for r in range(nslab):
            rows = slice(r * slab, (r + 1) * slab)
            o_ref[rows, :] = x_ref[rows, :] + gate * (o_ref[rows, :] + partial_ffn(rows))
            if final:
                for p in range(slab // nr):
                    piece = slice(r * slab + p * nr, r * slab + (p + 1) * nr)
                    o_ref[piece, :] = _rms(o_ref[piece, :], fg_ref[...])


def _ffn(x, mod, pr, *, layer, which, seq, final_g=None, tm_pref=1024):
    n, d = x.shape
    final = final_g is not None
    groups = mod.shape[0]
    tm = _pick(seq if groups > 1 else n, tm_pref)
    tf = FFN_TILE
    nf = D_FF // tf
    assert nf >= 2
    per = seq // tm if groups > 1 else 1
    mod_map = (lambda i, j: (i // per, 0, 0)) if groups > 1 else (lambda i, j: (0, 0, 0))
    in_specs = [pl.BlockSpec((tm, d), lambda i, j: (i, 0)),
                pl.BlockSpec((None, N_MOD, d), mod_map),
                pl.BlockSpec((None, None, 1, d), lambda i, j: (layer, 2 * which, 0, 0)),
                pl.BlockSpec((None, None, d, tf), lambda i, j: (layer, which, 0, j)),
                pl.BlockSpec((None, None, d, tf), lambda i, j: (layer, which, 0, nf + j)),
                pl.BlockSpec((None, None, tf, d), lambda i, j: (layer, which, j, 0))]
    args = [x, mod, pr["norm_g"], pr["ffn_w13"], pr["ffn_w13"], pr["ffn_w2"]]
    if final:
        in_specs.append(pl.BlockSpec((1, d), lambda i, j: (0, 0)))
        args.append(final_g)
    return pl.pallas_call(
        functools.partial(_ffn_kernel, row0=6 * which, slab=_pick(tm, 512), final=final),
        out_shape=jax.ShapeDtypeStruct((n, d), F32),
        grid=(n // tm, nf),
        in_specs=in_specs,
        out_specs=pl.BlockSpec((tm, d), lambda i, j: (i, 0)),
        scratch_shapes=[pltpu.VMEM((tm, d), BF16)],
        compiler_params=_cparams(("parallel", "arbitrary"), FFN_VMEM_LIMIT_BYTES),
        name="ffn",
    )(*args)


def _inproj_kernel(x_ref, mod_ref, g_ref, w_ref, o_ref):
    h = _modulate(x_ref[...], g_ref[...], mod_ref[4:5, :], mod_ref[3:4, :]).astype(BF16)
    o_ref[...] = jnp.dot(h, w_ref[...], preferred_element_type=F32)


def _inproj(x, mod, pr, *, layer, seq, tm_pref=512):
    n, d = x.shape
    groups = mod.shape[0]
    tm = _pick(seq if groups > 1 else n, tm_pref)
    per = seq // tm if groups > 1 else 1
    mod_map = (lambda i: (i // per, 0, 0)) if groups > 1 else (lambda i: (0, 0, 0))
    return pl.pallas_call(
        _inproj_kernel,
        out_shape=jax.ShapeDtypeStruct((n, PROJ_COLS), F32),
        grid=(n // tm,),
        in_specs=[pl.BlockSpec((tm, d), lambda i: (i, 0)),
                  pl.BlockSpec((None, N_MOD, d), mod_map),
                  pl.BlockSpec((None, None, 1, d), lambda i: (layer, 1, 0, 0)),
                  _layer_spec(pr["w_in"], layer, 1)],
        out_specs=pl.BlockSpec((tm, PROJ_COLS), lambda i: (i, 0)),
        compiler_params=_cparams(("parallel",)),
        name="in_proj",
    )(x, mod, pr["norm_g"], pr["w_in"])


def _mla_kv(ckv_n, kr128, wukv_ref, ka_ref, va_ref):
    kv = jnp.dot(ckv_n.astype(BF16), wukv_ref[...], preferred_element_type=F32)
    krb = kr128.astype(BF16)
    hw = MLA_NOPE + MLA_V
    for h in range(MLA_HEADS):
        ka_ref[:, h * 2 * LANE:h * 2 * LANE + MLA_NOPE] = kv[:, h * hw:h * hw + MLA_NOPE].astype(BF16)
        ka_ref[:, h * 2 * LANE + MLA_NOPE:(h + 1) * 2 * LANE] = krb
        va_ref[:, h * MLA_V:(h + 1) * MLA_V] = kv[:, h * hw + MLA_NOPE:(h + 1) * hw].astype(BF16)


def _attn_prep_kernel(*refs, rope):
    if rope:
        (cq_ref, ckv_ref, krmg_ref, gq_ref, gk_ref, gv_ref, ca_ref, sa_ref, cb_ref, sb_ref,
         qng_ref, kvng_ref, wuq_ref, wukv_ref, gqg_ref, gkg_ref,
         qa_ref, ka_ref, va_ref, qb_ref, kb_ref, vb_ref) = refs
    else:
        (cq_ref, ckv_ref, krmg_ref, gq_ref, gk_ref, gv_ref,
         qng_ref, kvng_ref, wuq_ref, wukv_ref, gqg_ref, gkg_ref,
         qa_ref, ka_ref, va_ref, qb_ref, kb_ref, vb_ref, ckvn_ref, kbn_ref) = refs

    cqn = _rms(cq_ref[...], qng_ref[...]).astype(BF16)
    qa = jnp.dot(cqn, wuq_ref[...], preferred_element_type=F32)
    if rope:
        ca = jnp.concatenate([ca_ref[...]] * MLA_HEADS, axis=1)
        sa = jnp.concatenate([sa_ref[...]] * MLA_HEADS, axis=1)
        qa = qa * ca + _swap_halves(qa, MLA_ROPE // 4) * sa
    qa_ref[...] = (qa * ((MLA_NOPE + MLA_ROPE) ** -0.5 * LOG2E)).astype(BF16)

    ckv_n = _rms(ckv_ref[...], kvng_ref[...])
    if not rope:
        ckvn_ref[...] = ckv_n
    krmg = krmg_ref[...]
    lane = lax.broadcasted_iota(jnp.int32, krmg.shape, 1)
    if rope:
        krmg = krmg * ca_ref[:, LANE:] + _swap_halves(krmg, MLA_ROPE // 4) * sa_ref[:, LANE:]
    kr128 = jnp.where(lane < MLA_ROPE, krmg, 0.0)
    _mla_kv(ckv_n, kr128, wukv_ref, ka_ref, va_ref)

    hd = GQA_HEAD_DIM
    gq, gk = gq_ref[...], gk_ref[...]
    for h in range(GQA_HEADS):
        q = _rms(gq[:, h * hd:(h + 1) * hd], gqg_ref[...])
        if rope:
            q = q * cb_ref[...] + _swap_halves(q, hd // 4) * sb_ref[...]
        qb_ref[:, h * hd:(h + 1) * hd] = (q * (hd ** -0.5 * LOG2E)).astype(BF16)
    for h in range(GQA_KV_HEADS):
        k = _rms(gk[:, h * hd:(h + 1) * hd], gkg_ref[...])
        if rope:
            k = k * cb_ref[...] + _swap_halves(k, hd // 4) * sb_ref[...]
        else:
            kbn_ref[:, h * hd:(h + 1) * hd] = k
        kb_ref[:, h * hd:(h + 1) * hd] = k.astype(BF16)
    vb_ref[...] = gv_ref[...].astype(BF16)


def _attn_prep(proj, tables, pr, *, layer, seq, tm_pref=1024):
    n = proj.shape[0]
    rope = tables is not None
    tm = _pick(seq if rope else n, tm_pref)
    per = max(1, seq // tm)

    def pblock(name, width):
        idx = _COL[name] // width
        assert idx * width == _COL[name]
        return pl.BlockSpec((tm, width), lambda i: (i, idx))

    in_specs = [pblock("cq", MLA_Q_LORA), pblock("ckv", MLA_KV_LORA),
                pl.BlockSpec((tm, LANE), lambda i: (i, KRMG_BLOCK)),
                pblock("gq", 512), pblock("gk", 256), pblock("gv", 256)]
    args = [proj] * 6
    if rope:
        in_specs += [pl.BlockSpec((tm, t.shape[1]), lambda i: (i % per, 0)) for t in tables]
        args += list(tables)
    ws = [pr["mla_q_norm"], pr["mla_kv_norm"], pr["mla_w_uq"], pr["mla_w_ukv"], pr["gqa_q_norm"], pr["gqa_k_norm"]]
    in_specs += [_layer_spec(w, layer, 1) for w in ws]
    args += ws
    gkv = GQA_KV_HEADS * GQA_HEAD_DIM
    widths = [(4 * 2 * LANE, BF16), (4 * 2 * LANE, BF16), (4 * MLA_V, BF16),
              (GQA_HEADS * GQA_HEAD_DIM, BF16), (gkv, BF16), (gkv, BF16)]
    if not rope:
        widths += [(MLA_KV_LORA, F32), (gkv, F32)]
    return pl.pallas_call(
        functools.partial(_attn_prep_kernel, rope=rope),
        out_shape=tuple(jax.ShapeDtypeStruct((n, w), dt) for w, dt in widths),
        grid=(n // tm,),
        in_specs=in_specs,
        out_specs=tuple(pl.BlockSpec((tm, w), lambda i: (i, 0)) for w, _ in widths),
        compiler_params=_cparams(("parallel",)),
        name="attn_prep",
    )(*args)


def _cache_kv_kernel(ckv_ref, kr_ref, wukv_ref, ka_ref, va_ref):
    _mla_kv(ckv_ref[...], kr_ref[...], wukv_ref, ka_ref, va_ref)


def _cache_kv(ckv, kr128, wukv, *, layer):
    b, _, p, _ = ckv.shape
    return pl.pallas_call(
        _cache_kv_kernel,
        out_shape=(jax.ShapeDtypeStruct((b, p, 4 * 2 * LANE), BF16), jax.ShapeDtypeStruct((b, p, 4 * MLA_V), BF16)),
        grid=(b,),
        in_specs=[pl.BlockSpec((None, None, p, MLA_KV_LORA), lambda i: (i, layer, 0, 0)),
                  pl.BlockSpec((None, None, p, LANE), lambda i: (i, layer, 0, 0)),
                  _layer_spec(wukv, layer, 1)],
        out_specs=(pl.BlockSpec((None, p, 4 * 2 * LANE), lambda i: (i, 0, 0)),
                   pl.BlockSpec((None, p, 4 * MLA_V), lambda i: (i, 0, 0))),
        compiler_params=_cparams(("parallel",)),
        name="mla_cache_kv",
    )(ckv, kr128, wukv)


def _attn_kernel(q_ref, *refs, nseg, hp, g, dk, dv):
    o_ref = refs[-1]
    for j in range(hp):
        kv = j // g
        q = q_ref[:, j * dk:(j + 1) * dk]
        ks = [refs[2 * i][:, kv * dk:(kv + 1) * dk].astype(BF16) for i in range(nseg)]
        vs = [refs[2 * i + 1][:, kv * dv:(kv + 1) * dv].astype(BF16) for i in range(nseg)]
        ss = [lax.dot_general(q, k, (((1,), (1,)), ((), ())), preferred_element_type=F32) for k in ks]
        m = functools.reduce(jnp.maximum, [jnp.max(s, axis=-1, keepdims=True) for s in ss])
        ps = [jnp.exp2(s - m) for s in ss]
        l = functools.reduce(jnp.add, [jnp.sum(p, axis=-1, keepdims=True) for p in ps])
        o = functools.reduce(jnp.add, [jnp.dot(p.astype(BF16), v, preferred_element_type=F32)
                                       for p, v in zip(ps, vs)])
        o_ref[:, j * dv:(j + 1) * dv] = o / l


def _attention(q, segs, *, heads, kv_heads, dk, dv, tq_pref=512, hp=4):
    b, sq, _ = q.shape
    tq = _pick(sq, tq_pref)
    g = heads // kv_heads
    assert heads % hp == 0 and (hp % g == 0 or g % hp == 0)
    kvp = max(1, hp // g)
    kcol = (lambda hb: hb) if hp >= g else (lambda hb: hb * hp // g)
    in_specs = [pl.BlockSpec((None, tq, hp * dk), lambda bi, hb, i: (bi, i, hb))]
    args = [q]
    for k, v, layer in segs:
        sk = k.shape[-2]
        if layer is None:
            in_specs += [pl.BlockSpec((None, sk, kvp * dk), lambda bi, hb, i: (bi, 0, kcol(hb))),
                         pl.BlockSpec((None, sk, kvp * dv), lambda bi, hb, i: (bi, 0, kcol(hb)))]
        else:
            in_specs += [pl.BlockSpec((None, None, sk, kvp * dk),
                                      lambda bi, hb, i, layer=layer: (bi, layer, 0, kcol(hb))),
                         pl.BlockSpec((None, None, sk, kvp * dv),
                                      lambda bi, hb, i, layer=layer: (bi, layer, 0, kcol(hb)))]
        args += [k, v]
    return pl.pallas_call(
        functools.partial(_attn_kernel, nseg=len(segs), hp=hp, g=min(g, hp), dk=dk, dv=dv),
        out_shape=jax.ShapeDtypeStruct((b, sq, heads * dv), F32),
        grid=(b, heads // hp, sq // tq),
        in_specs=in_specs,
        out_specs=pl.BlockSpec((None, tq, hp * dv), lambda bi, hb, i: (bi, i, hb)),
        compiler_params=_cparams(("parallel", "parallel", "parallel")),
        name="attention",
    )(*args)


def _log_sigmoid(x):
    return jnp.minimum(x, 0.0) - jnp.log1p(jnp.exp(-jnp.abs(x)))


def _mlstm_chunk_operands(q_ref, k_ref, v_ref, gate_ref, bias_ref):
    nh, hd = MLSTM_HEADS, MLSTM_HEAD_DIM
    g = gate_ref[...] + bias_ref[...]
    lf = _log_sigmoid(g)
    lf_hi = lf.astype(BF16)
    rem = lf - lf_hi.astype(F32)
    lf_mid = rem.astype(BF16)
    lf_lo = (rem - lf_mid.astype(F32)).astype(BF16)
    heads = []
    for h in range(nh):
        q = q_ref[:, h * hd:(h + 1) * hd]
        kh = k_ref[:, h * hd:(h + 1) * hd] * (hd ** -0.5)
        kt = kh.T
        qh = q.astype(BF16)
        heads.append(dict(q=q, qh=qh, kh=kh, kt=kt, vh=v_ref[:, h * hd:(h + 1) * hd].astype(BF16),
                          s=jnp.dot(qh, kt.astype(BF16), preferred_element_type=F32)))
    return dict(g=g, gt=g.T, lf=(lf_hi, lf_mid, lf_lo), heads=heads)


def _prefix_max_lanes(x, rev):
    n = x.shape[1]
    lane = lax.broadcasted_iota(jnp.int32, x.shape, 1)
    k = 1
    while k < n:
        if rev:
            shifted = jnp.where(lane < n - k, pltpu.roll(x, n - k, axis=1), -jnp.inf)
        else:
            shifted = jnp.where(lane >= k, pltpu.roll(x, k, axis=1), -jnp.inf)
        x = jnp.maximum(x, shifted)
        k *= 2
    return x


def _mlstm_chain(ops, h_ref, c_s, n_s, m_s, *, direction):
    nh, hd = MLSTM_HEADS, MLSTM_HEAD_DIM
    rev = direction == 1
    g, gt = ops["g"], ops["gt"]
    cl = g.shape[0]
    ri = lax.broadcasted_iota(jnp.int32, (cl, cl), 0)
    ci = lax.broadcasted_iota(jnp.int32, (cl, cl), 1)
    mask = (ri <= ci) if rev else (ri >= ci)
    tri = jnp.where(mask, 1.0, 0.0).astype(BF16)
    cum = functools.reduce(jnp.add, [jnp.dot(tri, part, preferred_element_type=F32) for part in ops["lf"]])
    cumt = cum.T
    last = 0 if rev else cl - 1
    li0 = MG_LANE + direction * 2 * nh
    lf0 = li0 + nh
    assert li0 % SUBLANE == 0 and 2 * nh == SUBLANE

    b8 = pltpu.roll(gt[li0:li0 + SUBLANE, :], nh, axis=0) - cumt[li0:li0 + SUBLANE, :]
    pm8 = _prefix_max_lanes(b8, rev)
    pm_rows = jnp.concatenate([jnp.zeros((li0, cl), F32), pm8, jnp.zeros((LANE - li0 - SUBLANE, cl), F32)], axis=0)
    pm_c = pm_rows.T
    lane = lax.broadcasted_iota(jnp.int32, (1, LANE), 1)
    m_lanes = functools.reduce(jnp.add, [jnp.where(lane == lf0 + h, m_s[direction, h:h + 1, :], 0.0)
                                         for h in range(nh)])
    mx_all = jnp.maximum(m_lanes, pm_c)
    interw_all = jnp.exp(m_lanes - mx_all)
    em_all = jnp.exp(-(cum + mx_all))

    for h in range(nh):
        hv = ops["heads"][h]
        cum_c = cum[:, lf0 + h:lf0 + h + 1]
        cum_r = cumt[lf0 + h:lf0 + h + 1, :]
        li_r = gt[li0 + h:li0 + h + 1, :]
        li_c = g[:, li0 + h:li0 + h + 1]
        m_prev = m_s[direction, h:h + 1, 0:1]
        w = jnp.where(mask, jnp.exp(b8[nh + h:nh + h + 1, :] - mx_all[:, lf0 + h:lf0 + h + 1]), 0.0)
        sc = hv["s"] * w
        inter_w = interw_all[:, lf0 + h:lf0 + h + 1]
        cmat = c_s[direction, h]
        nrow = n_s[direction, h:h + 1, :]
        num = (jnp.dot(sc.astype(BF16), hv["vh"], preferred_element_type=F32)
               + inter_w * jnp.dot(hv["qh"], cmat.astype(BF16), preferred_element_type=F32))
        qn = jnp.sum(hv["q"] * nrow, axis=1, keepdims=True)
        den = jnp.sum(sc, axis=1, keepdims=True) + inter_w * qn
        h_ref[:, h * hd:(h + 1) * hd] = num / jnp.maximum(jnp.abs(den), em_all[:, lf0 + h:lf0 + h + 1])

        tot = cum_c[last:last + 1, :]
        g_r = tot - cum_r + li_r
        g_c = tot - cum_c + li_c
        m_new = jnp.maximum(tot + m_prev, jnp.max(g_r, axis=1, keepdims=True))
        decay = jnp.exp(tot + m_prev - m_new)
        ws_r = jnp.exp(g_r - m_new)
        ws_c = jnp.exp(g_c - m_new)
        c_s[direction, h] = decay * cmat + jnp.dot((hv["kt"] * ws_r).astype(BF16), hv["vh"],
                                                   preferred_element_type=F32)
        n_s[direction, h:h + 1, :] = decay * nrow + jnp.sum(hv["kh"] * ws_c, axis=0, keepdims=True)
        m_s[direction, h:h + 1, :] = jnp.broadcast_to(m_new, (1, hd))


def _mlstm_kernel(*refs, shared, has_init, emit_state):
    nin = 4 if shared else 8
    fwd_in = refs[:4]
    bwd_in = fwd_in if shared else refs[4:8]
    bias_ref = refs[nin]
    pos = nin + 1
    if has_init:
        c0_ref, n0_ref, m0_ref = refs[pos:pos + 3]
        pos += 3
    hf_ref, hb_ref = refs[pos:pos + 2]
    pos += 2
    if emit_state:
        co_ref, no_ref, mo_ref = refs[pos:pos + 3]
        pos += 3
    c_s, n_s, m_s = refs[pos:pos + 3]
    c = pl.program_id(1)

    @pl.when(c == 0)
    def _():
        if has_init:
            c_s[...] = c0_ref[...]
            n_s[...] = n0_ref[...]
            m_s[...] = m0_ref[...]
        else:
            c_s[...] = jnp.zeros_like(c_s)
            n_s[...] = jnp.zeros_like(n_s)
            m_s[...] = jnp.zeros_like(m_s)

    ops_f = _mlstm_chunk_operands(*fwd_in, bias_ref)
    ops_b = ops_f if shared else _mlstm_chunk_operands(*bwd_in, bias_ref)
    _mlstm_chain(ops_f, hf_ref, c_s, n_s, m_s, direction=0)
    _mlstm_chain(ops_b, hb_ref, c_s, n_s, m_s, direction=1)

    if emit_state:
        @pl.when(c == pl.num_programs(1) - 1)
        def _():
            co_ref[...] = c_s[...]
            no_ref[...] = n_s[...]
            mo_ref[...] = m_s[...]


def _mlstm(proj, bias128, init, *, layer, batch, seq, emit_state, chunk_pref=256):
    n = proj.shape[0]
    cl = _pick(seq, chunk_pref)
    nc = seq // cl
    shared = nc == 1
    nh, hd = MLSTM_HEADS, MLSTM_HEAD_DIM
    rows = (lambda b, c: b * nc + c), (lambda b, c: b * nc + (nc - 1 - c))

    def chunk_specs(row):
        qkv = lambda name: pl.BlockSpec((cl, nh * hd), lambda b, c: (row(b, c), _COL[name] // (nh * hd)))
        return [qkv("mq"), qkv("mk"), qkv("mv"), pl.BlockSpec((cl, LANE), lambda b, c: (row(b, c), KRMG_BLOCK))]

    in_specs = chunk_specs(rows[0]) + ([] if shared else chunk_specs(rows[1]))
    in_specs.append(pl.BlockSpec((None, 1, LANE), lambda b, c: (layer, 0, 0)))
    args = [proj] * (len(in_specs) - 1) + [bias128]
    has_init = init is not None
    if has_init:
        c0, n0, m0 = init
        in_specs += [pl.BlockSpec((None, None, 2, nh, hd, hd), lambda b, c: (b, layer, 0, 0, 0, 0)),
                     pl.BlockSpec((None, None, 2, nh, hd), lambda b, c: (b, layer, 0, 0, 0)),
                     pl.BlockSpec((None, None, 2, nh, hd), lambda b, c: (b, layer, 0, 0, 0))]
        args += [c0, n0, m0]
    out_shape = [jax.ShapeDtypeStruct((n, nh * hd), F32)] * 2
    out_specs = [pl.BlockSpec((cl, nh * hd), lambda b, c, row=row: (row(b, c), 0)) for row in rows]
    if emit_state:
        out_shape += [jax.ShapeDtypeStruct((batch, 2, nh, hd, hd), F32), jax.ShapeDtypeStruct((batch, 2, nh, hd), F32),
                      jax.ShapeDtypeStruct((batch, 2, nh, hd), F32)]
        out_specs += [pl.BlockSpec((None, 2, nh, hd, hd), lambda b, c: (b, 0, 0, 0, 0)),
                      pl.BlockSpec((None, 2, nh, hd), lambda b, c: (b, 0, 0, 0)),
                      pl.BlockSpec((None, 2, nh, hd), lambda b, c: (b, 0, 0, 0))]
    return pl.pallas_call(
        functools.partial(_mlstm_kernel, shared=shared, has_init=has_init, emit_state=emit_state),
        out_shape=tuple(out_shape),
        grid=(batch, nc),
        in_specs=in_specs,
        out_specs=tuple(out_specs),
        scratch_shapes=[pltpu.VMEM((2, nh, hd, hd), F32), pltpu.VMEM((2, nh, hd), F32), pltpu.VMEM((2, nh, hd), F32)],
        compiler_params=_cparams(("parallel", "arbitrary")),
        name="mlstm",
    )(*args)


_S5_STATE_TILES = 2 * S5_NS // MXU_TILE
_S5_GROUPS_PER_TILE = MXU_TILE // S5_STATE


def _s5_channel_tile(state_tile):
    first_group = (state_tile % (S5_NS // MXU_TILE)) * _S5_GROUPS_PER_TILE
    return first_group * S5_GROUP // MXU_TILE


def _to_time_major_kernel(u_ref, p_ref, o_ref):
    rows = p_ref.shape[0]
    sg, ch = u_ref.shape[0], u_ref.shape[2]
    tsteps = rows // sg
    for k in range(u_ref.shape[1] // tsteps):
        u = u_ref[:, k * tsteps:(k + 1) * tsteps, :].reshape(rows, ch).astype(BF16)
        o_ref[k * rows:(k + 1) * rows, :] = jnp.dot(p_ref[...], u, preferred_element_type=F32).astype(BF16)


def _from_time_major_kernel(yf_ref, yb_ref, pt_ref, o_ref):
    rows = pt_ref.shape[0]
    sg, ch = o_ref.shape[0], o_ref.shape[2]
    tsteps = rows // sg
    pt = pt_ref[...]
    for k in range(o_ref.shape[1] // tsteps):
        y = yf_ref[k * rows:(k + 1) * rows, :] + yb_ref[k * rows:(k + 1) * rows, :]
        hi = y.astype(BF16)
        lo = (y - hi.astype(F32)).astype(BF16)
        out = jnp.dot(pt, hi, preferred_element_type=F32) + jnp.dot(pt, lo, preferred_element_type=F32)
        o_ref[:, k * tsteps:(k + 1) * tsteps, :] = out.reshape(sg, tsteps, ch)


def _s5_kernel(*refs, has_init):
    if has_init:
        u_ref, bd_ref, a_ref, cd_ref, x0_ref, y_ref, xf_ref, bu_s, x_s = refs
    else:
        u_ref, bd_ref, a_ref, cd_ref, y_ref, xf_ref, bu_s, x_s = refs
    d = pl.program_id(0)
    c = pl.program_id(2)
    ns = S5_NS
    sg = x_s.shape[0]
    tsteps = u_ref.shape[0] // sg
    mt = MXU_TILE

    @pl.when(c == 0)
    def _():
        if has_init:
            x_s[...] = x0_ref[...]
        else:
            x_s[...] = jnp.zeros_like(x_s)

    u = u_ref[...]
    for st in range(_S5_STATE_TILES):
        ct = _s5_channel_tile(st)
        bu_s[:, st * mt:(st + 1) * mt] = jnp.dot(u[:, ct * mt:(ct + 1) * mt],
                                                 bd_ref[ct * mt:(ct + 1) * mt, st * mt:(st + 1) * mt],
                                                 preferred_element_type=F32)
    a_re = jnp.broadcast_to(a_ref[:, :ns], (sg, ns))
    a_im = jnp.broadcast_to(a_ref[:, ns:], (sg, ns))

    def body(t, carry):
        xr, xi = carry
        tt = t + d * (tsteps - 1 - 2 * t)
        r0 = pl.multiple_of(tt * sg, sg)
        br = bu_s[pl.ds(r0, sg), :ns]
        bi = bu_s[pl.ds(r0, sg), ns:]
        nr = a_re * xr - a_im * xi + br
        ni = a_re * xi + a_im * xr + bi
        bu_s[pl.ds(r0, sg), :ns] = nr
        bu_s[pl.ds(r0, sg), ns:] = ni
        return nr, ni

    xr, xi = lax.fori_loop(0, tsteps, body, (x_s[:, :ns], x_s[:, ns:]), unroll=SCAN_UNROLL)
    x_s[:, :ns] = xr
    x_s[:, ns:] = xi

    for ct in range(S5_CH // mt):
        acc = None
        for st in range(_S5_STATE_TILES):
            if _s5_channel_tile(st) != ct:
                continue
            t = jnp.dot(bu_s[:, st * mt:(st + 1) * mt].astype(BF16),
                        cd_ref[st * mt:(st + 1) * mt, ct * mt:(ct + 1) * mt], preferred_element_type=F32)
            acc = t if acc is None else acc + t
        y_ref[:, ct * mt:(ct + 1) * mt] = acc

    @pl.when(c == pl.num_programs(2) - 1)
    def _():
        xf_ref[...] = x_s[...]


def _s5(proj3, pr, x0, *, layer, rows_pref=512, scan_rows_pref=1024):
    batch, seq, _ = proj3.shape
    sg = SUBLANE if batch % SUBLANE == 0 else batch
    ng = batch // sg
    tsteps = _pick(seq, max(SUBLANE, rows_pref // sg))
    rows = sg * tsteps
    nc = seq // tsteps
    r = jnp.arange(rows)
    perm = (jnp.arange(rows)[None, :] == ((r % sg) * tsteps + r // sg)[:, None]).astype(BF16)
    const2 = lambda a: pl.BlockSpec(a.shape, lambda g, c: (0,) * a.ndim, pipeline_mode=pl.Buffered(1))
    nsub = math.gcd(nc, RELAYOUT_SUB)
    nc_r = nc // nsub

    u_tm = pl.pallas_call(
        _to_time_major_kernel,
        out_shape=jax.ShapeDtypeStruct((ng, seq * sg, S5_CH), BF16),
        grid=(ng, nc_r),
        in_specs=[pl.BlockSpec((sg, nsub * tsteps, S5_CH), lambda g, c: (g, c, _COL["su"] // S5_CH)), const2(perm)],
        out_specs=pl.BlockSpec((None, nsub * rows, S5_CH), lambda g, c: (g, c, 0)),
        compiler_params=_cparams(("parallel", "parallel")),
        name="s5_to_time_major",
    )(proj3, perm)

    ts_scan = _pick(seq, max(SUBLANE, scan_rows_pref // sg))
    rows_scan = sg * ts_scan
    nc_scan = seq // ts_scan
    cpos = lambda d, c: c + d * (nc_scan - 1 - 2 * c)
    has_init = x0 is not None
    in_specs = [pl.BlockSpec((None, rows_scan, S5_CH), lambda d, g, c: (g, cpos(d, c), 0)),
                pl.BlockSpec((None, None, S5_CH, 2 * S5_NS), lambda d, g, c: (layer, d, 0, 0)),
                pl.BlockSpec((None, None, 1, 2 * S5_NS), lambda d, g, c: (layer, d, 0, 0)),
                _layer_spec(pr["s5_cd"], layer, 3)]
    args = [u_tm, pr["s5_bd"], pr["s5_abar"], pr["s5_cd"]]
    if has_init:
        in_specs.append(pl.BlockSpec((None, sg, 2 * S5_NS), lambda d, g, c: (d, g, 0)))
        args.append(x0)
    y_tm, xfin = pl.pallas_call(
        functools.partial(_s5_kernel, has_init=has_init),
        out_shape=(jax.ShapeDtypeStruct((2, ng, seq * sg, S5_CH), F32),
                   jax.ShapeDtypeStruct((2, batch, 2 * S5_NS), F32)),
        grid=(2, ng, nc_scan),
        in_specs=in_specs,
        out_specs=(pl.BlockSpec((None, None, rows_scan, S5_CH), lambda d, g, c: (d, g, cpos(d, c), 0)),
                   pl.BlockSpec((None, sg, 2 * S5_NS), lambda d, g, c: (d, g, 0))),
        scratch_shapes=[pltpu.VMEM((rows_scan, 2 * S5_NS), F32), pltpu.VMEM((sg, 2 * S5_NS), F32)],
        compiler_params=_cparams(("parallel", "parallel", "arbitrary")),
        name="s5_scan",
    )(*args)

    y = pl.pallas_call(
        _from_time_major_kernel,
        out_shape=jax.ShapeDtypeStruct((batch, seq, S5_CH), F32),
        grid=(ng, nc_r),
        in_specs=[pl.BlockSpec((None, None, nsub * rows, S5_CH), lambda g, c: (0, g, c, 0)),
                  pl.BlockSpec((None, None, nsub * rows, S5_CH), lambda g, c: (1, g, c, 0)),
                  const2(perm)],
        out_specs=pl.BlockSpec((sg, nsub * tsteps, S5_CH), lambda g, c: (g, c, 0)),
        compiler_params=_cparams(("parallel", "parallel")),
        name="s5_from_time_major",
    )(y_tm, y_tm, perm.T)
    return y, xfin


def _gelu_tanh(x):
    return 0.5 * x * (1.0 + jnp.tanh(math.sqrt(2.0 / math.pi) * (x + 0.044715 * (x * x * x))))


def _outproj_kernel(x_ref, mod_ref, oa_ref, ob_ref, hf_ref, hb_ref, mo_ref, y_ref, su_ref,
                    d_ref, wglu_ref, on_ref, wout_ref, o_ref):
    slab = math.gcd(x_ref.shape[0], OUTPROJ_SLAB)
    for r in range(x_ref.shape[0] // slab):
        rows = slice(r * slab, (r + 1) * slab)
        oc = jax.nn.sigmoid(mo_ref[rows, :]) * (hf_ref[rows, :] + hb_ref[rows, :])
        y = _gelu_tanh(y_ref[rows, :] + d_ref[...] * su_ref[rows, :])
        od = y * jax.nn.sigmoid(jnp.dot(y.astype(BF16), wglu_ref[...], preferred_element_type=F32))
        acc = None
        for gi, part in enumerate((oa_ref[rows, :], ob_ref[rows, :], oc, od)):
            nrm = _rms(part, on_ref[gi:gi + 1, :]).astype(BF16)
            t = jnp.dot(nrm, wout_ref[gi * GROUP_WIDTH:(gi + 1) * GROUP_WIDTH, :], preferred_element_type=F32)
            acc = t if acc is None else acc + t
        o_ref[rows, :] = x_ref[rows, :] + mod_ref[5:6, :] * acc


def _outproj(x, mod, oa, ob, hf, hb, proj, y, pr, *, layer, seq, tm_pref=512):
    n, d = x.shape
    groups = mod.shape[0]
    tm = _pick(seq if groups > 1 else n, tm_pref)
    per = seq // tm if groups > 1 else 1
    mod_map = (lambda i: (i // per, 0, 0)) if groups > 1 else (lambda i: (0, 0, 0))
    gw = GROUP_WIDTH
    rowblk = pl.BlockSpec((tm, gw), lambda i: (i, 0))
    ws = [pr["s5_d"], pr["s5_w_glu"], pr["out_norm"], pr["w_out"]]
    return pl.pallas_call(
        _outproj_kernel,
        out_shape=jax.ShapeDtypeStruct((n, d), F32),
        grid=(n // tm,),
        in_specs=[pl.BlockSpec((tm, d), lambda i: (i, 0)),
                  pl.BlockSpec((None, N_MOD, d), mod_map),
                  rowblk, rowblk, rowblk, rowblk,
                  pl.BlockSpec((tm, gw), lambda i: (i, _COL["mo"] // gw)),
                  rowblk,
                  pl.BlockSpec((tm, gw), lambda i: (i, _COL["su"] // gw))] + [_layer_spec(w, layer, 1) for w in ws],
        out_specs=pl.BlockSpec((tm, d), lambda i: (i, 0)),
        compiler_params=_cparams(("parallel",)),
        name="merge_out_proj",
    )(x, mod, oa, ob, hf, hb, proj, y, proj, *ws)


def _trunk_layer(x, mod, pr, *, layer, batch, seq, tables, cache, final_g=None):
    latent = cache is not None
    x = _ffn(x, mod, pr, layer=layer, which=0, seq=seq)
    proj = _inproj(x, mod, pr, layer=layer, seq=seq)

    prep = _attn_prep(proj, tables, pr, layer=layer, seq=seq)
    qa, ka, va, qb, kb, vb = prep[:6]
    r3 = lambda a: a.reshape(batch, seq, a.shape[-1])
    segs_a = [(r3(ka), r3(va), None)]
    segs_b = [(r3(kb), r3(vb), None)]
    if latent:
        segs_a.insert(0, cache["mla_kv"] + (None,))
        segs_b.insert(0, cache["gqa_kv"] + (layer,))
    oa = _attention(r3(qa), segs_a, heads=MLA_HEADS, kv_heads=MLA_HEADS, dk=2 * LANE, dv=MLA_V)
    ob = _attention(r3(qb), segs_b, heads=GQA_HEADS, kv_heads=GQA_KV_HEADS, dk=GQA_HEAD_DIM, dv=GQA_HEAD_DIM)
    oa = oa.reshape(batch * seq, -1)
    ob = ob.reshape(batch * seq, -1)

    m_init = cache["mlstm"] if latent else None
    mres = _mlstm(proj, pr["mlstm_bias"], m_init, layer=layer, batch=batch, seq=seq, emit_state=not latent)

    y, xfin = _s5(proj.reshape(batch, seq, PROJ_COLS), pr, cache["s5"] if latent else None, layer=layer)

    x = _outproj(x, mod, oa, ob, mres[0], mres[1], proj, y.reshape(batch * seq, S5_CH), pr,
                 layer=layer, seq=seq)
    x = _ffn(x, mod, pr, layer=layer, which=1, seq=seq, final_g=final_g)

    new_ctx = None
    if not latent:
        ckvn, kbn = prep[6:]
        kr = proj[:, _COL["kr"]:_COL["kr"] + MLA_ROPE]
        gv = proj[:, _COL["gv"]:_COL["gv"] + GQA_KV_HEADS * GQA_HEAD_DIM]
        xs = xfin.reshape(2, batch, 2, S5_GROUPS, S5_STATE).transpose(1, 0, 2, 3, 4)
        new_ctx = (ckvn.reshape(batch, seq, MLA_KV_LORA),
                   kr.reshape(batch, seq, MLA_ROPE),
                   kbn.reshape(batch, seq, GQA_KV_HEADS, GQA_HEAD_DIM),
                   gv.reshape(batch, seq, GQA_KV_HEADS, GQA_HEAD_DIM),
                   mres[2], mres[3], mres[4][..., 0],
                   xs[:, :, 0], xs[:, :, 1])
    return x, new_ctx


def _permute_w_in(w_in):
    parts = [w_in[..., _ORIG[n][0]:_ORIG[n][0] + _ORIG[n][1]] for n in _ORDER]
    pad = PROJ_COLS - sum(p.shape[-1] for p in parts)
    parts.append(jnp.zeros(w_in.shape[:-1] + (pad,), w_in.dtype))
    return jnp.concatenate(parts, axis=-1).astype(BF16)


def _permute_w_uq(w_uq):
    depth, k, _ = w_uq.shape
    w = w_uq.reshape(depth, k, MLA_HEADS, MLA_NOPE + MLA_ROPE)
    w = jnp.pad(w, ((0, 0), (0, 0), (0, 0), (0, 2 * LANE - MLA_NOPE - MLA_ROPE)))
    return w.reshape(depth, k, MLA_HEADS * 2 * LANE).astype(BF16)


def kernel(x_prompt, x_sample, cache_mla_ckv, cache_mla_krope, cache_gqa_k, cache_gqa_v, state_mlstm_c, state_mlstm_n, state_mlstm_m, state_s5_re, state_s5_im, c, c_ctx, ada_w, ada_b, norm_g, ffn_w13, ffn_w2, w_in, mla_q_norm, mla_kv_norm, mla_w_uq, mla_w_ukv, gqa_q_norm, gqa_k_norm, mlstm_gate_b, s5_a_re, s5_a_im, s5_log_dt, s5_b_re, s5_b_im, s5_c_re, s5_c_im, s5_d, s5_w_glu, out_norm, w_out, final_norm):
    bc, sc, d = x_prompt.shape
    bl, sl, _ = x_sample.shape
    depth = ada_w.shape[0]
    past = cache_mla_ckv.shape[2]

    rows = ((1 + bl + SUBLANE - 1) // SUBLANE) * SUBLANE
    cvecs = jnp.concatenate([c_ctx[None, :], c, jnp.zeros((rows - 1 - bl, d), F32)], axis=0)
    mod_all = _modulation(cvecs, ada_w, ada_b).reshape(depth, rows, N_MOD, d)

    tables = _rope_tables(sl)

    abar, bd, cd = _s5_params(s5_a_re, s5_a_im, s5_log_dt, s5_b_re, s5_b_im, s5_c_re, s5_c_im)
    nmg = MLSTM_HEADS * 4
    row = lambda a: a.reshape(depth, 1, a.shape[-1])
    pr = {"norm_g": norm_g.reshape(depth, 3, 1, d),
          "ffn_w13": ffn_w13.astype(BF16), "ffn_w2": ffn_w2.astype(BF16),
          "w_in": _permute_w_in(w_in),
          "mla_q_norm": row(mla_q_norm), "mla_kv_norm": row(mla_kv_norm),
          "mla_w_uq": _permute_w_uq(mla_w_uq), "mla_w_ukv": mla_w_ukv.astype(BF16),
          "gqa_q_norm": row(gqa_q_norm), "gqa_k_norm": row(gqa_k_norm),
          "mlstm_bias": jnp.pad(mlstm_gate_b.reshape(depth, 1, nmg), ((0, 0), (0, 0), (MG_LANE, LANE - MG_LANE - nmg))),
          "s5_abar": abar, "s5_bd": bd, "s5_cd": cd,
          "s5_d": row(s5_d), "s5_w_glu": s5_w_glu.astype(BF16),
          "out_norm": out_norm.reshape(depth, 4, GROUP_WIDTH), "w_out": w_out.astype(BF16)}

    gkv = GQA_KV_HEADS * GQA_HEAD_DIM
    cache_k = cache_gqa_k.reshape(bl, depth, past, gkv)
    cache_v = cache_gqa_v.reshape(bl, depth, past, gkv)
    m0 = jnp.broadcast_to(state_mlstm_m[..., None], state_mlstm_n.shape)
    kr_pad = jnp.pad(cache_mla_krope, ((0, 0), (0, 0), (0, 0), (0, LANE - MLA_ROPE)))

    x_ctx = x_prompt.reshape(bc * sc, d)
    x_lat = x_sample.reshape(bl * sl, d)
    per_layer = []
    for l in range(depth):
        final_g = final_norm[None, :] if l == depth - 1 else None
        x_ctx, ctx_l = _trunk_layer(x_ctx, mod_all[l, 0:1], pr, layer=l, batch=bc, seq=sc, tables=None, cache=None,
                                    final_g=final_g)
        per_layer.append(ctx_l)

        x0 = jnp.concatenate([state_s5_re[:, l].reshape(bl, 2, S5_NS), state_s5_im[:, l].reshape(bl, 2, S5_NS)],
                             axis=-1).transpose(1, 0, 2)
        cache = {"mla_kv": tuple(_cache_kv(cache_mla_ckv, kr_pad, pr["mla_w_ukv"], layer=l)),
                 "gqa_kv": (cache_k, cache_v),
                 "mlstm": (state_mlstm_c, state_mlstm_n, m0),
                 "s5": x0}
        x_lat, _ = _trunk_layer(x_lat, mod_all[l, 1:1 + bl], pr, layer=l, batch=bl, seq=sl, tables=tables,
                                cache=cache, final_g=final_g)

    new_ctx = [jnp.stack([t[i] for t in per_layer], axis=1) for i in range(9)]
    return (x_ctx.reshape(bc, sc, d), x_lat.reshape(bl, sl, d), *new_ctx)
```

```python
import functools
import math

import jax
import jax.numpy as jnp
from jax import lax
from jax.experimental import pallas as pl
from jax.experimental.pallas import tpu as pltpu

F32 = jnp.float32
BF16 = jnp.bfloat16

EPS = 1e-6
ROPE_THETA = 10000.0
GRID_W = 64
N_MOD = 9
LOG2E = math.log2(math.e)

D_FF = 5632
MLA_HEADS, MLA_NOPE, MLA_ROPE, MLA_V = 4, 128, 64, 128
MLA_Q_LORA, MLA_KV_LORA = 384, 256
GQA_HEADS, GQA_KV_HEADS, GQA_HEAD_DIM = 4, 2, 128
MLSTM_HEADS, MLSTM_HEAD_DIM = 4, 128
S5_GROUPS, S5_GROUP, S5_STATE = 32, 16, 64
S5_CH = S5_GROUPS * S5_GROUP
S5_NS = S5_GROUPS * S5_STATE
GROUP_WIDTH = 512

LANE = 128
SUBLANE = 8
MXU_TILE = 256
NORM_ROWS = 128
FFN_TILE = 512
RELAYOUT_SUB = 4
OUTPROJ_SLAB = 256
SCAN_UNROLL = 4
VMEM_BYTES = 64 * 1024 * 1024
VMEM_LIMIT_BYTES = VMEM_BYTES - 8 * 1024 * 1024
FFN_VMEM_LIMIT_BYTES = VMEM_BYTES - 4 * 1024 * 1024

_ORIG = dict(cq=(0, 384), ckv=(384, 256), kr=(640, 64), gq=(704, 512), gk=(1216, 256), gv=(1472, 256),
             mq=(1728, 512), mk=(2240, 512), mv=(2752, 512), mo=(3264, 512), mg=(3776, 16), su=(3792, 512))
_ORDER = ("mq", "mk", "mv", "mo", "gq", "su", "gk", "gv", "ckv", "cq", "kr", "mg")
_COL = {}
_off = 0
for _n in _ORDER:
    _COL[_n] = _off
    _off += _ORIG[_n][1]
PROJ_COLS = ((_off + LANE - 1) // LANE) * LANE
KRMG_BLOCK = _COL["kr"] // LANE
MG_LANE = _COL["mg"] - _COL["kr"]


def _cparams(sem, vmem_limit=VMEM_LIMIT_BYTES):
    return pltpu.CompilerParams(dimension_semantics=sem, vmem_limit_bytes=vmem_limit)


def _pick(n, pref):
    for t in range(min(n, pref), 0, -1):
        if n % t == 0 and (t % SUBLANE == 0 or t == n):
            return t
    return n


def _layer_spec(arr, layer, nargs):
    zeros = (0,) * (arr.ndim - 1)
    imap = {1: lambda i: (layer,) + zeros, 2: lambda i, j: (layer,) + zeros,
            3: lambda i, j, k: (layer,) + zeros}[nargs]
    return pl.BlockSpec((None,) + arr.shape[1:], imap, pipeline_mode=pl.Buffered(1))


def _rms(x, g):
    return x * lax.rsqrt(jnp.mean(x * x, axis=-1, keepdims=True) + EPS) * g


def _modulate(x, g, scale, shift):
    return x * lax.rsqrt(jnp.mean(x * x, axis=-1, keepdims=True) + EPS) * (g * (1.0 + scale)) + shift


def _swap_halves(x, half):
    w = x.shape[-1]
    lane = lax.broadcasted_iota(jnp.int32, x.shape, x.ndim - 1)
    first = (lane & (2 * half - 1)) < half
    return jnp.where(first, pltpu.roll(x, w - half, axis=x.ndim - 1), pltpu.roll(x, half, axis=x.ndim - 1))


def _silu(x):
    return x * jax.nn.sigmoid(x)


def _mod_kernel(c_ref, w_ref, b_ref, o_ref):
    a = _silu(c_ref[...]).astype(BF16)
    o_ref[...] = jnp.dot(a, w_ref[...].astype(BF16), preferred_element_type=F32) + b_ref[...]


def _modulation(cvecs, ada_w, ada_b):
    depth, d, nm = ada_w.shape
    r = cvecs.shape[0]
    tn = _pick(nm, 1024)
    return pl.pallas_call(
        _mod_kernel,
        out_shape=jax.ShapeDtypeStruct((depth, r, nm), F32),
        grid=(depth, nm // tn),
        in_specs=[pl.BlockSpec((r, d), lambda l, j: (0, 0)),
                  pl.BlockSpec((None, d, tn), lambda l, j: (l, 0, j)),
                  pl.BlockSpec((None, 1, tn), lambda l, j: (l, 0, j))],
        out_specs=pl.BlockSpec((None, r, tn), lambda l, j: (l, 0, j)),
        compiler_params=_cparams(("parallel", "parallel")),
        name="adaln_mod",
    )(cvecs, ada_w, ada_b.reshape(depth, 1, nm))


def _rope_kernel(ca_ref, sa_ref, cb_ref, sb_ref, *, log2w):
    s = ca_ref.shape[0]
    t = lax.broadcasted_iota(jnp.int32, (s, 1), 0)
    row = lax.shift_right_logical(t, log2w).astype(F32)
    col = (t & ((1 << log2w) - 1)).astype(F32)

    def tables(width, off, rd):
        lane = lax.broadcasted_iota(jnp.int32, (1, width), 1)
        r = lane - off
        inr = (r >= 0) & (r < rd)
        half, quarter = rd // 2, rd // 4
        is_col = r >= half
        rr = jnp.where(is_col, r - half, r)
        second = rr >= quarter
        j = jnp.where(second, rr - quarter, rr).astype(F32)
        inv = jnp.exp(j * (-2.0 / half * math.log(ROPE_THETA)))
        ang = jnp.where(is_col, col, row) * inv
        sign = jnp.where(second, 1.0, -1.0)
        return jnp.where(inr, jnp.cos(ang), 1.0), jnp.where(inr, sign * jnp.sin(ang), 0.0)

    ca, sa = tables(2 * LANE, MLA_NOPE, MLA_ROPE)
    cb, sb = tables(GQA_HEAD_DIM, 0, GQA_HEAD_DIM)
    ca_ref[...] = ca
    sa_ref[...] = sa
    cb_ref[...] = cb
    sb_ref[...] = sb


def _rope_tables(s):
    log2w = GRID_W.bit_length() - 1
    assert 1 << log2w == GRID_W
    shp = lambda w: jax.ShapeDtypeStruct((s, w), F32)
    return pl.pallas_call(
        functools.partial(_rope_kernel, log2w=log2w),
        out_shape=(shp(2 * LANE), shp(2 * LANE), shp(GQA_HEAD_DIM), shp(GQA_HEAD_DIM)),
        name="rope_tables",
    )()


def _s5_params(a_re, a_im, log_dt, b_re, b_im, c_re, c_im):
    depth, nd = a_re.shape[:2]
    r = depth * nd
    are = a_re.reshape(r, S5_NS)
    aim = a_im.reshape(r, S5_NS)
    ldt = jnp.broadcast_to(log_dt[..., None], (depth, nd, S5_GROUPS, S5_STATE)).reshape(r, S5_NS)
    rep = lambda b: jnp.repeat(jnp.transpose(b, (0, 3, 1, 2)).reshape(depth, 1, S5_GROUP, S5_NS), nd, axis=1)
    bre = rep(b_re).reshape(r, S5_GROUP, S5_NS)
    bim = rep(b_im).reshape(r, S5_GROUP, S5_NS)
    v = jax.ShapeDtypeStruct((r, S5_NS), F32)
    m = jax.ShapeDtypeStruct((r, S5_GROUP, S5_NS), F32)
    ar, ai, bbr, bbi = pl.pallas_call(_s5_param_rows_kernel, out_shape=(v, v, m, m), name="s5_discretise")(
        are, aim, ldt, bre, bim)
    eye = jnp.eye(S5_GROUPS, dtype=F32)

    def dense_b(bb):
        bb = bb.reshape(r, S5_GROUP, S5_GROUPS, S5_STATE)
        return jnp.einsum("dcgp,gh->dgchp", bb, eye).reshape(r, S5_CH, S5_NS)

    def dense_c(cc):
        return jnp.einsum("lgcp,gh->lhpgc", cc, eye).reshape(depth, S5_NS, S5_CH)

    bd = jnp.concatenate([dense_b(bbr), dense_b(bbi)], axis=-1).astype(BF16).reshape(depth, nd, S5_CH, 2 * S5_NS)
    cd = jnp.concatenate([dense_c(c_re), -dense_c(c_im)], axis=1).astype(BF16)
    abar = jnp.concatenate([ar, ai], axis=-1).reshape(depth, nd, 1, 2 * S5_NS)
    return abar, bd, cd


def _s5_param_rows_kernel(are_ref, aim_ref, ldt_ref, bre_ref, bim_ref, oar_ref, oai_ref, obr_ref, obi_ref):
    lr, li = are_ref[...], aim_ref[...]
    dt = jnp.exp(ldt_ref[...])
    mag = jnp.exp(lr * dt)
    ar, ai = mag * jnp.cos(li * dt), mag * jnp.sin(li * dt)
    oar_ref[...] = ar
    oai_ref[...] = ai
    nr, ni = ar - 1.0, ai
    den = lr * lr + li * li
    fr, fi = (nr * lr + ni * li) / den, (ni * lr - nr * li) / den
    for d in range(fr.shape[0]):
        br, bi = bre_ref[d], bim_ref[d]
        obr_ref[d] = fr[d:d + 1] * br - fi[d:d + 1] * bi
        obi_ref[d] = fr[d:d + 1] * bi + fi[d:d + 1] * br


def _ffn_kernel(*refs, row0, slab, final):
    if final:
        x_ref, mod_ref, g_ref, w1_ref, w3_ref, w2_ref, fg_ref, o_ref, h_ref = refs
    else:
        x_ref, mod_ref, g_ref, w1_ref, w3_ref, w2_ref, o_ref, h_ref = refs
    j = pl.program_id(1)
    nslab = h_ref.shape[0] // slab
    nr = math.gcd(slab, NORM_ROWS)

    def partial_ffn(rows):
        h = h_ref[rows, :]
        gate = jnp.dot(h, w1_ref[...], preferred_element_type=F32)
        up = jnp.dot(h, w3_ref[...], preferred_element_type=F32)
        act = (_silu(gate) * up).astype(BF16)
        return jnp.dot(act, w2_ref[...].astype(BF16), preferred_element_type=F32)

    @pl.when(j == 0)
    def _():
        for r in range(nslab):
            for p in range(slab // nr):
                rows = slice(r * slab + p * nr, r * slab + (p + 1) * nr)
                h = _modulate(x_ref[rows, :], g_ref[...], mod_ref[row0 + 1:row0 + 2, :], mod_ref[row0:row0 + 1, :])
                h_ref[rows, :] = h.astype(BF16)
            rows = slice(r * slab, (r + 1) * slab)
            o_ref[rows, :] = partial_ffn(rows)

    last = pl.num_programs(1) - 1

    @pl.when((j > 0) & (j < last))
    def _():
        for r in range(nslab):
            rows = slice(r * slab, (r + 1) * slab)
            o_ref[rows, :] += partial_ffn(rows)

    @pl.when(j == last)
    def _():
        gate = 0.5 * mod_ref[row0 + 2:row0 + 3, :]
        for r in range(nslab):
            rows = slice(r * slab, (r + 1) * slab)
            o_ref[rows, :] = x_ref[rows, :] + gate * (o_ref[rows, :] + partial_ffn(rows))
            if final:
                for p in range(slab // nr):
                    piece = slice(r * slab + p * nr, r * slab + (p + 1) * nr)
                    o_ref[piece, :] = _rms(o_ref[piece, :], fg_ref[...])


def _ffn(x, mod, pr, *, layer, which, seq, final_g=None, tm_pref=1024):
    n, d = x.shape
    final = final_g is not None
    groups = mod.shape[0]
    tm = _pick(seq if groups > 1 else n, tm_pref)
    tf = FFN_TILE
    nf = D_FF // tf
    assert nf >= 2
    per = seq // tm if groups > 1 else 1
    mod_map = (lambda i, j: (i // per, 0, 0)) if groups > 1 else (lambda i, j: (0, 0, 0))
    in_specs = [pl.BlockSpec((tm, d), lambda i, j: (i, 0)),
                pl.BlockSpec((None, N_MOD, d), mod_map),
                pl.BlockSpec((None, None, 1, d), lambda i, j: (layer, 2 * which, 0, 0)),
                pl.BlockSpec((None, None, d, tf), lambda i, j: (layer, which, 0, j)),
                pl.BlockSpec((None, None, d, tf), lambda i, j: (layer, which, 0, nf + j)),
                pl.BlockSpec((None, None, tf, d), lambda i, j: (layer, which, j, 0))]
    args = [x, mod, pr["norm_g"], pr["ffn_w13"], pr["ffn_w13"], pr["ffn_w2"]]
    if final:
        in_specs.append(pl.BlockSpec((1, d), lambda i, j: (0, 0)))
        args.append(final_g)
    return pl.pallas_call(
        functools.partial(_ffn_kernel, row0=6 * which, slab=_pick(tm, 512), final=final),
        out_shape=jax.ShapeDtypeStruct((n, d), F32),
        grid=(n // tm, nf),
        in_specs=in_specs,
        out_specs=pl.BlockSpec((tm, d), lambda i, j: (i, 0)),
        scratch_shapes=[pltpu.VMEM((tm, d), BF16)],
        compiler_params=_cparams(("parallel", "arbitrary"), FFN_VMEM_LIMIT_BYTES),
        name="ffn",
    )(*args)


def _inproj_kernel(x_ref, mod_ref, g_ref, w_ref, o_ref):
    h = _modulate(x_ref[...], g_ref[...], mod_ref[4:5, :], mod_ref[3:4, :]).astype(BF16)
    o_ref[...] = jnp.dot(h, w_ref[...], preferred_element_type=F32)


def _inproj(x, mod, pr, *, layer, seq, tm_pref=512):
    n, d = x.shape
    groups = mod.shape[0]
    tm = _pick(seq if groups > 1 else n, tm_pref)
    per = seq // tm if groups > 1 else 1
    mod_map = (lambda i: (i // per, 0, 0)) if groups > 1 else (lambda i: (0, 0, 0))
    return pl.pallas_call(
        _inproj_kernel,
        out_shape=jax.ShapeDtypeStruct((n, PROJ_COLS), F32),
        grid=(n // tm,),
        in_specs=[pl.BlockSpec((tm, d), lambda i: (i, 0)),
                  pl.BlockSpec((None, N_MOD, d), mod_map),
                  pl.BlockSpec((None, None, 1, d), lambda i: (layer, 1, 0, 0)),
                  _layer_spec(pr["w_in"], layer, 1)],
        out_specs=pl.BlockSpec((tm, PROJ_COLS), lambda i: (i, 0)),
        compiler_params=_cparams(("parallel",)),
        name="in_proj",
    )(x, mod, pr["norm_g"], pr["w_in"])


def _mla_kv(ckv_n, kr128, wukv_ref, ka_ref, va_ref):
    kv = jnp.dot(ckv_n.astype(BF16), wukv_ref[...], preferred_element_type=F32)
    krb = kr128.astype(BF16)
    hw = MLA_NOPE + MLA_V
    for h in range(MLA_HEADS):
        ka_ref[:, h * 2 * LANE:h * 2 * LANE + MLA_NOPE] = kv[:, h * hw:h * hw + MLA_NOPE].astype(BF16)
        ka_ref[:, h * 2 * LANE + MLA_NOPE:(h + 1) * 2 * LANE] = krb
        va_ref[:, h * MLA_V:(h + 1) * MLA_V] = kv[:, h * hw + MLA_NOPE:(h + 1) * hw].astype(BF16)


def _attn_prep_kernel(*refs, rope):
    if rope:
        (cq_ref, ckv_ref, krmg_ref, gq_ref, gk_ref, gv_ref, ca_ref, sa_ref, cb_ref, sb_ref,
         qng_ref, kvng_ref, wuq_ref, wukv_ref, gqg_ref, gkg_ref,
         qa_ref, ka_ref, va_ref, qb_ref, kb_ref, vb_ref) = refs
    else:
        (cq_ref, ckv_ref, krmg_ref, gq_ref, gk_ref, gv_ref,
         qng_ref, kvng_ref, wuq_ref, wukv_ref, gqg_ref, gkg_ref,
         qa_ref, ka_ref, va_ref, qb_ref, kb_ref, vb_ref, ckvn_ref, kbn_ref) = refs

    cqn = _rms(cq_ref[...], qng_ref[...]).astype(BF16)
    qa = jnp.dot(cqn, wuq_ref[...], preferred_element_type=F32)
    if rope:
        ca = jnp.concatenate([ca_ref[...]] * MLA_HEADS, axis=1)
        sa = jnp.concatenate([sa_ref[...]] * MLA_HEADS, axis=1)
        qa = qa * ca + _swap_halves(qa, MLA_ROPE // 4) * sa
    qa_ref[...] = (qa * ((MLA_NOPE + MLA_ROPE) ** -0.5 * LOG2E)).astype(BF16)

    ckv_n = _rms(ckv_ref[...], kvng_ref[...])
    if not rope:
        ckvn_ref[...] = ckv_n
    krmg = krmg_ref[...]
    lane = lax.broadcasted_iota(jnp.int32, krmg.shape, 1)
    if rope:
        krmg = krmg * ca_ref[:, LANE:] + _swap_halves(krmg, MLA_ROPE // 4) * sa_ref[:, LANE:]
    kr128 = jnp.where(lane < MLA_ROPE, krmg, 0.0)
    _mla_kv(ckv_n, kr128, wukv_ref, ka_ref, va_ref)

    hd = GQA_HEAD_DIM
    gq, gk = gq_ref[...], gk_ref[...]
    for h in range(GQA_HEADS):
        q = _rms(gq[:, h * hd:(h + 1) * hd], gqg_ref[...])
        if rope:
            q = q * cb_ref[...] + _swap_halves(q, hd // 4) * sb_ref[...]
        qb_ref[:, h * hd:(h + 1) * hd] = (q * (hd ** -0.5 * LOG2E)).astype(BF16)
    for h in range(GQA_KV_HEADS):
        k = _rms(gk[:, h * hd:(h + 1) * hd], gkg_ref[...])
        if rope:
            k = k * cb_ref[...] + _swap_halves(k, hd // 4) * sb_ref[...]
        else:
            kbn_ref[:, h * hd:(h + 1) * hd] = k
        kb_ref[:, h * hd:(h + 1) * hd] = k.astype(BF16)
    vb_ref[...] = gv_ref[...].astype(BF16)


def _attn_prep(proj, tables, pr, *, layer, seq, tm_pref=1024):
    n = proj.shape[0]
    rope = tables is not None
    tm = _pick(seq if rope else n, tm_pref)
    per = max(1, seq // tm)

    def pblock(name, width):
        idx = _COL[name] // width
        assert idx * width == _COL[name]
        return pl.BlockSpec((tm, width), lambda i: (i, idx))

    in_specs = [pblock("cq", MLA_Q_LORA), pblock("ckv", MLA_KV_LORA),
                pl.BlockSpec((tm, LANE), lambda i: (i, KRMG_BLOCK)),
                pblock("gq", 512), pblock("gk", 256), pblock("gv", 256)]
    args = [proj] * 6
    if rope:
        in_specs += [pl.BlockSpec((tm, t.shape[1]), lambda i: (i % per, 0)) for t in tables]
        args += list(tables)
    ws = [pr["mla_q_norm"], pr["mla_kv_norm"], pr["mla_w_uq"], pr["mla_w_ukv"], pr["gqa_q_norm"], pr["gqa_k_norm"]]
    in_specs += [_layer_spec(w, layer, 1) for w in ws]
    args += ws
    gkv = GQA_KV_HEADS * GQA_HEAD_DIM
    widths = [(4 * 2 * LANE, BF16), (4 * 2 * LANE, BF16), (4 * MLA_V, BF16),
              (GQA_HEADS * GQA_HEAD_DIM, BF16), (gkv, BF16), (gkv, BF16)]
    if not rope:
        widths += [(MLA_KV_LORA, F32), (gkv, F32)]
    return pl.pallas_call(
        functools.partial(_attn_prep_kernel, rope=rope),
        out_shape=tuple(jax.ShapeDtypeStruct((n, w), dt) for w, dt in widths),
        grid=(n // tm,),
        in_specs=in_specs,
        out_specs=tuple(pl.BlockSpec((tm, w), lambda i: (i, 0)) for w, _ in widths),
        compiler_params=_cparams(("parallel",)),
        name="attn_prep",
    )(*args)


def _cache_kv_kernel(ckv_ref, kr_ref, wukv_ref, ka_ref, va_ref):
    _mla_kv(ckv_ref[...], kr_ref[...], wukv_ref, ka_ref, va_ref)


def _cache_kv(ckv, kr128, wukv, *, layer):
    b, _, p, _ = ckv.shape
    return pl.pallas_call(
        _cache_kv_kernel,
        out_shape=(jax.ShapeDtypeStruct((b, p, 4 * 2 * LANE), BF16), jax.ShapeDtypeStruct((b, p, 4 * MLA_V), BF16)),
        grid=(b,),
        in_specs=[pl.BlockSpec((None, None, p, MLA_KV_LORA), lambda i: (i, layer, 0, 0)),
                  pl.BlockSpec((None, None, p, LANE), lambda i: (i, layer, 0, 0)),
                  _layer_spec(wukv, layer, 1)],
        out_specs=(pl.BlockSpec((None, p, 4 * 2 * LANE), lambda i: (i, 0, 0)),
                   pl.BlockSpec((None, p, 4 * MLA_V), lambda i: (i, 0, 0))),
        compiler_params=_cparams(("parallel",)),
        name="mla_cache_kv",
    )(ckv, kr128, wukv)


def _attn_kernel(q_ref, *refs, nseg, hp, g, dk, dv):
    o_ref = refs[-1]
    for j in range(hp):
        kv = j // g
        q = q_ref[:, j * dk:(j + 1) * dk]
        ks = [refs[2 * i][:, kv * dk:(kv + 1) * dk].astype(BF16) for i in range(nseg)]
        vs = [refs[2 * i + 1][:, kv * dv:(kv + 1) * dv].astype(BF16) for i in range(nseg)]
        ss = [lax.dot_general(q, k, (((1,), (1,)), ((), ())), preferred_element_type=F32) for k in ks]
        m = functools.reduce(jnp.maximum, [jnp.max(s, axis=-1, keepdims=True) for s in ss])
        ps = [jnp.exp2(s - m) for s in ss]
        l = functools.reduce(jnp.add, [jnp.sum(p, axis=-1, keepdims=True) for p in ps])
        o = functools.reduce(jnp.add, [jnp.dot(p.astype(BF16), v, preferred_element_type=F32)
                                       for p, v in zip(ps, vs)])
        o_ref[:, j * dv:(j + 1) * dv] = o / l


def _attention(q, segs, *, heads, kv_heads, dk, dv, tq_pref=512, hp=4):
    b, sq, _ = q.shape
    tq = _pick(sq, tq_pref)
    g = heads // kv_heads
    assert heads % hp == 0 and (hp % g == 0 or g % hp == 0)
    kvp = max(1, hp // g)
    kcol = (lambda hb: hb) if hp >= g else (lambda hb: hb * hp // g)
    in_specs = [pl.BlockSpec((None, tq, hp * dk), lambda bi, hb, i: (bi, i, hb))]
    args = [q]
    for k, v, layer in segs:
        sk = k.shape[-2]
        if layer is None:
            in_specs += [pl.BlockSpec((None, sk, kvp * dk), lambda bi, hb, i: (bi, 0, kcol(hb))),
                         pl.BlockSpec((None, sk, kvp * dv), lambda bi, hb, i: (bi, 0, kcol(hb)))]
        else:
            in_specs += [pl.BlockSpec((None, None, sk, kvp * dk),
                                      lambda bi, hb, i, layer=layer: (bi, layer, 0, kcol(hb))),
                         pl.BlockSpec((None, None, sk, kvp * dv),
                                      lambda bi, hb, i, layer=layer: (bi, layer, 0, kcol(hb)))]
        args += [k, v]
    return pl.pallas_call(
        functools.partial(_attn_kernel, nseg=len(segs), hp=hp, g=min(g, hp), dk=dk, dv=dv),
        out_shape=jax.ShapeDtypeStruct((b, sq, heads * dv), F32),
        grid=(b, heads // hp, sq // tq),
        in_specs=in_specs,
        out_specs=pl.BlockSpec((None, tq, hp * dv), lambda bi, hb, i: (bi, i, hb)),
        compiler_params=_cparams(("parallel", "parallel", "parallel")),
        name="attention",
    )(*args)


def _log_sigmoid(x):
    return jnp.minimum(x, 0.0) - jnp.log1p(jnp.exp(-jnp.abs(x)))


def _mlstm_chunk_operands(q_ref, k_ref, v_ref, gate_ref, bias_ref):
    nh, hd = MLSTM_HEADS, MLSTM_HEAD_DIM
    g = gate_ref[...] + bias_ref[...]
    lf = _log_sigmoid(g)
    lf_hi = lf.astype(BF16)
    rem = lf - lf_hi.astype(F32)
    lf_mid = rem.astype(BF16)
    lf_lo = (rem - lf_mid.astype(F32)).astype(BF16)
    heads = []
    for h in range(nh):
        q = q_ref[:, h * hd:(h + 1) * hd]
        kh = k_ref[:, h * hd:(h + 1) * hd] * (hd ** -0.5)
        kt = kh.T
        qh = q.astype(BF16)
        heads.append(dict(q=q, qh=qh, kh=kh, kt=kt, vh=v_ref[:, h * hd:(h + 1) * hd].astype(BF16),
                          s=jnp.dot(qh, kt.astype(BF16), preferred_element_type=F32)))
    return dict(g=g, gt=g.T, lf=(lf_hi, lf_mid, lf_lo), heads=heads)


def _prefix_max_lanes(x, rev):
    n = x.shape[1]
    lane = lax.broadcasted_iota(jnp.int32, x.shape, 1)
    k = 1
    while k < n:
        if rev:
            shifted = jnp.where(lane < n - k, pltpu.roll(x, n - k, axis=1), -jnp.inf)
        else:
            shifted = jnp.where(lane >= k, pltpu.roll(x, k, axis=1), -jnp.inf)
        x = jnp.maximum(x, shifted)
        k *= 2
    return x


def _mlstm_chain(ops, h_ref, c_s, n_s, m_s, *, direction):
    nh, hd = MLSTM_HEADS, MLSTM_HEAD_DIM
    rev = direction == 1
    g, gt = ops["g"], ops["gt"]
    cl = g.shape[0]
    ri = lax.broadcasted_iota(jnp.int32, (cl, cl), 0)
    ci = lax.broadcasted_iota(jnp.int32, (cl, cl), 1)
    mask = (ri <= ci) if rev else (ri >= ci)
    tri = jnp.where(mask, 1.0, 0.0).astype(BF16)
    cum = functools.reduce(jnp.add, [jnp.dot(tri, part, preferred_element_type=F32) for part in ops["lf"]])
    cumt = cum.T
    last = 0 if rev else cl - 1
    li0 = MG_LANE + direction * 2 * nh
    lf0 = li0 + nh
    assert li0 % SUBLANE == 0 and 2 * nh == SUBLANE

    b8 = pltpu.roll(gt[li0:li0 + SUBLANE, :], nh, axis=0) - cumt[li0:li0 + SUBLANE, :]
    pm8 = _prefix_max_lanes(b8, rev)
    pm_rows = jnp.concatenate([jnp.zeros((li0, cl), F32), pm8, jnp.zeros((LANE - li0 - SUBLANE, cl), F32)], axis=0)
    pm_c = pm_rows.T
    lane = lax.broadcasted_iota(jnp.int32, (1, LANE), 1)
    m_lanes = functools.reduce(jnp.add, [jnp.where(lane == lf0 + h, m_s[direction, h:h + 1, :], 0.0)
                                         for h in range(nh)])
    mx_all = jnp.maximum(m_lanes, pm_c)
    interw_all = jnp.exp(m_lanes - mx_all)
    em_all = jnp.exp(-(cum + mx_all))

    for h in range(nh):
        hv = ops["heads"][h]
        cum_c = cum[:, lf0 + h:lf0 + h + 1]
        cum_r = cumt[lf0 + h:lf0 + h + 1, :]
        li_r = gt[li0 + h:li0 + h + 1, :]
        li_c = g[:, li0 + h:li0 + h + 1]
        m_prev = m_s[direction, h:h + 1, 0:1]
        w = jnp.where(mask, jnp.exp(b8[nh + h:nh + h + 1, :] - mx_all[:, lf0 + h:lf0 + h + 1]), 0.0)
        sc = hv["s"] * w
        inter_w = interw_all[:, lf0 + h:lf0 + h + 1]
        cmat = c_s[direction, h]
        nrow = n_s[direction, h:h + 1, :]
        num = (jnp.dot(sc.astype(BF16), hv["vh"], preferred_element_type=F32)
               + inter_w * jnp.dot(hv["qh"], cmat.astype(BF16), preferred_element_type=F32))
        qn = jnp.sum(hv["q"] * nrow, axis=1, keepdims=True)
        den = jnp.sum(sc, axis=1, keepdims=True) + inter_w * qn
        h_ref[:, h * hd:(h + 1) * hd] = num / jnp.maximum(jnp.abs(den), em_all[:, lf0 + h:lf0 + h + 1])

        tot = cum_c[last:last + 1, :]
        g_r = tot - cum_r + li_r
        g_c = tot - cum_c + li_c
        m_new = jnp.maximum(tot + m_prev, jnp.max(g_r, axis=1, keepdims=True))
        decay = jnp.exp(tot + m_prev - m_new)
        ws_r = jnp.exp(g_r - m_new)
        ws_c = jnp.exp(g_c - m_new)
        c_s[direction, h] = decay * cmat + jnp.dot((hv["kt"] * ws_r).astype(BF16), hv["vh"],
                                                   preferred_element_type=F32)
        n_s[direction, h:h + 1, :] = decay * nrow + jnp.sum(hv["kh"] * ws_c, axis=0, keepdims=True)
        m_s[direction, h:h + 1, :] = jnp.broadcast_to(m_new, (1, hd))


def _mlstm_kernel(*refs, shared, has_init, emit_state):
    nin = 4 if shared else 8
    fwd_in = refs[:4]
    bwd_in = fwd_in if shared else refs[4:8]
    bias_ref = refs[nin]
    pos = nin + 1
    if has_init:
        c0_ref, n0_ref, m0_ref = refs[pos:pos + 3]
        pos += 3
    hf_ref, hb_ref = refs[pos:pos + 2]
    pos += 2
    if emit_state:
        co_ref, no_ref, mo_ref = refs[pos:pos + 3]
        pos += 3
    c_s, n_s, m_s = refs[pos:pos + 3]
    c = pl.program_id(1)

    @pl.when(c == 0)
    def _():
        if has_init:
            c_s[...] = c0_ref[...]
            n_s[...] = n0_ref[...]
            m_s[...] = m0_ref[...]
        else:
            c_s[...] = jnp.zeros_like(c_s)
            n_s[...] = jnp.zeros_like(n_s)
            m_s[...] = jnp.zeros_like(m_s)

    ops_f = _mlstm_chunk_operands(*fwd_in, bias_ref)
    ops_b = ops_f if shared else _mlstm_chunk_operands(*bwd_in, bias_ref)
    _mlstm_chain(ops_f, hf_ref, c_s, n_s, m_s, direction=0)
    _mlstm_chain(ops_b, hb_ref, c_s, n_s, m_s, direction=1)

    if emit_state:
        @pl.when(c == pl.num_programs(1) - 1)
        def _():
            co_ref[...] = c_s[...]
            no_ref[...] = n_s[...]
            mo_ref[...] = m_s[...]


def _mlstm(proj, bias128, init, *, layer, batch, seq, emit_state, chunk_pref=256):
    n = proj.shape[0]
    cl = _pick(seq, chunk_pref)
    nc = seq // cl
    shared = nc == 1
    nh, hd = MLSTM_HEADS, MLSTM_HEAD_DIM
    rows = (lambda b, c: b * nc + c), (lambda b, c: b * nc + (nc - 1 - c))

    def chunk_specs(row):
        qkv = lambda name: pl.BlockSpec((cl, nh * hd), lambda b, c: (row(b, c), _COL[name] // (nh * hd)))
        return [qkv("mq"), qkv("mk"), qkv("mv"), pl.BlockSpec((cl, LANE), lambda b, c: (row(b, c), KRMG_BLOCK))]

    in_specs = chunk_specs(rows[0]) + ([] if shared else chunk_specs(rows[1]))
    in_specs.append(pl.BlockSpec((None, 1, LANE), lambda b, c: (layer, 0, 0)))
    args = [proj] * (len(in_specs) - 1) + [bias128]
    has_init = init is not None
    if has_init:
        c0, n0, m0 = init
        in_specs += [pl.BlockSpec((None, None, 2, nh, hd, hd), lambda b, c: (b, layer, 0, 0, 0, 0)),
                     pl.BlockSpec((None, None, 2, nh, hd), lambda b, c: (b, layer, 0, 0, 0)),
                     pl.BlockSpec((None, None, 2, nh, hd), lambda b, c: (b, layer, 0, 0, 0))]
        args += [c0, n0, m0]
    out_shape = [jax.ShapeDtypeStruct((n, nh * hd), F32)] * 2
    out_specs = [pl.BlockSpec((cl, nh * hd), lambda b, c, row=row: (row(b, c), 0)) for row in rows]
    if emit_state:
        out_shape += [jax.ShapeDtypeStruct((batch, 2, nh, hd, hd), F32), jax.ShapeDtypeStruct((batch, 2, nh, hd), F32),
                      jax.ShapeDtypeStruct((batch, 2, nh, hd), F32)]
        out_specs += [pl.BlockSpec((None, 2, nh, hd, hd), lambda b, c: (b, 0, 0, 0, 0)),
                      pl.BlockSpec((None, 2, nh, hd), lambda b, c: (b, 0, 0, 0)),
                      pl.BlockSpec((None, 2, nh, hd), lambda b, c: (b, 0, 0, 0))]
    return pl.pallas_call(
        functools.partial(_mlstm_kernel, shared=shared, has_init=has_init, emit_state=emit_state),
        out_shape=tuple(out_shape),
        grid=(batch, nc),
        in_specs=in_specs,
        out_specs=tuple(out_specs),
        scratch_shapes=[pltpu.VMEM((2, nh, hd, hd), F32), pltpu.VMEM((2, nh, hd), F32), pltpu.VMEM((2, nh, hd), F32)],
        compiler_params=_cparams(("parallel", "arbitrary")),
        name="mlstm",
    )(*args)


_S5_STATE_TILES = 2 * S5_NS // MXU_TILE
_S5_GROUPS_PER_TILE = MXU_TILE // S5_STATE


def _s5_channel_tile(state_tile):
    first_group = (state_tile % (S5_NS // MXU_TILE)) * _S5_GROUPS_PER_TILE
    return first_group * S5_GROUP // MXU_TILE


def _to_time_major_kernel(u_ref, p_ref, o_ref):
    rows = p_ref.shape[0]
    sg, ch = u_ref.shape[0], u_ref.shape[2]
    tsteps = rows // sg
    for k in range(u_ref.shape[1] // tsteps):
        u = u_ref[:, k * tsteps:(k + 1) * tsteps, :].reshape(rows, ch).astype(BF16)
        o_ref[k * rows:(k + 1) * rows, :] = jnp.dot(p_ref[...], u, preferred_element_type=F32).astype(BF16)


def _from_time_major_kernel(yf_ref, yb_ref, pt_ref, o_ref):
    rows = pt_ref.shape[0]
    sg, ch = o_ref.shape[0], o_ref.shape[2]
    tsteps = rows // sg
    pt = pt_ref[...]
    for k in range(o_ref.shape[1] // tsteps):
        y = yf_ref[k * rows:(k + 1) * rows, :] + yb_ref[k * rows:(k + 1) * rows, :]
        hi = y.astype(BF16)
        lo = (y - hi.astype(F32)).astype(BF16)
        out = jnp.dot(pt, hi, preferred_element_type=F32) + jnp.dot(pt, lo, preferred_element_type=F32)
        o_ref[:, k * tsteps:(k + 1) * tsteps, :] = out.reshape(sg, tsteps, ch)


def _s5_kernel(*refs, has_init):
    if has_init:
        u_ref, bd_ref, a_ref, cd_ref, x0_ref, y_ref, xf_ref, bu_s, x_s = refs
    else:
        u_ref, bd_ref, a_ref, cd_ref, y_ref, xf_ref, bu_s, x_s = refs
    d = pl.program_id(0)
    c = pl.program_id(2)
    ns = S5_NS
    sg = x_s.shape[0]
    tsteps = u_ref.shape[0] // sg
    mt = MXU_TILE

    @pl.when(c == 0)
    def _():
        if has_init:
            x_s[...] = x0_ref[...]
        else:
            x_s[...] = jnp.zeros_like(x_s)

    u = u_ref[...]
    for st in range(_S5_STATE_TILES):
        ct = _s5_channel_tile(st)
        bu_s[:, st * mt:(st + 1) * mt] = jnp.dot(u[:, ct * mt:(ct + 1) * mt],
                                                 bd_ref[ct * mt:(ct + 1) * mt, st * mt:(st + 1) * mt],
                                                 preferred_element_type=F32)
    a_re = jnp.broadcast_to(a_ref[:, :ns], (sg, ns))
    a_im = jnp.broadcast_to(a_ref[:, ns:], (sg, ns))

    def body(t, carry):
        xr, xi = carry
        tt = t + d * (tsteps - 1 - 2 * t)
        r0 = pl.multiple_of(tt * sg, sg)
        br = bu_s[pl.ds(r0, sg), :ns]
        bi = bu_s[pl.ds(r0, sg), ns:]
        nr = a_re * xr - a_im * xi + br
        ni = a_re * xi + a_im * xr + bi
        bu_s[pl.ds(r0, sg), :ns] = nr
        bu_s[pl.ds(r0, sg), ns:] = ni
        return nr, ni

    xr, xi = lax.fori_loop(0, tsteps, body, (x_s[:, :ns], x_s[:, ns:]), unroll=SCAN_UNROLL)
    x_s[:, :ns] = xr
    x_s[:, ns:] = xi

    for ct in range(S5_CH // mt):
        acc = None
        for st in range(_S5_STATE_TILES):
            if _s5_channel_tile(st) != ct:
                continue
            t = jnp.dot(bu_s[:, st * mt:(st + 1) * mt].astype(BF16),
                        cd_ref[st * mt:(st + 1) * mt, ct * mt:(ct + 1) * mt], preferred_element_type=F32)
            acc = t if acc is None else acc + t
        y_ref[:, ct * mt:(ct + 1) * mt] = acc

    @pl.when(c == pl.num_programs(2) - 1)
    def _():
        xf_ref[...] = x_s[...]


def _s5(proj3, pr, x0, *, layer, rows_pref=512, scan_rows_pref=1024):
    batch, seq, _ = proj3.shape
    sg = SUBLANE if batch % SUBLANE == 0 else batch
    ng = batch // sg
    tsteps = _pick(seq, max(SUBLANE, rows_pref // sg))
    rows = sg * tsteps
    nc = seq // tsteps
    r = jnp.arange(rows)
    perm = (jnp.arange(rows)[None, :] == ((r % sg) * tsteps + r // sg)[:, None]).astype(BF16)
    const2 = lambda a: pl.BlockSpec(a.shape, lambda g, c: (0,) * a.ndim, pipeline_mode=pl.Buffered(1))
    nsub = math.gcd(nc, RELAYOUT_SUB)
    nc_r = nc // nsub

    u_tm = pl.pallas_call(
        _to_time_major_kernel,
        out_shape=jax.ShapeDtypeStruct((ng, seq * sg, S5_CH), BF16),
        grid=(ng, nc_r),
        in_specs=[pl.BlockSpec((sg, nsub * tsteps, S5_CH), lambda g, c: (g, c, _COL["su"] // S5_CH)), const2(perm)],
        out_specs=pl.BlockSpec((None, nsub * rows, S5_CH), lambda g, c: (g, c, 0)),
        compiler_params=_cparams(("parallel", "parallel")),
        name="s5_to_time_major",
    )(proj3, perm)

    ts_scan = _pick(seq, max(SUBLANE, scan_rows_pref // sg))
    rows_scan = sg * ts_scan
    nc_scan = seq // ts_scan
    cpos = lambda d, c: c + d * (nc_scan - 1 - 2 * c)
    has_init = x0 is not None
    in_specs = [pl.BlockSpec((None, rows_scan, S5_CH), lambda d, g, c: (g, cpos(d, c), 0)),
                pl.BlockSpec((None, None, S5_CH, 2 * S5_NS), lambda d, g, c: (layer, d, 0, 0)),
                pl.BlockSpec((None, None, 1, 2 * S5_NS), lambda d, g, c: (layer, d, 0, 0)),
                _layer_spec(pr["s5_cd"], layer, 3)]
    args = [u_tm, pr["s5_bd"], pr["s5_abar"], pr["s5_cd"]]
    if has_init:
        in_specs.append(pl.BlockSpec((None, sg, 2 * S5_NS), lambda d, g, c: (d, g, 0)))
        args.append(x0)
    y_tm, xfin = pl.pallas_call(
        functools.partial(_s5_kernel, has_init=has_init),
        out_shape=(jax.ShapeDtypeStruct((2, ng, seq * sg, S5_CH), F32),
                   jax.ShapeDtypeStruct((2, batch, 2 * S5_NS), F32)),
        grid=(2, ng, nc_scan),
        in_specs=in_specs,
        out_specs=(pl.BlockSpec((None, None, rows_scan, S5_CH), lambda d, g, c: (d, g, cpos(d, c), 0)),
                   pl.BlockSpec((None, sg, 2 * S5_NS), lambda d, g, c: (d, g, 0))),
        scratch_shapes=[pltpu.VMEM((rows_scan, 2 * S5_NS), F32), pltpu.VMEM((sg, 2 * S5_NS), F32)],
        compiler_params=_cparams(("parallel", "parallel", "arbitrary")),
        name="s5_scan",
    )(*args)

    y = pl.pallas_call(
        _from_time_major_kernel,
        out_shape=jax.ShapeDtypeStruct((batch, seq, S5_CH), F32),
        grid=(ng, nc_r),
        in_specs=[pl.BlockSpec((None, None, nsub * rows, S5_CH), lambda g, c: (0, g, c, 0)),
                  pl.BlockSpec((None, None, nsub * rows, S5_CH), lambda g, c: (1, g, c, 0)),
                  const2(perm)],
        out_specs=pl.BlockSpec((sg, nsub * tsteps, S5_CH), lambda g, c: (g, c, 0)),
        compiler_params=_cparams(("parallel", "parallel")),
        name="s5_from_time_major",
    )(y_tm, y_tm, perm.T)
    return y, xfin


def _gelu_tanh(x):
    return 0.5 * x * (1.0 + jnp.tanh(math.sqrt(2.0 / math.pi) * (x + 0.044715 * (x * x * x))))


def _outproj_kernel(x_ref, mod_ref, oa_ref, ob_ref, hf_ref, hb_ref, mo_ref, y_ref, su_ref,
                    d_ref, wglu_ref, on_ref, wout_ref, o_ref):
    slab = math.gcd(x_ref.shape[0], OUTPROJ_SLAB)
    for r in range(x_ref.shape[0] // slab):
        rows = slice(r * slab, (r + 1) * slab)
        oc = jax.nn.sigmoid(mo_ref[rows, :]) * (hf_ref[rows, :] + hb_ref[rows, :])
        y = _gelu_tanh(y_ref[rows, :] + d_ref[...] * su_ref[rows, :])
        od = y * jax.nn.sigmoid(jnp.dot(y.astype(BF16), wglu_ref[...], preferred_element_type=F32))
        acc = None
        for gi, part in enumerate((oa_ref[rows, :], ob_ref[rows, :], oc, od)):
            nrm = _rms(part, on_ref[gi:gi + 1, :]).astype(BF16)
            t = jnp.dot(nrm, wout_ref[gi * GROUP_WIDTH:(gi + 1) * GROUP_WIDTH, :], preferred_element_type=F32)
            acc = t if acc is None else acc + t
        o_ref[rows, :] = x_ref[rows, :] + mod_ref[5:6, :] * acc


def _outproj(x, mod, oa, ob, hf, hb, proj, y, pr, *, layer, seq, tm_pref=512):
    n, d = x.shape
    groups = mod.shape[0]
    tm = _pick(seq if groups > 1 else n, tm_pref)
    per = seq // tm if groups > 1 else 1
    mod_map = (lambda i: (i // per, 0, 0)) if groups > 1 else (lambda i: (0, 0, 0))
    gw = GROUP_WIDTH
    rowblk = pl.BlockSpec((tm, gw), lambda i: (i, 0))
    ws = [pr["s5_d"], pr["s5_w_glu"], pr["out_norm"], pr["w_out"]]
    return pl.pallas_call(
        _outproj_kernel,
        out_shape=jax.ShapeDtypeStruct((n, d), F32),
        grid=(n // tm,),
        in_specs=[pl.BlockSpec((tm, d), lambda i: (i, 0)),
                  pl.BlockSpec((None, N_MOD, d), mod_map),
                  rowblk, rowblk, rowblk, rowblk,
                  pl.BlockSpec((tm, gw), lambda i: (i, _COL["mo"] // gw)),
                  rowblk,
                  pl.BlockSpec((tm, gw), lambda i: (i, _COL["su"] // gw))] + [_layer_spec(w, layer, 1) for w in ws],
        out_specs=pl.BlockSpec((tm, d), lambda i: (i, 0)),
        compiler_params=_cparams(("parallel",)),
        name="merge_out_proj",
    )(x, mod, oa, ob, hf, hb, proj, y, proj, *ws)


def _trunk_layer(x, mod, pr, *, layer, batch, seq, tables, cache, final_g=None):
    latent = cache is not None
    x = _ffn(x, mod, pr, layer=layer, which=0, seq=seq)
    proj = _inproj(x, mod, pr, layer=layer, seq=seq)

    prep = _attn_prep(proj, tables, pr, layer=layer, seq=seq)
    qa, ka, va, qb, kb, vb = prep[:6]
    r3 = lambda a: a.reshape(batch, seq, a.shape[-1])
    segs_a = [(r3(ka), r3(va), None)]
    segs_b = [(r3(kb), r3(vb), None)]
    if latent:
        segs_a.insert(0, cache["mla_kv"] + (None,))
        segs_b.insert(0, cache["gqa_kv"] + (layer,))
    oa = _attention(r3(qa), segs_a, heads=MLA_HEADS, kv_heads=MLA_HEADS, dk=2 * LANE, dv=MLA_V)
    ob = _attention(r3(qb), segs_b, heads=GQA_HEADS, kv_heads=GQA_KV_HEADS, dk=GQA_HEAD_DIM, dv=GQA_HEAD_DIM)
    oa = oa.reshape(batch * seq, -1)
    ob = ob.reshape(batch * seq, -1)

    m_init = cache["mlstm"] if latent else None
    mres = _mlstm(proj, pr["mlstm_bias"], m_init, layer=layer, batch=batch, seq=seq, emit_state=not latent)

    y, xfin = _s5(proj.reshape(batch, seq, PROJ_COLS), pr, cache["s5"] if latent else None, layer=layer)

    x = _outproj(x, mod, oa, ob, mres[0], mres[1], proj, y.reshape(batch * seq, S5_CH), pr,
                 layer=layer, seq=seq)
    x = _ffn(x, mod, pr, layer=layer, which=1, seq=seq, final_g=final_g)

    new_ctx = None
    if not latent:
        ckvn, kbn = prep[6:]
        kr = proj[:, _COL["kr"]:_COL["kr"] + MLA_ROPE]
        gv = proj[:, _COL["gv"]:_COL["gv"] + GQA_KV_HEADS * GQA_HEAD_DIM]
        xs = xfin.reshape(2, batch, 2, S5_GROUPS, S5_STATE).transpose(1, 0, 2, 3, 4)
        new_ctx = (ckvn.reshape(batch, seq, MLA_KV_LORA),
                   kr.reshape(batch, seq, MLA_ROPE),
                   kbn.reshape(batch, seq, GQA_KV_HEADS, GQA_HEAD_DIM),
                   gv.reshape(batch, seq, GQA_KV_HEADS, GQA_HEAD_DIM),
                   mres[2], mres[3], mres[4][..., 0],
                   xs[:, :, 0], xs[:, :, 1])
    return x, new_ctx


def _permute_w_in(w_in):
    parts = [w_in[..., _ORIG[n][0]:_ORIG[n][0] + _ORIG[n][1]] for n in _ORDER]
    pad = PROJ_COLS - sum(p.shape[-1] for p in parts)
    parts.append(jnp.zeros(w_in.shape[:-1] + (pad,), w_in.dtype))
    return jnp.concatenate(parts, axis=-1).astype(BF16)


def _permute_w_uq(w_uq):
    depth, k, _ = w_uq.shape
    w = w_uq.reshape(depth, k, MLA_HEADS, MLA_NOPE + MLA_ROPE)
    w = jnp.pad(w, ((0, 0), (0, 0), (0, 0), (0, 2 * LANE - MLA_NOPE - MLA_ROPE)))
    return w.reshape(depth, k, MLA_HEADS * 2 * LANE).astype(BF16)


def kernel(x_prompt, x_sample, cache_mla_ckv, cache_mla_krope, cache_gqa_k, cache_gqa_v, state_mlstm_c, state_mlstm_n, state_mlstm_m, state_s5_re, state_s5_im, c, c_ctx, ada_w, ada_b, norm_g, ffn_w13, ffn_w2, w_in, mla_q_norm, mla_kv_norm, mla_w_uq, mla_w_ukv, gqa_q_norm, gqa_k_norm, mlstm_gate_b, s5_a_re, s5_a_im, s5_log_dt, s5_b_re, s5_b_im, s5_c_re, s5_c_im, s5_d, s5_w_glu, out_norm, w_out, final_norm):
    bc, sc, d = x_prompt.shape
    bl, sl, _ = x_sample.shape
    depth = ada_w.shape[0]
    past = cache_mla_ckv.shape[2]

    rows = ((1 + bl + SUBLANE - 1) // SUBLANE) * SUBLANE
    cvecs = jnp.concatenate([c_ctx[None, :], c, jnp.zeros((rows - 1 - bl, d), F32)], axis=0)
    mod_all = _modulation(cvecs, ada_w, ada_b).reshape(depth, rows, N_MOD, d)

    tables = _rope_tables(sl)

    abar, bd, cd = _s5_params(s5_a_re, s5_a_im, s5_log_dt, s5_b_re, s5_b_im, s5_c_re, s5_c_im)
    nmg = MLSTM_HEADS * 4
    row = lambda a: a.reshape(depth, 1, a.shape[-1])
    pr = {"norm_g": norm_g.reshape(depth, 3, 1, d),
          "ffn_w13": ffn_w13.astype(BF16), "ffn_w2": ffn_w2,
          "w_in": _permute_w_in(w_in),
          "mla_q_norm": row(mla_q_norm), "mla_kv_norm": row(mla_kv_norm),
          "mla_w_uq": _permute_w_uq(mla_w_uq), "mla_w_ukv": mla_w_ukv.astype(BF16),
          "gqa_q_norm": row(gqa_q_norm), "gqa_k_norm": row(gqa_k_norm),
          "mlstm_bias": jnp.pad(mlstm_gate_b.reshape(depth, 1, nmg), ((0, 0), (0, 0), (MG_LANE, LANE - MG_LANE - nmg))),
          "s5_abar": abar, "s5_bd": bd, "s5_cd": cd,
          "s5_d": row(s5_d), "s5_w_glu": s5_w_glu.astype(BF16),
          "out_norm": out_norm.reshape(depth, 4, GROUP_WIDTH), "w_out": w_out.astype(BF16)}

    gkv = GQA_KV_HEADS * GQA_HEAD_DIM
    cache_k = cache_gqa_k.reshape(bl, depth, past, gkv)
    cache_v = cache_gqa_v.reshape(bl, depth, past, gkv)
    m0 = jnp.broadcast_to(state_mlstm_m[..., None], state_mlstm_n.shape)
    kr_pad = jnp.pad(cache_mla_krope, ((0, 0), (0, 0), (0, 0), (0, LANE - MLA_ROPE)))

    x_ctx = x_prompt.reshape(bc * sc, d)
    x_lat = x_sample.reshape(bl * sl, d)
    per_layer = []
    for l in range(depth):
        final_g = final_norm[None, :] if l == depth - 1 else None
        x_ctx, ctx_l = _trunk_layer(x_ctx, mod_all[l, 0:1], pr, layer=l, batch=bc, seq=sc, tables=None, cache=None,
                                    final_g=final_g)
        per_layer.append(ctx_l)

        x0 = jnp.concatenate([state_s5_re[:, l].reshape(bl, 2, S5_NS), state_s5_im[:, l].reshape(bl, 2, S5_NS)],
                             axis=-1).transpose(1, 0, 2)
        cache = {"mla_kv": tuple(_cache_kv(cache_mla_ckv, kr_pad, pr["mla_w_ukv"], layer=l)),
                 "gqa_kv": (cache_k, cache_v),
                 "mlstm": (state_mlstm_c, state_mlstm_n, m0),
                 "s5": x0}
        x_lat, _ = _trunk_layer(x_lat, mod_all[l, 1:1 + bl], pr, layer=l, batch=bl, seq=sl, tables=tables,
                                cache=cache, final_g=final_g)

    new_ctx = [jnp.stack([t[i] for t in per_layer], axis=1) for i in range(9)]
    return (x_ctx.reshape(bc, sc, d), x_lat.reshape(bl, sl, d), *new_ctx)
```
